```python
import jax, jax.numpy as jnp
from jax import lax
import numpy as np

D_MODEL = 1024
BATCH = 4
SEQ = 8192
DEPTH = 2
DEC_BATCH = 32
DEC_SEQ = 4
PAST_LEN = 16384
PAGE_SIZE = 128

H_A = 4
DH_A = 64
H_B = 4
DK_B = 128
DV_B = 128
HGRN_CHUNK = 64
LOG_F_MIN = -30.0
H_C = 4
DH_C = 64
H_IDX = 4
D_IDX = 64
TOPK_MAX = 256
ROPE_THETA = 10000.0
D_FF = 128 * ((8 * D_MODEL // 3 + 127) // 128)
CONV_W = 3
N_BRANCH = 3
Q_BLOCK = 128
EPS = 1e-6
NEG_BIG = -1e30

SPLIT_SIZES = (H_A * DH_A, H_A * DH_A, H_A * DH_A,
               H_B * DK_B, H_B * DK_B, H_B * DV_B, H_B * DV_B,
               H_C * DH_C, H_C * DH_C, H_C * DH_C,
               H_IDX * D_IDX, D_IDX, H_IDX,
               N_BRANCH * D_MODEL)
N_IN = int(sum(SPLIT_SIZES))
SPLIT_IDX = tuple(int(v) for v in np.cumsum(SPLIT_SIZES)[:-1])

kernel_name = 'hybrid_sb_hgrn2_dsa_convffn_step'

F32 = jnp.float32


def rmsnorm(x, g):
    xf = x.astype(F32)
    y = xf * lax.rsqrt(jnp.mean(xf * xf, axis=-1, keepdims=True) + EPS) * g.astype(F32)
    return y.astype(x.dtype)


def rope(x, pos):
    half = x.shape[-1] // 2
    inv_freq = ROPE_THETA ** (-jnp.arange(half, dtype=F32) / half)
    ang = pos.astype(F32)[:, None] * inv_freq[None, :]
    cos = jnp.cos(ang)[None, :, None, :]
    sin = jnp.sin(ang)[None, :, None, :]
    xf = x.astype(F32)
    x1, x2 = xf[..., :half], xf[..., half:]
    return jnp.concatenate([x1 * cos - x2 * sin, x2 * cos + x1 * sin], axis=-1).astype(x.dtype)


def over_query_blocks(fn, qs, q_pos):
    L = q_pos.shape[0]
    if L <= Q_BLOCK or L % Q_BLOCK:
        return fn(qs, q_pos)
    nb = L // Q_BLOCK
    split = lambda a: jnp.moveaxis(a.reshape(a.shape[0], nb, Q_BLOCK, *a.shape[2:]), 1, 0)
    out = lax.map(lambda xs: fn(xs[0], xs[1]), (tuple(split(a) for a in qs), q_pos.reshape(nb, Q_BLOCK)))
    out = jnp.moveaxis(out, 0, 1)
    return out.reshape(out.shape[0], L, *out.shape[3:])


def stick_breaking(q, k, v, q_pos, k_pos):
    z = jnp.einsum('bqhd,bkhd->bhqk', q, k).astype(F32) * DH_A ** -0.5
    visible = (k_pos[None, :] < q_pos[:, None])[None, None]
    log_skip = jnp.where(visible, jax.nn.log_sigmoid(-z), 0.0)
    between = lax.cumsum(log_skip, axis=3, reverse=True) - log_skip
    a = jnp.where(visible, jnp.exp(jax.nn.log_sigmoid(z) + between), 0.0)
    return jnp.einsum('bhqk,bkhd->bqhd', a.astype(v.dtype), v)


def dsa_attend(q, qi, wi, q_pos, k, v, ki, k_pos):
    n_sel = min(TOPK_MAX, k.shape[1] // 4)
    visible = k_pos[None, :] <= q_pos[:, None]
    dots = jax.nn.relu(jnp.einsum('bqjd,bkd->bqjk', qi, ki).astype(F32) * D_IDX ** -0.5)
    score = jnp.einsum('bqj,bqjk->bqk', wi.astype(F32), dots)
    score = jnp.where(visible[None], score, NEG_BIG)
    _, idx = lax.top_k(score, n_sel)
    valid = k_pos[idx] <= q_pos[None, :, None]
    take = jax.vmap(lambda rows, ids: rows[ids])
    k_sel = take(k, idx)
    v_sel = take(v, idx)
    logits = jnp.einsum('bqhd,bqnhd->bhqn', q, k_sel).astype(F32) * DH_C ** -0.5
    logits = jnp.where(valid[:, None], logits, NEG_BIG)
    p = jax.nn.softmax(logits, axis=-1)
    return jnp.einsum('bhqn,bqnhd->bqhd', p.astype(v.dtype), v_sel)


def hgrn2_chunked(q, log_f, k, v, s0):
    B, L, H, _ = q.shape
    C = HGRN_CHUNK if L % HGRN_CHUNK == 0 else L
    nc = L // C
    chunks = lambda a: jnp.transpose(a.astype(F32).reshape(B, nc, C, H, a.shape[-1]), (1, 0, 3, 2, 4))
    causal = jnp.tril(jnp.ones((C, C), dtype=bool))[:, :, None]

    def step(S, inp):
        qc, gc, kc, vc = inp
        cum = lax.cumsum(gc, axis=2)
        o_inter = jnp.einsum('bhtk,bhkv->bhtv', qc * jnp.exp(cum), S)
        diff = cum[:, :, :, None, :] - cum[:, :, None, :, :]
        decay = jnp.where(causal, jnp.exp(jnp.where(causal, diff, 0.0)), 0.0)
        scores = jnp.einsum('bhtk,bhtsk,bhsk->bhts', qc, decay, kc)
        o = o_inter + jnp.einsum('bhts,bhsv->bhtv', scores, vc)
        last = cum[:, :, -1]
        S = jnp.exp(last)[..., None] * S + jnp.einsum('bhsk,bhsv->bhkv', kc * jnp.exp(last[:, :, None] - cum), vc)
        return S, o

    S, o = lax.scan(step, s0.astype(F32), (chunks(q), chunks(log_f), chunks(k), chunks(v)))
    o = jnp.transpose(o, (1, 0, 3, 2, 4)).reshape(B, L, H, -1)
    return o, S


def gather_pages(pool, page_table):
    rows = pool[page_table]
    return rows.reshape(rows.shape[0], -1, *rows.shape[3:])


def hybrid_layer(x, kv_a_past, kv_c_past, kidx_past, s_prev, conv_prev,
                 g_mix, w_in, lb, g_hn, w_br_a, w_br_b, w_br_c, w_out,
                 g_ffn, w_up, conv_w, conv_b, w_down):
    B, L, _ = x.shape
    P = kv_a_past.shape[1]
    q_pos = P + jnp.arange(L, dtype=jnp.int32)
    k_pos = jnp.arange(P + L, dtype=jnp.int32)

    h = rmsnorm(x, g_mix)
    (qa, ka, va, qh, fh, ih, gh, qc, kc, vc, qi, ki, wi, gates) = jnp.split(h @ w_in, SPLIT_IDX, axis=-1)
    heads = lambda t, n: t.reshape(B, L, n, -1)

    kv_a_new = jnp.stack([heads(ka, H_A), heads(va, H_A)], axis=2)
    kv_a_all = jnp.concatenate([kv_a_past.astype(x.dtype), kv_a_new], axis=1)
    k_a, v_a = kv_a_all[:, :, 0], kv_a_all[:, :, 1]
    o_a = over_query_blocks(lambda qs, pb: stick_breaking(qs[0], k_a, v_a, pb, k_pos), (heads(qa, H_A),), q_pos)

    lb_h = lb.astype(F32).reshape(H_B, DK_B)
    k_h = (1.0 - lb_h) * jax.nn.sigmoid(-heads(fh, H_B).astype(F32))
    log_f = jnp.maximum(jnp.log1p(-k_h), LOG_F_MIN)
    o_h, s_new = hgrn2_chunked(jax.nn.silu(heads(qh, H_B)), log_f, k_h, heads(ih, H_B), s_prev)
    o_h = rmsnorm(o_h, g_hn) * jax.nn.silu(heads(gh, H_B).astype(F32))
    o_b = o_h.reshape(B, L, -1).astype(x.dtype)

    qc_r = rope(heads(qc, H_C), q_pos)
    kc_r = rope(heads(kc, H_C), q_pos)
    qi_r = rope(heads(qi, H_IDX), q_pos)
    ki_r = rope(ki[:, :, None, :], q_pos)[:, :, 0]
    kv_c_new = jnp.stack([kc_r, heads(vc, H_C)], axis=2)
    kv_c_all = jnp.concatenate([kv_c_past.astype(x.dtype), kv_c_new], axis=1)
    k_c, v_c = kv_c_all[:, :, 0], kv_c_all[:, :, 1]
    ki_all = jnp.concatenate([kidx_past.astype(x.dtype), ki_r], axis=1)
    w_idx = wi * H_IDX ** -0.5
    o_c = over_query_blocks(lambda qs, pb: dsa_attend(qs[0], qs[1], qs[2], pb, k_c, v_c, ki_all, k_pos),
                            (qc_r, qi_r, w_idx), q_pos)

    gate = jax.nn.sigmoid(gates.reshape(B, L, N_BRANCH, D_MODEL))
    merged = (gate[:, :, 0] * (o_a.reshape(B, L, -1) @ w_br_a)
              + gate[:, :, 1] * (o_b @ w_br_b)
              + gate[:, :, 2] * (o_c.reshape(B, L, -1) @ w_br_c))
    x = x + merged @ w_out

    u = rmsnorm(x, g_ffn) @ w_up
    buf = jnp.concatenate([conv_prev.astype(u.dtype), u], axis=1)
    c = conv_b + sum(conv_w[j] * buf[:, j:j + L] for j in range(CONV_W))
    a, b = jnp.split(c, 2, axis=-1)
    x = x + (jax.nn.silu(a) * b) @ w_down
    conv_new = buf[:, buf.shape[1] - (CONV_W - 1):]
    return x, kv_a_new, kv_c_new, ki_r, s_new, conv_new


def setup_inputs(seed: int = 0) -> dict:
    key = jax.random.key(seed)
    ks = jax.random.split(key, 24)
    nrm = lambda k, shape, scale: scale * jax.random.normal(k, shape, F32)
    n_pages = PAST_LEN // PAGE_SIZE
    n_used = DEC_BATCH * n_pages
    n_pool = (5 * n_used) // 4
    page_table = jax.random.permutation(ks[0], n_pool)[:n_used].reshape(DEC_BATCH, n_pages).astype(jnp.int32)
    return {
        'x_prompt': nrm(ks[1], (BATCH, SEQ, D_MODEL), 1.0),
        'x_sample': nrm(ks[2], (DEC_BATCH, DEC_SEQ, D_MODEL), 1.0),
        'cache_kv_a': nrm(ks[3], (DEPTH, n_pool, PAGE_SIZE, 2, H_A, DH_A), 1.0),
        'cache_kv_c': nrm(ks[4], (DEPTH, n_pool, PAGE_SIZE, 2, H_C, DH_C), 1.0),
        'cache_kidx_c': nrm(ks[5], (DEPTH, n_pool, PAGE_SIZE, D_IDX), 1.0),
        'state_hgrn': nrm(ks[6], (DEPTH, DEC_BATCH, H_B, DK_B, DV_B), 0.5),
        'state_ffn_conv': nrm(ks[7], (DEPTH, DEC_BATCH, CONV_W - 1, 2 * D_FF), 1.0),
        'page_table': page_table,
        'norm_mix': 1.0 + nrm(ks[8], (DEPTH, D_MODEL), 0.01),
        'w_in': nrm(ks[9], (DEPTH, D_MODEL, N_IN), D_MODEL ** -0.5),
        'hgrn_lb_logits': nrm(ks[10], (DEPTH, H_B * DK_B), 1.0),
        'hgrn_gnorm': 1.0 + nrm(ks[11], (DEPTH, DV_B), 0.01),
        'w_br_a': nrm(ks[12], (DEPTH, H_A * DH_A, D_MODEL), (H_A * DH_A) ** -0.5),
        'w_br_b': nrm(ks[13], (DEPTH, H_B * DV_B, D_MODEL), (H_B * DV_B) ** -0.5),
        'w_br_c': nrm(ks[14], (DEPTH, H_C * DH_C, D_MODEL), (H_C * DH_C) ** -0.5),
        'w_out': nrm(ks[15], (DEPTH, D_MODEL, D_MODEL), D_MODEL ** -0.5),
        'norm_ffn': 1.0 + nrm(ks[16], (DEPTH, D_MODEL), 0.01),
        'w_up': nrm(ks[17], (DEPTH, D_MODEL, 2 * D_FF), D_MODEL ** -0.5),
        'conv_w': nrm(ks[18], (DEPTH, CONV_W, 2 * D_FF), CONV_W ** -0.5),
        'conv_b': nrm(ks[19], (DEPTH, 2 * D_FF), 0.01),
        'w_down': nrm(ks[20], (DEPTH, D_FF, D_MODEL), D_FF ** -0.5),
        'norm_final': 1.0 + nrm(ks[21], (D_MODEL,), 0.01),
    }


def reference(x_prompt, x_sample, cache_kv_a, cache_kv_c, cache_kidx_c, state_hgrn, state_ffn_conv,
              page_table, norm_mix, w_in, hgrn_lb_logits, hgrn_gnorm, w_br_a, w_br_b, w_br_c, w_out,
              norm_ffn, w_up, conv_w, conv_b, w_down, norm_final):
    lb_soft = jax.nn.softmax(hgrn_lb_logits.astype(F32), axis=0)
    lb_all = jnp.cumsum(lb_soft, axis=0) - lb_soft[0]

    bp = x_prompt.shape[0]
    dt = x_prompt.dtype
    y_p, y_s = x_prompt, x_sample
    new_p = ([], [], [], [], [])
    new_s = ([], [], [], [], [])
    for l in range(DEPTH):
        params = (norm_mix[l], w_in[l], lb_all[l], hgrn_gnorm[l], w_br_a[l], w_br_b[l], w_br_c[l], w_out[l],
                  norm_ffn[l], w_up[l], conv_w[l], conv_b[l], w_down[l])
        y_p, *st_p = hybrid_layer(
            y_p,
            jnp.zeros((bp, 0, 2, H_A, DH_A), dt),
            jnp.zeros((bp, 0, 2, H_C, DH_C), dt),
            jnp.zeros((bp, 0, D_IDX), dt),
            jnp.zeros((bp, H_B, DK_B, DV_B), F32),
            jnp.zeros((bp, CONV_W - 1, 2 * D_FF), dt),
            *params)
        y_s, *st_s = hybrid_layer(
            y_s,
            gather_pages(cache_kv_a[l], page_table),
            gather_pages(cache_kv_c[l], page_table),
            gather_pages(cache_kidx_c[l], page_table),
            state_hgrn[l],
            state_ffn_conv[l],
            *params)
        for lst, arr in zip(new_p, st_p):
            lst.append(arr)
        for lst, arr in zip(new_s, st_s):
            lst.append(arr)

    y_prompt = rmsnorm(y_p, norm_final)
    y_sample = rmsnorm(y_s, norm_final)
    kv_a_prompt, kv_c_prompt, kidx_prompt, hgrn_prompt, conv_prompt = [jnp.stack(a, axis=0) for a in new_p]
    kv_a_sample, kv_c_sample, kidx_sample, hgrn_sample, conv_sample = [jnp.stack(a, axis=0) for a in new_s]
    return (y_prompt, y_sample, kv_a_prompt, kv_a_sample, kv_c_prompt, kv_c_sample,
            kidx_prompt, kidx_sample, hgrn_prompt, hgrn_sample, conv_prompt, conv_sample)
```

```python
import functools

import jax
import jax.numpy as jnp
from jax import lax
from jax.experimental import pallas as pl
from jax.experimental.pallas import tpu as pltpu

F32 = jnp.float32
BF16 = jnp.bfloat16
I32 = jnp.int32

EPS = 1e-6
NEG_BIG = -1e30
LOG_F_MIN = -30.0
ROPE_THETA = 10000.0
TOPK_MAX = 256
PAGE = 128
N_HEADS = 4
DH = 64
DK = 128
LANES = 128
VMEM_LIMIT = 56 * 1024 * 1024
INT_MIN = -(2 ** 31)


def _cparams(sem):
    return pltpu.CompilerParams(dimension_semantics=sem, vmem_limit_bytes=VMEM_LIMIT)


def _dot(a, b):
    return jnp.dot(a, b, preferred_element_type=F32)


def _dot_nt(a, b):
    return lax.dot_general(a, b, (((1,), (1,)), ((), ())), preferred_element_type=F32)


def _dot_tn(a, b):
    return lax.dot_general(a, b, (((0,), (0,)), ((), ())), preferred_element_type=F32)


def _silu(x):
    return x * jax.nn.sigmoid(x)


def _stack_heads(q, rows):
    head = lax.broadcasted_iota(I32, (rows, N_HEADS * DH), 1) // DH
    return jnp.concatenate([jnp.where(head == h, q, jnp.zeros_like(q)) for h in range(N_HEADS)], axis=0)


def _unstack_heads(acc, rows):
    head = lax.broadcasted_iota(I32, (rows, N_HEADS * DH), 1) // DH
    out = jnp.zeros((rows, N_HEADS * DH), acc.dtype)
    for h in range(N_HEADS):
        out = jnp.where(head == h, acc[h * rows:(h + 1) * rows], out)
    return out


def _proj_body(*refs, rope, emit_bf16):
    x_ref, g_ref, w_ref = refs[:3]
    rest = refs[3:]
    if rope:
        wrot_ref, cos_ref, sin_ref = rest[:3]
        rest = rest[3:]
    of_ref = rest[0]
    ob_ref = rest[1] if emit_bf16 else None
    h_ref = rest[-1]

    @pl.when(pl.program_id(1) == 0)
    def _():
        x = x_ref[...]
        ms = jnp.mean(x * x, axis=-1, keepdims=True)
        h_ref[...] = (x * lax.rsqrt(ms + EPS) * g_ref[...]).astype(BF16)

    h = h_ref[...]
    acc = _dot(h, w_ref[...])
    if rope:
        acc = acc * cos_ref[...] + _dot(h, wrot_ref[...]) * sin_ref[...]
    of_ref[...] = acc
    if emit_bf16:
        ob_ref[...] = acc.astype(BF16)


def _rms_proj(x, g, w, *, tm, tn, rope_args=None, emit_bf16=True):
    T, D = x.shape
    N = w.shape[1]
    rope = rope_args is not None
    in_specs = [pl.BlockSpec((tm, D), lambda i, j: (i, 0)),
                pl.BlockSpec((1, D), lambda i, j: (0, 0)),
                pl.BlockSpec((D, tn), lambda i, j: (0, j))]
    args = [x, g.reshape(1, D), w]
    if rope:
        w_rot, cos, sin = rope_args
        nblk = cos.shape[0] // tm
        in_specs += [pl.BlockSpec((D, tn), lambda i, j: (0, j)),
                     pl.BlockSpec((tm, tn), lambda i, j: (i % nblk, 0)),
                     pl.BlockSpec((tm, tn), lambda i, j: (i % nblk, 0))]
        args += [w_rot, cos, sin]
    out_shape = [jax.ShapeDtypeStruct((T, N), F32)]
    out_specs = [pl.BlockSpec((tm, tn), lambda i, j: (i, j))]
    if emit_bf16:
        out_shape.append(jax.ShapeDtypeStruct((T, N), BF16))
        out_specs.append(pl.BlockSpec((tm, tn), lambda i, j: (i, j)))
    outs = pl.pallas_call(
        functools.partial(_proj_body, rope=rope, emit_bf16=emit_bf16),
        grid=(T // tm, N // tn),
        in_specs=in_specs, out_specs=out_specs, out_shape=out_shape,
        scratch_shapes=[pltpu.VMEM((tm, D), BF16)],
        compiler_params=_cparams(("parallel", "arbitrary")),
        name="rms_proj_rope" if rope else "rms_proj",
    )(*args)
    return outs if emit_bf16 else outs[0]


def _suffix_matrix():
    j = lax.broadcasted_iota(I32, (2 * LANES, 2 * LANES), 0) & (LANES - 1)
    s = lax.broadcasted_iota(I32, (2 * LANES, 2 * LANES), 1)
    return jnp.where((s >= LANES) | (j > s), 1.0, 0.0).astype(BF16)


def _sb_update(z, carry, suffix, vis):
    M, tk = z.shape
    ls = -(jnp.maximum(z, 0.0) + jnp.log1p(jnp.exp(-jnp.abs(z))))
    if vis is not None:
        ls = jnp.where(vis, ls, 0.0)
    hi = ls.astype(BF16)
    lo = (ls - hi.astype(F32)).astype(BF16)
    lz = z + ls
    n = tk // LANES
    outs = [None] * n
    for g in reversed(range(n)):
        sl = slice(g * LANES, (g + 1) * LANES)
        cs = _dot(jnp.concatenate([hi[:, sl], lo[:, sl]], axis=1), suffix)
        e = lz[:, sl] + cs[:, :LANES] + carry
        if vis is not None:
            e = jnp.where(vis[:, sl], e, NEG_BIG)
        outs[g] = jnp.exp(e)
        carry = carry + cs[:, LANES:]
    a = outs[0] if n == 1 else jnp.concatenate(outs, axis=1)
    return a.astype(BF16), carry


def _attn_a_prompt_body(q_ref, k_ref, v_ref, o_ref, *, tq):
    i = pl.program_id(1)
    M = N_HEADS * tq
    qs = _stack_heads(q_ref[...], tq) * jnp.asarray(DH ** -0.5, BF16)
    suffix = _suffix_matrix()

    def block(k0, carry, acc, vis):
        kb = k_ref[pl.ds(k0, tq), :]
        vb = v_ref[pl.ds(k0, tq), :]
        a, carry = _sb_update(_dot_nt(qs, kb), carry, suffix, vis)
        return carry, acc + _dot(a, vb)

    row_t = lax.broadcasted_iota(I32, (M, tq), 0) & (tq - 1)
    col_s = lax.broadcasted_iota(I32, (M, tq), 1)
    carry, acc = block(pl.multiple_of(i * tq, tq), jnp.zeros((M, LANES), F32),
                       jnp.zeros((M, N_HEADS * DH), F32), col_s < row_t)

    def body(step, c):
        k0 = pl.multiple_of((i - 1 - step) * tq, tq)
        return block(k0, c[0], c[1], None)

    carry, acc = lax.fori_loop(0, i, body, (carry, acc))
    o_ref[...] = _unstack_heads(acc, tq).astype(o_ref.dtype)


def _attn_a_prompt(pb, B, L, *, tq, q_blk, k_blk, v_blk):
    W = N_HEADS * DH
    nq = L // tq
    return pl.pallas_call(
        functools.partial(_attn_a_prompt_body, tq=tq),
        grid=(B, nq),
        in_specs=[pl.BlockSpec((tq, W), lambda b, i: (b * nq + i, q_blk)),
                  pl.BlockSpec((L, W), lambda b, i: (b, k_blk)),
                  pl.BlockSpec((L, W), lambda b, i: (b, v_blk))],
        out_specs=pl.BlockSpec((tq, W), lambda b, i: (b * nq + i, 0)),
        out_shape=jax.ShapeDtypeStruct((B * L, W), BF16),
        compiler_params=_cparams(("parallel", "arbitrary")),
        name="attn_a_prompt",
    )(pb, pb, pb)


def _attn_a_sample_body(pt_ref, q_ref, kn_ref, vn_ref, *rest, G):
    pages = rest[:G]
    o_ref, carry_ref, acc_ref = rest[G:]
    p = pl.program_id(1)
    W = N_HEADS * DH
    R = 8
    M = N_HEADS * R
    qs = _stack_heads(q_ref[0], R) * jnp.asarray(DH ** -0.5, BF16)
    suffix = _suffix_matrix()

    @pl.when(p == 0)
    def _():
        row_t = lax.broadcasted_iota(I32, (M, PAGE), 0) & (R - 1)
        col_s = lax.broadcasted_iota(I32, (M, PAGE), 1)
        a, carry = _sb_update(_dot_nt(qs, kn_ref[0]), jnp.zeros((M, LANES), F32), suffix, col_s < row_t)
        carry_ref[...] = carry
        acc_ref[...] = _dot(a, vn_ref[0])

    kcat = jnp.concatenate([pages[g][:, :W] for g in reversed(range(G))], axis=0).astype(BF16)
    vcat = jnp.concatenate([pages[g][:, W:] for g in reversed(range(G))], axis=0).astype(BF16)
    a, carry = _sb_update(_dot_nt(qs, kcat), carry_ref[...], suffix, None)
    carry_ref[...] = carry
    acc_ref[...] += _dot(a, vcat)

    @pl.when(p == pl.num_programs(1) - 1)
    def _():
        o_ref[0] = _unstack_heads(acc_ref[...], R).astype(o_ref.dtype)


def _attn_a_sample(q8, kn, vn, cache, layer, page_table, *, G):
    Bs = q8.shape[0]
    NP = page_table.shape[1]
    W = N_HEADS * DH

    def page_spec(g):
        return pl.BlockSpec((None, None, PAGE, 2 * W),
                            lambda b, p, pt: (layer, pt[b, NP - 1 - (p * G + g)], 0, 0))

    grid_spec = pltpu.PrefetchScalarGridSpec(
        num_scalar_prefetch=1,
        grid=(Bs, NP // G),
        in_specs=[pl.BlockSpec((1, 8, W), lambda b, p, pt: (b, 0, 0)),
                  pl.BlockSpec((1, PAGE, W), lambda b, p, pt: (b, 0, 0)),
                  pl.BlockSpec((1, PAGE, W), lambda b, p, pt: (b, 0, 0))] + [page_spec(g) for g in range(G)],
        out_specs=pl.BlockSpec((1, 8, W), lambda b, p, pt: (b, 0, 0)),
        scratch_shapes=[pltpu.VMEM((N_HEADS * 8, LANES), F32), pltpu.VMEM((N_HEADS * 8, W), F32)],
    )
    return pl.pallas_call(
        functools.partial(_attn_a_sample_body, G=G),
        grid_spec=grid_spec,
        out_shape=jax.ShapeDtypeStruct((Bs, 8, W), BF16),
        compiler_params=_cparams(("parallel", "arbitrary")),
        name="attn_a_sample",
    )(page_table, q8, kn, vn, *([cache] * G))


def _cumsum_rows(x):
    C = x.shape[0]
    row = lax.broadcasted_iota(I32, x.shape, 0)
    sh = 1
    while sh < C:
        x = x + jnp.where(row >= sh, pltpu.roll(x, sh, axis=0), 0.0)
        sh *= 2
    return x


def _hgrn_body(*refs, C, c, valid, has_s0):
    q_ref, f_ref, i_ref, g_ref, lb_ref, gn_ref = refs[:6]
    rest = refs[6:]
    if has_s0:
        s0_ref = rest[0]
        rest = rest[1:]
    o_ref, sout_ref, st_ref = rest
    ci = pl.program_id(1)

    @pl.when(ci == 0)
    def _():
        for h in range(N_HEADS):
            st_ref[h] = s0_ref[0, h].T if has_s0 else jnp.zeros((DK, DK), F32)

    row = lax.broadcasted_iota(I32, (C, DK), 0)
    rowc = lax.broadcasted_iota(I32, (c, 1), 0)
    for h in range(N_HEADS):
        hs = slice(h * DK, (h + 1) * DK)
        kk = (1.0 - lb_ref[:, hs]) * jax.nn.sigmoid(-f_ref[0, :, hs])
        lg = jnp.maximum(jnp.log1p(-kk), LOG_F_MIN)
        if valid < C:
            kk = jnp.where(row < valid, kk, 0.0)
            lg = jnp.where(row < valid, lg, 0.0)
        qq = _silu(q_ref[0, :, hs])
        vv = i_ref[0, :, hs]
        cum = _cumsum_rows(lg)
        st = st_ref[h]
        o = _dot_nt((qq * jnp.exp(cum)).astype(BF16), st.astype(BF16))
        parts = []
        for blk in range(C // c):
            r0 = blk * c
            q_b = qq[r0:r0 + c]
            cum_b = cum[r0:r0 + c]
            o_b = jnp.zeros((c, DK), F32)
            if blk > 0:
                base = cum[r0 - 1:r0]
                qt = q_b * jnp.exp(cum_b - base)
                kt = kk[:r0] * jnp.exp(base - cum[:r0])
                sc = _dot_nt(qt.astype(BF16), kt.astype(BF16))
                o_b = o_b + _dot(sc.astype(BF16), vv[:r0].astype(BF16))
            for s in range(c):
                r = r0 + s
                d = jnp.minimum(cum_b - cum[r:r + 1], 0.0)
                w = jnp.sum(q_b * kk[r:r + 1] * jnp.exp(d), axis=1, keepdims=True)
                o_b = o_b + jnp.where(rowc >= s, w, 0.0) * vv[r:r + 1]
            parts.append(o_b)
        o = o + (parts[0] if len(parts) == 1 else jnp.concatenate(parts, axis=0))
        last = cum[C - 1:C]
        kd = kk * jnp.exp(last - cum)
        st_ref[h] = st * jnp.exp(last) + _dot_tn(vv.astype(BF16), kd.astype(BF16))
        ms = jnp.mean(o * o, axis=1, keepdims=True)
        y = o * lax.rsqrt(ms + EPS) * gn_ref[...] * _silu(g_ref[0, :, hs])
        o_ref[0, :, hs] = y.astype(o_ref.dtype)

    @pl.when(ci == pl.num_programs(1) - 1)
    def _():
        for h in range(N_HEADS):
            sout_ref[0, h] = st_ref[h].T


def _hgrn(p3, lb, gn, s0, *, C, c, valid, col0=0):
    B, L, _ = p3.shape
    W = N_HEADS * DK
    has_s0 = s0 is not None
    in_specs = [pl.BlockSpec((1, C, W), functools.partial(lambda b, ci, k: (b, ci, col0 + k), k=k)) for k in range(4)]
    in_specs += [pl.BlockSpec((1, W), lambda b, ci: (0, 0)), pl.BlockSpec((1, DK), lambda b, ci: (0, 0))]
    args = [p3, p3, p3, p3, lb.reshape(1, W), gn.reshape(1, DK)]
    if has_s0:
        in_specs.append(pl.BlockSpec((1, N_HEADS, DK, DK), lambda b, ci: (b, 0, 0, 0)))
        args.append(s0)
    return pl.pallas_call(
        functools.partial(_hgrn_body, C=C, c=c, valid=valid, has_s0=has_s0),
        grid=(B, L // C),
        in_specs=in_specs,
        out_specs=[pl.BlockSpec((1, C, W), lambda b, ci: (b, ci, 0)),
                   pl.BlockSpec((1, N_HEADS, DK, DK), lambda b, ci: (b, 0, 0, 0))],
        out_shape=[jax.ShapeDtypeStruct((B, L, W), BF16), jax.ShapeDtypeStruct((B, N_HEADS, DK, DK), F32)],
        scratch_shapes=[pltpu.VMEM((N_HEADS, DK, DK), F32)],
        compiler_params=_cparams(("parallel", "arbitrary")),
        name="hgrn2",
    )(*args)


def _sort_key(score):
    b = pltpu.bitcast(score, I32)
    return jnp.where(b < 0, b ^ jnp.int32(0x7FFFFFFF), b)


def _prefix_matrix(n):
    j = lax.broadcasted_iota(I32, (n, n + LANES), 0)
    s = lax.broadcasted_iota(I32, (n, n + LANES), 1)
    return jnp.where((s >= n) | (j <= s), 1.0, 0.0).astype(BF16)


def _kth_largest_key(count_ge, rows, n_sel):
    def body(t, T):
        cand = T + jnp.left_shift(jnp.int32(1), 31 - t)
        return jnp.where(count_ge(cand) >= n_sel, cand, T)
    return lax.fori_loop(0, 32, body, jnp.full((rows, 1), INT_MIN, I32))


def _dsa_prompt_body(qc_ref, kc_ref, vc_ref, qi_ref, ki_ref, wi_ref, pm_ref, o_ref, keys_ref, *, tq, tk, n_sel):
    i = pl.program_id(1)
    nkb = ((i + 1) * tq + tk - 1) // tk
    M = N_HEADS * tq
    kpos0 = lax.broadcasted_iota(I32, (tq, tk), 1)
    qpos = i * tq + lax.broadcasted_iota(I32, (tq, tk), 0)

    qis = _stack_heads(qi_ref[...], tq)
    wi = wi_ref[...]
    wcol = [wi[:, j:j + 1] * (N_HEADS ** -0.5) for j in range(N_HEADS)]

    def score_body(j, _):
        k0 = pl.multiple_of(j * tk, tk)
        d = jnp.maximum(_dot_nt(qis, ki_ref[pl.ds(k0, tk), :]) * (DH ** -0.5), 0.0)
        score = d[0:tq] * wcol[0]
        for jh in range(1, N_HEADS):
            score = score + d[jh * tq:(jh + 1) * tq] * wcol[jh]
        score = jnp.where(kpos0 + k0 <= qpos, score + 0.0, NEG_BIG)
        keys_ref[j] = _sort_key(score)
        return 0

    lax.fori_loop(0, nkb, score_body, 0)

    def count_ge(cand):
        def body(j, acc):
            m = jnp.where(keys_ref[j] >= cand, 1.0, 0.0)
            for g in range(tk // LANES):
                acc = acc + m[:, g * LANES:(g + 1) * LANES]
            return acc
        acc = lax.fori_loop(0, nkb, body, jnp.zeros((tq, LANES), F32))
        return jnp.sum(acc, axis=1, keepdims=True)

    T = _kth_largest_key(count_ge, tq, n_sel)
    room = n_sel - count_ge(T + 1)

    qcs = _stack_heads(qc_ref[...], tq) * jnp.asarray(DH ** -0.5, BF16)
    pm = pm_ref[...]

    def att_body(j, c):
        m, l, acc, eq_before = c
        k0 = pl.multiple_of(j * tk, tk)
        key = keys_ref[j]
        eq = key == T
        pc = _dot(jnp.where(eq, 1.0, 0.0).astype(BF16), pm)
        rank = pc[:, :tk] + jnp.concatenate([eq_before] * (tk // LANES), axis=1)
        sel = jnp.where(key > T, 1.0, jnp.where(eq & (rank <= room), 1.0, 0.0))
        sel = jnp.where(kpos0 + k0 <= qpos, sel, 0.0)
        sel4 = jnp.concatenate([sel] * N_HEADS, axis=0)
        s = jnp.where(sel4 > 0.5, _dot_nt(qcs, kc_ref[pl.ds(k0, tk), :]), NEG_BIG)
        m_new = jnp.maximum(m, jnp.max(s, axis=1, keepdims=True))
        alpha = jnp.exp(m - m_new)
        p = jnp.exp(s - m_new)
        l = alpha * l + jnp.sum(p, axis=1, keepdims=True)
        acc = alpha * acc + _dot(p.astype(BF16), vc_ref[pl.ds(k0, tk), :])
        return m_new, l, acc, eq_before + pc[:, tk:]

    init = (jnp.full((M, 1), NEG_BIG, F32), jnp.zeros((M, 1), F32),
            jnp.zeros((M, N_HEADS * DH), F32), jnp.zeros((tq, LANES), F32))
    m, l, acc, _ = lax.fori_loop(0, nkb, att_body, init)
    o_ref[...] = _unstack_heads(acc / l, tq).astype(o_ref.dtype)


def _dsa_prompt(rb, pb, pf, B, L, *, tq, tk, n_sel, vc_blk, wi_blk):
    W = N_HEADS * DH
    nq = L // tq
    pm = jnp.where((jnp.arange(tk + LANES)[None, :] >= tk) | (jnp.arange(tk)[:, None] <= jnp.arange(tk + LANES)[None, :]),
                   1.0, 0.0).astype(BF16)
    return pl.pallas_call(
        functools.partial(_dsa_prompt_body, tq=tq, tk=tk, n_sel=n_sel),
        grid=(B, nq),
        in_specs=[pl.BlockSpec((tq, W), lambda b, i: (b * nq + i, 0)),
                  pl.BlockSpec((L, W), lambda b, i: (b, 1)),
                  pl.BlockSpec((L, W), lambda b, i: (b, vc_blk)),
                  pl.BlockSpec((tq, W), lambda b, i: (b * nq + i, 2)),
                  pl.BlockSpec((L, W), lambda b, i: (b, 3)),
                  pl.BlockSpec((tq, LANES), lambda b, i: (b * nq + i, wi_blk)),
                  pl.BlockSpec((tk, tk + LANES), lambda b, i: (0, 0))],
        out_specs=pl.BlockSpec((tq, W), lambda b, i: (b * nq + i, 0)),
        out_shape=jax.ShapeDtypeStruct((B * L, W), BF16),
        scratch_shapes=[pltpu.VMEM((L // tk, tq, tk), I32)],
        compiler_params=_cparams(("parallel", "arbitrary")),
        name="dsa_prompt",
    )(rb, rb, pb, rb, rb, pf, pm)


def _dsa_s_score_body(pt_ref, qi_ref, w_ref, kn_ref, *rest, G, n_new):
    pages = rest[:G]
    o_ref = rest[G]
    p = pl.program_id(1)
    R = 8
    qi = qi_ref[0]
    w = w_ref[0]

    def score(keys):
        d = jnp.maximum(_dot_nt(qi, keys) * (DH ** -0.5), 0.0) * w
        return (d[0:R] + d[R:2 * R]) + (d[2 * R:3 * R] + d[3 * R:4 * R]) + 0.0

    @pl.when(p < pl.num_programs(1) - 1)
    def _():
        for g in range(G):
            o_ref[0, :, g * PAGE:(g + 1) * PAGE] = score(pages[g][...].astype(BF16))

    @pl.when(p == pl.num_programs(1) - 1)
    def _():
        t = lax.broadcasted_iota(I32, (R, PAGE), 0)
        s = lax.broadcasted_iota(I32, (R, PAGE), 1)
        o_ref[0, :, 0:PAGE] = jnp.where((s <= t) & (s < n_new), score(kn_ref[0]), NEG_BIG)
        if G > 1:
            o_ref[0, :, PAGE:] = jnp.full((R, (G - 1) * PAGE), NEG_BIG, F32)


def _dsa_s_select_body(s_ref, o_ref, *, n_sel):
    R = 8
    key = _sort_key(s_ref[0])
    W = key.shape[1]

    def count_ge(cand):
        return jnp.sum(jnp.where(key >= cand, 1.0, 0.0), axis=1, keepdims=True)

    T = _kth_largest_key(count_ge, R, n_sel)
    room = n_sel - count_ge(T + 1)
    eq = key == T
    gt = key > T
    nt = W // LANES
    eqf = jnp.where(eq, 1.0, 0.0)
    stacked = jnp.concatenate([eqf[:, g * LANES:(g + 1) * LANES] for g in range(nt)], axis=0).astype(BF16)
    pc = _dot(stacked, _prefix_matrix(LANES))
    before = jnp.zeros((R, LANES), F32)
    tiles = []
    for g in range(nt):
        sl = slice(g * LANES, (g + 1) * LANES)
        rank = pc[g * R:(g + 1) * R, :LANES] + before
        tiles.append(jnp.where(gt[:, sl], 1.0, jnp.where(eq[:, sl] & (rank <= room), 1.0, 0.0)))
        before = before + pc[g * R:(g + 1) * R, LANES:]
    o_ref[0] = jnp.concatenate(tiles, axis=1)


def _dsa_s_attend_body(pt_ref, q_ref, kn_ref, vn_ref, mask_ref, maskn_ref, *rest, G, n_new):
    pages = rest[:G]
    o_ref, m_ref, l_ref, acc_ref = rest[G:]
    p = pl.program_id(1)
    W = N_HEADS * DH
    R = 8
    qs = q_ref[0]

    def update(s, v, m, l, acc):
        m_new = jnp.maximum(m, jnp.max(s, axis=1, keepdims=True))
        alpha = jnp.exp(m - m_new)
        pr = jnp.exp(s - m_new)
        return m_new, alpha * l + jnp.sum(pr, axis=1, keepdims=True), alpha * acc + _dot(pr.astype(BF16), v)

    @pl.when(p == 0)
    def _():
        t = lax.broadcasted_iota(I32, (R, PAGE), 0)
        sidx = lax.broadcasted_iota(I32, (R, PAGE), 1)
        sel = jnp.where((sidx <= t) & (sidx < n_new), maskn_ref[0, :, 0:PAGE], 0.0)
        s = jnp.where(jnp.concatenate([sel] * N_HEADS, axis=0) > 0.5, _dot_nt(qs, kn_ref[0]), NEG_BIG)
        m, l, acc = update(s, vn_ref[0], jnp.full((N_HEADS * R, 1), NEG_BIG, F32),
                           jnp.zeros((N_HEADS * R, 1), F32), jnp.zeros((N_HEADS * R, W), F32))
        m_ref[...] = jnp.broadcast_to(m, m_ref.shape)
        l_ref[...] = jnp.broadcast_to(l, l_ref.shape)
        acc_ref[...] = acc

    kcat = jnp.concatenate([pages[g][:, :W] for g in range(G)], axis=0).astype(BF16)
    vcat = jnp.concatenate([pages[g][:, W:] for g in range(G)], axis=0).astype(BF16)
    sel = jnp.concatenate([mask_ref[0]] * N_HEADS, axis=0)
    s = jnp.where(sel > 0.5, _dot_nt(qs, kcat), NEG_BIG)
    m, l, acc = update(s, vcat, m_ref[:, 0:1], l_ref[:, 0:1], acc_ref[...])
    m_ref[...] = jnp.broadcast_to(m, m_ref.shape)
    l_ref[...] = jnp.broadcast_to(l, l_ref.shape)
    acc_ref[...] = acc

    @pl.when(p == pl.num_programs(1) - 1)
    def _():
        o_ref[0] = _unstack_heads(acc_ref[...] / l_ref[:, 0:1], R).astype(o_ref.dtype)


def _dsa_sample(qi32, w32, kin, qc32, kcn, vcn, cache_kidx, cache_kv, layer, page_table, *, G, n_new, n_sel):
    Bs = qi32.shape[0]
    NP = page_table.shape[1]
    W = N_HEADS * DH
    nstep = NP // G
    width = (nstep + 1) * G * PAGE

    def kidx_spec(g):
        return pl.BlockSpec((None, None, PAGE, DH),
                            lambda b, p, pt: (layer, pt[b, jnp.minimum(p * G + g, NP - 1)], 0, 0))

    scores = pl.pallas_call(
        functools.partial(_dsa_s_score_body, G=G, n_new=n_new),
        grid_spec=pltpu.PrefetchScalarGridSpec(
            num_scalar_prefetch=1, grid=(Bs, nstep + 1),
            in_specs=[pl.BlockSpec((1, 32, DH), lambda b, p, pt: (b, 0, 0)),
                      pl.BlockSpec((1, 32, LANES), lambda b, p, pt: (b, 0, 0)),
                      pl.BlockSpec((1, PAGE, DH), lambda b, p, pt: (b, 0, 0))] + [kidx_spec(g) for g in range(G)],
            out_specs=pl.BlockSpec((1, 8, G * PAGE), lambda b, p, pt: (b, 0, p))),
        out_shape=jax.ShapeDtypeStruct((Bs, 8, width), F32),
        compiler_params=_cparams(("parallel", "arbitrary")),
        name="dsa_sample_score",
    )(page_table, qi32, w32, kin, *([cache_kidx] * G))

    mask = pl.pallas_call(
        functools.partial(_dsa_s_select_body, n_sel=n_sel),
        grid=(Bs,),
        in_specs=[pl.BlockSpec((1, 8, width), lambda b: (b, 0, 0))],
        out_specs=pl.BlockSpec((1, 8, width), lambda b: (b, 0, 0)),
        out_shape=jax.ShapeDtypeStruct((Bs, 8, width), F32),
        compiler_params=_cparams(("parallel",)),
        name="dsa_sample_select",
    )(scores)

    def kv_spec(g):
        return pl.BlockSpec((None, None, PAGE, 2 * W), lambda b, p, pt: (layer, pt[b, p * G + g], 0, 0))

    return pl.pallas_call(
        functools.partial(_dsa_s_attend_body, G=G, n_new=n_new),
        grid_spec=pltpu.PrefetchScalarGridSpec(
            num_scalar_prefetch=1, grid=(Bs, nstep),
            in_specs=[pl.BlockSpec((1, 32, W), lambda b, p, pt: (b, 0, 0)),
                      pl.BlockSpec((1, PAGE, W), lambda b, p, pt: (b, 0, 0)),
                      pl.BlockSpec((1, PAGE, W), lambda b, p, pt: (b, 0, 0)),
                      pl.BlockSpec((1, 8, G * PAGE), lambda b, p, pt: (b, 0, p)),
                      pl.BlockSpec((1, 8, G * PAGE), lambda b, p, pt: (b, 0, nstep))] + [kv_spec(g) for g in range(G)],
            out_specs=pl.BlockSpec((1, 8, W), lambda b, p, pt: (b, 0, 0)),
            scratch_shapes=[pltpu.VMEM((32, LANES), F32), pltpu.VMEM((32, LANES), F32), pltpu.VMEM((32, W), F32)]),
        out_shape=jax.ShapeDtypeStruct((Bs, 8, W), BF16),
        compiler_params=_cparams(("parallel", "arbitrary")),
        name="dsa_sample_attend",
    )(page_table, qc32, kcn, vcn, mask, mask, *([cache_kv] * G))


def _merge_body(oa_ref, ob_ref, oc_ref, g0_ref, g1_ref, g2_ref, x_ref, wa_ref, wb_ref, wc_ref, wo_ref, o_ref):
    merged = (jax.nn.sigmoid(g0_ref[...]) * _dot(oa_ref[...], wa_ref[...])
              + jax.nn.sigmoid(g1_ref[...]) * _dot(ob_ref[...], wb_ref[...])
              + jax.nn.sigmoid(g2_ref[...]) * _dot(oc_ref[...], wc_ref[...]))
    o_ref[...] = x_ref[...] + _dot(merged.astype(BF16), wo_ref[...])


def _merge(oa, ob, oc, pf, x, wa, wb, wc, wo, *, tm, gate_blk):
    T, D = x.shape
    row = lambda i: (i, 0)
    const = lambda i: (0, 0)
    return pl.pallas_call(
        _merge_body,
        grid=(T // tm,),
        in_specs=[pl.BlockSpec((tm, oa.shape[1]), row), pl.BlockSpec((tm, ob.shape[1]), row),
                  pl.BlockSpec((tm, oc.shape[1]), row),
                  pl.BlockSpec((tm, D), lambda i: (i, gate_blk)), pl.BlockSpec((tm, D), lambda i: (i, gate_blk + 1)),
                  pl.BlockSpec((tm, D), lambda i: (i, gate_blk + 2)), pl.BlockSpec((tm, D), row),
                  pl.BlockSpec(wa.shape, const), pl.BlockSpec(wb.shape, const),
                  pl.BlockSpec(wc.shape, const), pl.BlockSpec(wo.shape, const)],
        out_specs=pl.BlockSpec((tm, D), row),
        out_shape=jax.ShapeDtypeStruct((T, D), F32),
        compiler_params=_cparams(("parallel",)),
        name="merge",
    )(oa, ob, oc, pf, pf, pf, x, wa, wb, wc, wo)


def _ffn_down_body(*refs, tm, seq_tiles, expanded, final_norm):
    ua_ref, ub_ref, ha_ref, hb_ref = refs[:4]
    rest = refs[4:]
    if expanded:
        ha2_ref, hb2_ref, t_ref = rest[:3]
        rest = rest[3:]
    cwa_ref, cwb_ref, cba_ref, cbb_ref, wd_ref, x_ref = rest[:6]
    rest = rest[6:]
    if final_norm:
        gf_ref = rest[0]
        rest = rest[1:]
    o_ref, acc_ref = rest
    i = pl.program_id(0)
    k = pl.program_id(1)

    @pl.when(k == 0)
    def _():
        acc_ref[...] = jnp.zeros_like(acc_ref)

    def conv(u_ref, h_ref, h2_ref, cw_ref, cb_ref):
        u = u_ref[...]
        row = lax.broadcasted_iota(I32, u.shape, 0)
        r1 = pltpu.roll(u, 1, axis=0)
        r2 = pltpu.roll(u, 2, axis=0)
        if expanded:
            t = t_ref[...]
            u1 = jnp.where(t >= 1, r1, h_ref[...])
            u2 = jnp.where(t >= 2, r2, h2_ref[...])
        else:
            h = h_ref[...]
            h = jnp.where(i % seq_tiles == 0, jnp.zeros_like(h), h)
            u1 = jnp.where(row == 0, h[7:8], r1)
            u2 = jnp.where(row == 0, h[6:7], jnp.where(row == 1, h[7:8], r2))
        cw = cw_ref[...]
        return cb_ref[...] + cw[0:1] * u2 + cw[1:2] * u1 + cw[2:3] * u

    a = conv(ua_ref, ha_ref, ha2_ref if expanded else None, cwa_ref, cba_ref)
    b = conv(ub_ref, hb_ref, hb2_ref if expanded else None, cwb_ref, cbb_ref)
    acc_ref[...] += _dot((_silu(a) * b).astype(BF16), wd_ref[...])

    @pl.when(k == pl.num_programs(1) - 1)
    def _():
        y = x_ref[...] + acc_ref[...]
        if final_norm:
            ms = jnp.mean(y * y, axis=-1, keepdims=True)
            y = y * lax.rsqrt(ms + EPS) * gf_ref[...]
        o_ref[...] = y


def _ffn_down(u, x, cw, cb, wd, *, tm, tkf, seq_len, prev=None, g_final=None):
    T, D = x.shape
    F = wd.shape[0]
    nk = F // tkf
    expanded = prev is not None
    final_norm = g_final is not None
    seq_tiles = max(seq_len // tm, 1)
    hb8 = tm // 8
    ua = pl.BlockSpec((tm, tkf), lambda i, k: (i, k))
    ub = pl.BlockSpec((tm, tkf), lambda i, k: (i, nk + k))
    in_specs = [ua, ub]
    args = [u, u]
    if expanded:
        p1, p2, tpos = prev
        in_specs += [ua, ub, ua, ub, pl.BlockSpec((tm, 1), lambda i, k: (i, 0))]
        args += [p1, p1, p2, p2, tpos]
    else:
        in_specs += [pl.BlockSpec((8, tkf), lambda i, k: (jnp.maximum(i * hb8 - 1, 0), k)),
                     pl.BlockSpec((8, tkf), lambda i, k: (jnp.maximum(i * hb8 - 1, 0), nk + k))]
        args += [u, u]
    in_specs += [pl.BlockSpec((3, tkf), lambda i, k: (0, k)), pl.BlockSpec((3, tkf), lambda i, k: (0, nk + k)),
                 pl.BlockSpec((1, tkf), lambda i, k: (0, k)), pl.BlockSpec((1, tkf), lambda i, k: (0, nk + k)),
                 pl.BlockSpec((tkf, D), lambda i, k: (k, 0)), pl.BlockSpec((tm, D), lambda i, k: (i, 0))]
    args += [cw, cw, cb.reshape(1, -1), cb.reshape(1, -1), wd, x]
    if final_norm:
        in_specs.append(pl.BlockSpec((1, D), lambda i, k: (0, 0)))
        args.append(g_final.reshape(1, D))
    return pl.pallas_call(
        functools.partial(_ffn_down_body, tm=tm, seq_tiles=seq_tiles, expanded=expanded, final_norm=final_norm),
        grid=(T // tm, nk),
        in_specs=in_specs,
        out_specs=pl.BlockSpec((tm, D), lambda i, k: (i, 0)),
        out_shape=jax.ShapeDtypeStruct((T, D), F32),
        scratch_shapes=[pltpu.VMEM((tm, D), F32)],
        compiler_params=_cparams(("parallel", "arbitrary")),
        name="ffn_down",
    )(*args)


def _rotate_half_cols(w):
    D, N = w.shape
    w4 = w.reshape(D, N // DH, 2, DH // 2)
    return jnp.concatenate([-w4[:, :, 1], w4[:, :, 0]], axis=-1).reshape(D, N)


def _rope_tables(pos, width):
    half = DH // 2
    inv_freq = ROPE_THETA ** (-jnp.arange(half, dtype=F32) / half)
    ang = pos.astype(F32)[:, None] * inv_freq[None, :]
    reps = width // half
    return jnp.tile(jnp.cos(ang), (1, reps)), jnp.tile(jnp.sin(ang), (1, reps))


def _pick(n, prefs):
    for t in prefs:
        if n % t == 0:
            return t
    return n


def kernel(x_prompt, x_sample, cache_kv_a, cache_kv_c, cache_kidx_c, state_hgrn, state_ffn_conv, page_table,
           norm_mix, w_in, hgrn_lb_logits, hgrn_gnorm, w_br_a, w_br_b, w_br_c, w_out, norm_ffn, w_up, conv_w,
           conv_b, w_down, norm_final):
    B, L, D = x_prompt.shape
    Bs, Ls, _ = x_sample.shape
    depth = w_in.shape[0]
    NP = page_table.shape[1]
    past = NP * PAGE
    F = w_down.shape[1]
    WA = N_HEADS * DH
    WB = N_HEADS * DK
    Tp, Ts = B * L, Bs * Ls
    n_pool = cache_kv_a.shape[1]

    lb_soft = jax.nn.softmax(hgrn_lb_logits.astype(F32), axis=0)
    lb_all = jnp.cumsum(lb_soft, axis=0) - lb_soft[0]

    sizes = (WA, WA, WA, WB, WB, WB, WB, WA, WA, WA, N_HEADS * DH, DH, N_HEADS, 3 * D)
    offs = [0]
    for s in sizes:
        offs.append(offs[-1] + s)
    (o_qa, o_ka, o_va, o_qh, o_fh, o_ih, o_gh, o_qc, o_kc, o_vc, o_qi, o_ki, o_wi, o_gt, o_end) = offs
    QA_BLK, KA_BLK, VA_BLK, VC_BLK = (4 * WB) // WA, (4 * WB) // WA + 1, (4 * WB) // WA + 2, (4 * WB) // WA + 3
    gate_off = 4 * WB + 4 * WA
    assert gate_off % D == 0
    GATE_BLK = gate_off // D
    wi_off = gate_off + 3 * D
    WI_BLK = wi_off // LANES
    NPJ = wi_off + LANES

    cache_a = cache_kv_a.reshape(depth, n_pool, PAGE, 2 * WA)
    cache_c = cache_kv_c.reshape(depth, n_pool, PAGE, 2 * WA)

    cos_p, sin_p = _rope_tables(jnp.arange(L, dtype=I32), WA)
    cos_s, sin_s = _rope_tables(past + (jnp.arange(Ts, dtype=I32) % Ls), WA)

    tm_p = _pick(Tp, (1024, 512, 256, 128))
    tm_p = min(tm_p, L)
    tn_main = _pick(NPJ, (512, 256, 128))
    tn_up = _pick(2 * F, (512, 256, 128))
    tkf = _pick(F, (256, 128))
    tq_a = _pick(L, (256, 128))
    tq_c = _pick(L, (128,))
    tk_c = _pick(L, (512, 256, 128))
    C_h = _pick(L, (64,))
    G = _pick(NP, (8, 4, 2, 1))
    n_sel_p = min(TOPK_MAX, L // 4)
    n_sel_s = min(TOPK_MAX, (past + Ls) // 4)

    xp = x_prompt.reshape(Tp, D)
    xs = x_sample.reshape(Ts, D)
    tpos_s = (jnp.arange(Ts, dtype=I32) % Ls).reshape(Ts, 1)

    def pad_rows(a, rows):
        return jnp.pad(a, ((0, 0), (0, rows - a.shape[1]), (0, 0)))

    outs_p = ([], [], [], [], [])
    outs_s = ([], [], [], [], [])
    for l in range(depth):
        w = w_in[l]
        col = lambda o, n: w[:, o:o + n]
        w_main = jnp.concatenate(
            [col(o_qh, 4 * WB), col(o_qa, 3 * WA), col(o_vc, WA), col(o_gt, 3 * D), col(o_wi, N_HEADS),
             jnp.zeros((D, LANES - N_HEADS), w.dtype)], axis=1).astype(BF16)
        w_r = jnp.concatenate([col(o_qc, WA), col(o_kc, WA), col(o_qi, WA)] + [col(o_ki, DH)] * N_HEADS, axis=1)
        w_rot = _rotate_half_cols(w_r).astype(BF16)
        w_r = w_r.astype(BF16)
        wa_b, wb_b, wc_b, wo_b = (t[l].astype(BF16) for t in (w_br_a, w_br_b, w_br_c, w_out))
        wup_b = w_up[l].astype(BF16)
        wd_b = w_down[l].astype(BF16)
        last = l == depth - 1

        pf, pb = _rms_proj(xp, norm_mix[l], w_main, tm=tm_p, tn=tn_main)
        rf, rb = _rms_proj(xp, norm_mix[l], w_r, tm=tm_p, tn=WA, rope_args=(w_rot, cos_p, sin_p))
        oa = _attn_a_prompt(pb, B, L, tq=tq_a, q_blk=QA_BLK, k_blk=KA_BLK, v_blk=VA_BLK)
        ob, s_new = _hgrn(pf.reshape(B, L, NPJ), lb_all[l], hgrn_gnorm[l], None, C=C_h, c=min(16, C_h), valid=C_h)
        oc = _dsa_prompt(rb, pb, pf, B, L, tq=tq_c, tk=tk_c, n_sel=n_sel_p, vc_blk=VC_BLK, wi_blk=WI_BLK)
        x1 = _merge(oa, ob.reshape(Tp, WB), oc, pf, xp, wa_b, wb_b, wc_b, wo_b, tm=min(512, tm_p), gate_blk=GATE_BLK)
        u = _rms_proj(x1, norm_ffn[l], wup_b, tm=tm_p, tn=tn_up, emit_bf16=False)
        xp = _ffn_down(u, x1, conv_w[l], conv_b[l], wd_b, tm=min(512, tm_p), tkf=tkf, seq_len=L,
                       g_final=norm_final if last else None)
        kv_off = (KA_BLK * WA)
        outs_p[0].append(pf[:, kv_off:kv_off + 2 * WA].reshape(B, L, 2, N_HEADS, DH))
        outs_p[1].append(jnp.stack([rf[:, WA:2 * WA].reshape(B, L, N_HEADS, DH),
                                    pf[:, VC_BLK * WA:(VC_BLK + 1) * WA].reshape(B, L, N_HEADS, DH)], axis=2))
        outs_p[2].append(rf[:, 3 * WA:3 * WA + DH].reshape(B, L, DH))
        outs_p[3].append(s_new)
        outs_p[4].append(u.reshape(B, L, 2 * F)[:, L - 2:])

        pf, pb = _rms_proj(xs, norm_mix[l], w_main, tm=Ts, tn=tn_main)
        rf, rb = _rms_proj(xs, norm_mix[l], w_r, tm=Ts, tn=WA, rope_args=(w_rot, cos_s, sin_s))
        pb3 = pb.reshape(Bs, Ls, NPJ)
        rb3 = rb.reshape(Bs, Ls, 4 * WA)
        blk = lambda a, k: a[:, :, k * WA:(k + 1) * WA]
        oa8 = _attn_a_sample(pad_rows(blk(pb3, QA_BLK), 8), pad_rows(blk(pb3, KA_BLK), PAGE),
                             pad_rows(blk(pb3, VA_BLK), PAGE), cache_a, l, page_table, G=G)
        ph = jnp.pad(pf[:, :4 * WB].reshape(Bs, Ls, 4 * WB), ((0, 0), (0, 8 - Ls), (0, 0)))
        ob8, s_new = _hgrn(ph, lb_all[l], hgrn_gnorm[l], state_hgrn[l], C=8, c=8, valid=Ls)
        qi = blk(rb3, 2).reshape(Bs, Ls, N_HEADS, DH).transpose(0, 2, 1, 3)
        qi32 = jnp.pad(qi, ((0, 0), (0, 0), (0, 8 - Ls), (0, 0))).reshape(Bs, 4 * 8, DH)
        wi = pf[:, wi_off:wi_off + N_HEADS].reshape(Bs, Ls, N_HEADS).transpose(0, 2, 1) * (N_HEADS ** -0.5)
        w32 = jnp.broadcast_to(jnp.pad(wi, ((0, 0), (0, 0), (0, 8 - Ls))).reshape(Bs, 32, 1), (Bs, 32, LANES))
        kin = pad_rows(rb3[:, :, 3 * WA:3 * WA + DH], PAGE)
        qc8 = pad_rows(blk(rb3, 0), 8)
        head = (jnp.arange(WA) // DH)[None, None, None, :] == jnp.arange(N_HEADS)[None, :, None, None]
        qc32 = (jnp.where(head, qc8[:, None], 0) * jnp.asarray(DH ** -0.5, BF16)).reshape(Bs, 32, WA).astype(BF16)
        oc8 = _dsa_sample(qi32, w32, kin, qc32, pad_rows(blk(rb3, 1), PAGE), pad_rows(blk(pb3, VC_BLK), PAGE),
                          cache_kidx_c, cache_c, l, page_table, G=G, n_new=Ls, n_sel=n_sel_s)
        x1 = _merge(oa8[:, :Ls].reshape(Ts, WA), ob8[:, :Ls].reshape(Ts, WB), oc8[:, :Ls].reshape(Ts, WA), pf, xs,
                    wa_b, wb_b, wc_b, wo_b, tm=Ts, gate_blk=GATE_BLK)
        u = _rms_proj(x1, norm_ffn[l], wup_b, tm=Ts, tn=tn_up, emit_bf16=False)
        prev = state_ffn_conv[l]
        u3 = u.reshape(Bs, Ls, 2 * F)
        p1 = jnp.broadcast_to(prev[:, 1:2], (Bs, Ls, 2 * F)).reshape(Ts, 2 * F)
        p2 = jnp.concatenate([prev, jnp.zeros((Bs, Ls - 2, 2 * F), F32)], axis=1).reshape(Ts, 2 * F)
        xs = _ffn_down(u, x1, conv_w[l], conv_b[l], wd_b, tm=Ts, tkf=tkf, seq_len=Ls, prev=(p1, p2, tpos_s),
                       g_final=norm_final if last else None)
        kv_off = (KA_BLK * WA)
        outs_s[0].append(pf[:, kv_off:kv_off + 2 * WA].reshape(Bs, Ls, 2, N_HEADS, DH))
        outs_s[1].append(jnp.stack([rf[:, WA:2 * WA].reshape(Bs, Ls, N_HEADS, DH),
                                    pf[:, VC_BLK * WA:(VC_BLK + 1) * WA].reshape(Bs, Ls, N_HEADS, DH)], axis=2))
        outs_s[2].append(rf[:, 3 * WA:3 * WA + DH].reshape(Bs, Ls, DH))
        outs_s[3].append(s_new)
        outs_s[4].append(jnp.concatenate([prev, u3], axis=1)[:, Ls:])

    st = lambda xs_: jnp.stack(xs_, axis=0)
    return (xp.reshape(B, L, D), xs.reshape(Bs, Ls, D),
            st(outs_p[0]), st(outs_s[0]), st(outs_p[1]), st(outs_s[1]), st(outs_p[2]), st(outs_s[2]),
            st(outs_p[3]), st(outs_s[3]), st(outs_p[4]), st(outs_s[4]))
```

```python
import functools

import jax
import jax.numpy as jnp
from jax import lax
from jax.experimental import pallas as pl
from jax.experimental.pallas import tpu as pltpu

F32 = jnp.float32
BF16 = jnp.bfloat16
I32 = jnp.int32

EPS = 1e-6
NEG_BIG = -1e30
LOG_F_MIN = -30.0
ROPE_THETA = 10000.0
TOPK_MAX = 256
PAGE = 128
N_HEADS = 4
DH = 64
DK = 128
LANES = 128
VMEM_LIMIT = 56 * 1024 * 1024
INT_MIN = -(2 ** 31)


def _cparams(sem):
    return pltpu.CompilerParams(dimension_semantics=sem, vmem_limit_bytes=VMEM_LIMIT)


def _dot(a, b):
    return jnp.dot(a, b, preferred_element_type=F32)


def _dot_nt(a, b):
    return lax.dot_general(a, b, (((1,), (1,)), ((), ())), preferred_element_type=F32)


def _dot_tn(a, b):
    return lax.dot_general(a, b, (((0,), (0,)), ((), ())), preferred_element_type=F32)


def _silu(x):
    return x * jax.nn.sigmoid(x)


def _stack_heads(q, rows):
    head = lax.broadcasted_iota(I32, (rows, N_HEADS * DH), 1) // DH
    return jnp.concatenate([jnp.where(head == h, q, jnp.zeros_like(q)) for h in range(N_HEADS)], axis=0)


def _unstack_heads(acc, rows):
    head = lax.broadcasted_iota(I32, (rows, N_HEADS * DH), 1) // DH
    out = jnp.zeros((rows, N_HEADS * DH), acc.dtype)
    for h in range(N_HEADS):
        out = jnp.where(head == h, acc[h * rows:(h + 1) * rows], out)
    return out


def _proj_body(*refs, rope, emit_bf16, n_rope_tiles):
    x_ref, g_ref, w_ref = refs[:3]
    rest = refs[3:]
    if rope:
        wrot_ref, cos_ref, sin_ref = rest[:3]
        rest = rest[3:]
    of_ref = rest[0]
    ob_ref = rest[1] if emit_bf16 else None
    h_ref = rest[-1]

    @pl.when(pl.program_id(1) == 0)
    def _():
        x = x_ref[...]
        ms = jnp.mean(x * x, axis=-1, keepdims=True)
        h_ref[...] = (x * lax.rsqrt(ms + EPS) * g_ref[...]).astype(BF16)

    h = h_ref[...]
    acc = _dot(h, w_ref[...])
    if rope:
        roped = acc * cos_ref[...] + _dot(h, wrot_ref[...]) * sin_ref[...]
        acc = jnp.where(pl.program_id(1) < n_rope_tiles, roped, acc)
    of_ref[...] = acc
    if emit_bf16:
        ob_ref[...] = acc.astype(BF16)


def _rms_proj(x, g, w, *, tm, tn, rope_args=None, emit_bf16=True, n_rope_tiles=0):
    T, D = x.shape
    N = w.shape[1]
    rope = rope_args is not None
    in_specs = [pl.BlockSpec((tm, D), lambda i, j: (i, 0)),
                pl.BlockSpec((1, D), lambda i, j: (0, 0)),
                pl.BlockSpec((D, tn), lambda i, j: (0, j))]
    args = [x, g.reshape(1, D), w]
    if rope:
        w_rot, cos, sin = rope_args
        nblk = cos.shape[0] // tm
        in_specs += [pl.BlockSpec((D, tn), lambda i, j: (0, j)),
                     pl.BlockSpec((tm, tn), lambda i, j: (i % nblk, 0)),
                     pl.BlockSpec((tm, tn), lambda i, j: (i % nblk, 0))]
        args += [w_rot, cos, sin]
    out_shape = [jax.ShapeDtypeStruct((T, N), F32)]
    out_specs = [pl.BlockSpec((tm, tn), lambda i, j: (i, j))]
    if emit_bf16:
        out_shape.append(jax.ShapeDtypeStruct((T, N), BF16))
        out_specs.append(pl.BlockSpec((tm, tn), lambda i, j: (i, j)))
    outs = pl.pallas_call(
        functools.partial(_proj_body, rope=rope, emit_bf16=emit_bf16, n_rope_tiles=n_rope_tiles),
        grid=(T // tm, N // tn),
        in_specs=in_specs, out_specs=out_specs, out_shape=out_shape,
        scratch_shapes=[pltpu.VMEM((tm, D), BF16)],
        compiler_params=_cparams(("parallel", "arbitrary")),
        name="rms_proj_rope" if rope else "rms_proj",
    )(*args)
    return outs if emit_bf16 else outs[0]


def _suffix_matrix():
    j = lax.broadcasted_iota(I32, (2 * LANES, 2 * LANES), 0) & (LANES - 1)
    s = lax.broadcasted_iota(I32, (2 * LANES, 2 * LANES), 1)
    return jnp.where((s >= LANES) | (j > s), 1.0, 0.0).astype(BF16)


def _log_one_minus_beta(z):
    return -(jnp.maximum(z, 0.0) + jnp.log(1.0 + jnp.exp(-jnp.abs(z))))


def _sb_update(z, carry, suffix, vis):
    M, tk = z.shape
    ls = _log_one_minus_beta(z)
    if vis is not None:
        ls = jnp.where(vis, ls, 0.0)
    hi = ls.astype(BF16)
    lo = (ls - hi.astype(F32)).astype(BF16)
    lz = z + ls
    n = tk // LANES
    outs = [None] * n
    for g in reversed(range(n)):
        sl = slice(g * LANES, (g + 1) * LANES)
        cs = _dot(jnp.concatenate([hi[:, sl], lo[:, sl]], axis=1), suffix)
        e = lz[:, sl] + cs[:, :LANES] + carry
        if vis is not None:
            e = jnp.where(vis[:, sl], e, NEG_BIG)
        outs[g] = jnp.exp(e)
        carry = carry + cs[:, LANES:]
    a = outs[0] if n == 1 else jnp.concatenate(outs, axis=1)
    return a.astype(BF16), carry


def _attn_a_prompt_body(q_ref, k_ref, v_ref, o_ref, *, tq):
    i = pl.program_id(1)
    M = N_HEADS * tq
    qs = _stack_heads(q_ref[...], tq) * jnp.asarray(DH ** -0.5, BF16)
    suffix = _suffix_matrix()

    def block(k0, carry, acc, vis):
        kb = k_ref[pl.ds(k0, tq), :]
        vb = v_ref[pl.ds(k0, tq), :]
        a, carry = _sb_update(_dot_nt(qs, kb), carry, suffix, vis)
        return carry, acc + _dot(a, vb)

    row_t = lax.broadcasted_iota(I32, (M, tq), 0) & (tq - 1)
    col_s = lax.broadcasted_iota(I32, (M, tq), 1)
    carry, acc = block(pl.multiple_of(i * tq, tq), jnp.zeros((M, LANES), F32),
                       jnp.zeros((M, N_HEADS * DH), F32), col_s < row_t)

    def body(step, c):
        k0 = pl.multiple_of((i - 1 - step) * tq, tq)
        return block(k0, c[0], c[1], None)

    carry, acc = lax.fori_loop(0, i, body, (carry, acc))
    o_ref[...] = _unstack_heads(acc, tq).astype(o_ref.dtype)


def _attn_a_prompt(pb, B, L, *, tq, q_blk, k_blk, v_blk):
    W = N_HEADS * DH
    nq = L // tq
    return pl.pallas_call(
        functools.partial(_attn_a_prompt_body, tq=tq),
        grid=(B, nq),
        in_specs=[pl.BlockSpec((tq, W), lambda b, i: (b * nq + i, q_blk)),
                  pl.BlockSpec((L, W), lambda b, i: (b, k_blk)),
                  pl.BlockSpec((L, W), lambda b, i: (b, v_blk))],
        out_specs=pl.BlockSpec((tq, W), lambda b, i: (b * nq + i, 0)),
        out_shape=jax.ShapeDtypeStruct((B * L, W), BF16),
        compiler_params=_cparams(("parallel", "arbitrary")),
        name="attn_a_prompt",
    )(pb, pb, pb)


def _attn_a_sample_body(pt_ref, q_ref, kn_ref, vn_ref, *rest, G):
    pages = rest[:G]
    o_ref, carry_ref, acc_ref = rest[G:]
    p = pl.program_id(1)
    W = N_HEADS * DH
    R = 8
    M = N_HEADS * R
    qs = _stack_heads(q_ref[0], R) * jnp.asarray(DH ** -0.5, BF16)
    suffix = _suffix_matrix()

    @pl.when(p == 0)
    def _():
        row_t = lax.broadcasted_iota(I32, (M, PAGE), 0) & (R - 1)
        col_s = lax.broadcasted_iota(I32, (M, PAGE), 1)
        a, carry = _sb_update(_dot_nt(qs, kn_ref[0]), jnp.zeros((M, LANES), F32), suffix, col_s < row_t)
        carry_ref[...] = carry
        acc_ref[...] = _dot(a, vn_ref[0])

    kcat = jnp.concatenate([pages[g][:W, :] for g in reversed(range(G))], axis=1).astype(BF16)
    vcat = jnp.concatenate([pages[g][W:, :] for g in reversed(range(G))], axis=1).astype(BF16)
    a, carry = _sb_update(_dot(qs, kcat), carry_ref[...], suffix, None)
    carry_ref[...] = carry
    acc_ref[...] += _dot_nt(a, vcat)

    @pl.when(p == pl.num_programs(1) - 1)
    def _():
        o_ref[0] = _unstack_heads(acc_ref[...], R).astype(o_ref.dtype)


def _attn_a_sample(q8, kn, vn, cache, layer, page_table, *, G):
    Bs = q8.shape[0]
    NP = page_table.shape[1]
    W = N_HEADS * DH

    def page_spec(g):
        return pl.BlockSpec((None, None, 2 * W, PAGE),
                            lambda b, p, pt: (layer, pt[b, NP - 1 - (p * G + g)], 0, 0))

    grid_spec = pltpu.PrefetchScalarGridSpec(
        num_scalar_prefetch=1,
        grid=(Bs, NP // G),
        in_specs=[pl.BlockSpec((1, 8, W), lambda b, p, pt: (b, 0, 0)),
                  pl.BlockSpec((1, PAGE, W), lambda b, p, pt: (b, 0, 0)),
                  pl.BlockSpec((1, PAGE, W), lambda b, p, pt: (b, 0, 0))] + [page_spec(g) for g in range(G)],
        out_specs=pl.BlockSpec((1, 8, W), lambda b, p, pt: (b, 0, 0)),
        scratch_shapes=[pltpu.VMEM((N_HEADS * 8, LANES), F32), pltpu.VMEM((N_HEADS * 8, W), F32)],
    )
    return pl.pallas_call(
        functools.partial(_attn_a_sample_body, G=G),
        grid_spec=grid_spec,
        out_shape=jax.ShapeDtypeStruct((Bs, 8, W), BF16),
        compiler_params=_cparams(("parallel", "arbitrary")),
        name="attn_a_sample",
    )(page_table, q8, kn, vn, *([cache] * G))


def _cumsum_rows(x):
    C = x.shape[0]
    row = lax.broadcasted_iota(I32, x.shape, 0)
    sh = 1
    while sh < C:
        x = x + jnp.where(row >= sh, pltpu.roll(x, sh, axis=0), 0.0)
        sh *= 2
    return x


def _hgrn_body(*refs, C, c, valid, has_s0):
    q_ref, f_ref, i_ref, g_ref, lb_ref, gn_ref = refs[:6]
    rest = refs[6:]
    if has_s0:
        s0_ref = rest[0]
        rest = rest[1:]
    o_ref, sout_ref, st_ref = rest
    ci = pl.program_id(1)

    @pl.when(ci == 0)
    def _():
        for h in range(N_HEADS):
            st_ref[h] = s0_ref[0, h].T if has_s0 else jnp.zeros((DK, DK), F32)

    row = lax.broadcasted_iota(I32, (C, DK), 0)
    rowc = lax.broadcasted_iota(I32, (c, 1), 0)
    for h in range(N_HEADS):
        hs = slice(h * DK, (h + 1) * DK)
        kk = (1.0 - lb_ref[:, hs]) * jax.nn.sigmoid(-f_ref[0, :, hs])
        lg = jnp.maximum(jnp.log1p(-kk), LOG_F_MIN)
        if valid < C:
            kk = jnp.where(row < valid, kk, 0.0)
            lg = jnp.where(row < valid, lg, 0.0)
        qq = _silu(q_ref[0, :, hs])
        vv = i_ref[0, :, hs]
        cum = _cumsum_rows(lg)
        st = st_ref[h]
        o = _dot_nt((qq * jnp.exp(cum)).astype(BF16), st.astype(BF16))
        parts = []
        for blk in range(C // c):
            r0 = blk * c
            q_b = qq[r0:r0 + c]
            cum_b = cum[r0:r0 + c]
            o_b = jnp.zeros((c, DK), F32)
            if blk > 0:
                base = cum[r0 - 1:r0]
                qt = q_b * jnp.exp(cum_b - base)
                kt = kk[:r0] * jnp.exp(base - cum[:r0])
                sc = _dot_nt(qt.astype(BF16), kt.astype(BF16))
                o_b = o_b + _dot(sc.astype(BF16), vv[:r0].astype(BF16))
            for s in range(c):
                r = r0 + s
                d = jnp.minimum(cum_b - cum[r:r + 1], 0.0)
                w = jnp.sum(q_b * kk[r:r + 1] * jnp.exp(d), axis=1, keepdims=True)
                o_b = o_b + jnp.where(rowc >= s, w, 0.0) * vv[r:r + 1]
            parts.append(o_b)
        o = o + (parts[0] if len(parts) == 1 else jnp.concatenate(parts, axis=0))
        last = cum[C - 1:C]
        kd = kk * jnp.exp(last - cum)
        st_ref[h] = st * jnp.exp(last) + _dot_tn(vv.astype(BF16), kd.astype(BF16))
        ms = jnp.mean(o * o, axis=1, keepdims=True)
        y = o * lax.rsqrt(ms + EPS) * gn_ref[...] * _silu(g_ref[0, :, hs])
        o_ref[0, :, hs] = y.astype(o_ref.dtype)

    @pl.when(ci == pl.num_programs(1) - 1)
    def _():
        for h in range(N_HEADS):
            sout_ref[0, h] = st_ref[h].T


def _hgrn(p3, lb, gn, s0, *, C, c, valid, col0=0):
    B, L, _ = p3.shape
    W = N_HEADS * DK
    has_s0 = s0 is not None
    in_specs = [pl.BlockSpec((1, C, W), functools.partial(lambda b, ci, k: (b, ci, col0 + k), k=k)) for k in range(4)]
    in_specs += [pl.BlockSpec((1, W), lambda b, ci: (0, 0)), pl.BlockSpec((1, DK), lambda b, ci: (0, 0))]
    args = [p3, p3, p3, p3, lb.reshape(1, W), gn.reshape(1, DK)]
    if has_s0:
        in_specs.append(pl.BlockSpec((1, N_HEADS, DK, DK), lambda b, ci: (b, 0, 0, 0)))
        args.append(s0)
    return pl.pallas_call(
        functools.partial(_hgrn_body, C=C, c=c, valid=valid, has_s0=has_s0),
        grid=(B, L // C),
        in_specs=in_specs,
        out_specs=[pl.BlockSpec((1, C, W), lambda b, ci: (b, ci, 0)),
                   pl.BlockSpec((1, N_HEADS, DK, DK), lambda b, ci: (b, 0, 0, 0))],
        out_shape=[jax.ShapeDtypeStruct((B, L, W), BF16), jax.ShapeDtypeStruct((B, N_HEADS, DK, DK), F32)],
        scratch_shapes=[pltpu.VMEM((N_HEADS, DK, DK), F32)],
        compiler_params=_cparams(("parallel", "arbitrary")),
        name="hgrn2",
    )(*args)


def _sort_key(score):
    b = pltpu.bitcast(score, I32)
    return jnp.where(b < 0, b ^ jnp.int32(0x7FFFFFFF), b)


def _prefix_matrix(n):
    j = lax.broadcasted_iota(I32, (n, n + LANES), 0)
    s = lax.broadcasted_iota(I32, (n, n + LANES), 1)
    return jnp.where((s >= n) | (j <= s), 1.0, 0.0).astype(BF16)


def _kth_largest_key(count_ge, rows, n_sel):
    def body(t, T):
        cand = T + jnp.left_shift(jnp.int32(1), 31 - t)
        return jnp.where(count_ge(cand) >= n_sel, cand, T)
    return lax.fori_loop(0, 32, body, jnp.full((rows, 1), INT_MIN, I32))


def _dsa_prompt_body(qc_ref, kc_ref, vc_ref, qi_ref, ki_ref, wi_ref, pm_ref, o_ref, keys_ref, *, tq, tk, n_sel):
    i = pl.program_id(1)
    nkb = ((i + 1) * tq + tk - 1) // tk
    M = N_HEADS * tq
    kpos0 = lax.broadcasted_iota(I32, (tq, tk), 1)
    qpos = i * tq + lax.broadcasted_iota(I32, (tq, tk), 0)

    qis = _stack_heads(qi_ref[...], tq)
    wi = wi_ref[...]
    wcol = [wi[:, j:j + 1] * (N_HEADS ** -0.5) for j in range(N_HEADS)]

    def score_body(j, _):
        k0 = pl.multiple_of(j * tk, tk)
        d = jnp.maximum(_dot_nt(qis, ki_ref[pl.ds(k0, tk), :]) * (DH ** -0.5), 0.0)
        score = d[0:tq] * wcol[0]
        for jh in range(1, N_HEADS):
            score = score + d[jh * tq:(jh + 1) * tq] * wcol[jh]
        score = jnp.where(kpos0 + k0 <= qpos, score + 0.0, NEG_BIG)
        keys_ref[j] = _sort_key(score)
        return 0

    lax.fori_loop(0, nkb, score_body, 0)

    def count_ge(cand):
        def body(j, acc):
            m = jnp.where(keys_ref[j] >= cand, 1.0, 0.0)
            for g in range(tk // LANES):
                acc = acc + m[:, g * LANES:(g + 1) * LANES]
            return acc
        acc = lax.fori_loop(0, nkb, body, jnp.zeros((tq, LANES), F32))
        return jnp.sum(acc, axis=1, keepdims=True)

    T = _kth_largest_key(count_ge, tq, n_sel)
    room = n_sel - count_ge(T + 1)

    qcs = _stack_heads(qc_ref[...], tq) * jnp.asarray(DH ** -0.5, BF16)
    pm = pm_ref[...]

    def att_body(j, c):
        m, l, acc, eq_before = c
        k0 = pl.multiple_of(j * tk, tk)
        key = keys_ref[j]
        eq = key == T
        pc = _dot(jnp.where(eq, 1.0, 0.0).astype(BF16), pm)
        rank = pc[:, :tk] + jnp.concatenate([eq_before] * (tk // LANES), axis=1)
        sel = (key > T) | (eq & (rank <= room))
        bias = jnp.where(sel & (kpos0 + k0 <= qpos), 0.0, NEG_BIG)
        s = _dot_nt(qcs, kc_ref[pl.ds(k0, tk), :]) + jnp.concatenate([bias] * N_HEADS, axis=0)
        m_new = jnp.maximum(m, jnp.max(s, axis=1, keepdims=True))
        alpha = jnp.exp(m - m_new)
        p = jnp.exp(s - m_new)
        l = alpha * l + jnp.sum(p, axis=1, keepdims=True)
        acc = alpha * acc + _dot(p.astype(BF16), vc_ref[pl.ds(k0, tk), :])
        return m_new, l, acc, eq_before + pc[:, tk:]

    init = (jnp.full((M, 1), NEG_BIG, F32), jnp.zeros((M, 1), F32),
            jnp.zeros((M, N_HEADS * DH), F32), jnp.zeros((tq, LANES), F32))
    m, l, acc, _ = lax.fori_loop(0, nkb, att_body, init)
    o_ref[...] = _unstack_heads(acc / l, tq).astype(o_ref.dtype)


def _dsa_prompt(rb, pb, pf, B, L, *, tq, tk, n_sel, vc_blk, wi_blk):
    W = N_HEADS * DH
    nq = L // tq
    pm = jnp.where((jnp.arange(tk + LANES)[None, :] >= tk) | (jnp.arange(tk)[:, None] <= jnp.arange(tk + LANES)[None, :]),
                   1.0, 0.0).astype(BF16)
    return pl.pallas_call(
        functools.partial(_dsa_prompt_body, tq=tq, tk=tk, n_sel=n_sel),
        grid=(B, nq),
        in_specs=[pl.BlockSpec((tq, W), lambda b, i: (b * nq + i, 0)),
                  pl.BlockSpec((L, W), lambda b, i: (b, 1)),
                  pl.BlockSpec((L, W), lambda b, i: (b, vc_blk)),
                  pl.BlockSpec((tq, W), lambda b, i: (b * nq + i, 2)),
                  pl.BlockSpec((L, W), lambda b, i: (b, 3)),
                  pl.BlockSpec((tq, LANES), lambda b, i: (b * nq + i, wi_blk)),
                  pl.BlockSpec((tk, tk + LANES), lambda b, i: (0, 0))],
        out_specs=pl.BlockSpec((tq, W), lambda b, i: (b * nq + i, 0)),
        out_shape=jax.ShapeDtypeStruct((B * L, W), BF16),
        scratch_shapes=[pltpu.VMEM((L // tk, tq, tk), I32)],
        compiler_params=_cparams(("parallel", "arbitrary")),
        name="dsa_prompt",
    )(rb, rb, pb, rb, rb, pf, pm)


def _dsa_s_score_body(pt_ref, qi_ref, w_ref, kn_ref, *rest, G, n_new):
    pages = rest[:G]
    o_ref = rest[G]
    p = pl.program_id(1)
    R = 8
    qi = qi_ref[0]
    w = w_ref[0]

    def score(dots):
        d = jnp.maximum(dots * (DH ** -0.5), 0.0) * w
        return (d[0:R] + d[R:2 * R]) + (d[2 * R:3 * R] + d[3 * R:4 * R]) + 0.0

    @pl.when(p < pl.num_programs(1) - 1)
    def _():
        for g in range(G):
            o_ref[0, :, g * PAGE:(g + 1) * PAGE] = score(_dot(qi, pages[g][...].astype(BF16)))

    @pl.when(p == pl.num_programs(1) - 1)
    def _():
        t = lax.broadcasted_iota(I32, (R, PAGE), 0)
        s = lax.broadcasted_iota(I32, (R, PAGE), 1)
        o_ref[0, :, 0:PAGE] = jnp.where((s <= t) & (s < n_new), score(_dot_nt(qi, kn_ref[0])), NEG_BIG)
        if G > 1:
            o_ref[0, :, PAGE:] = jnp.full((R, (G - 1) * PAGE), NEG_BIG, F32)


def _dsa_s_select_body(s_ref, o_ref, *, n_sel):
    R = 8
    key = _sort_key(s_ref[0])
    W = key.shape[1]

    def count_ge(cand):
        return jnp.sum(jnp.where(key >= cand, 1.0, 0.0), axis=1, keepdims=True)

    T = _kth_largest_key(count_ge, R, n_sel)
    room = n_sel - count_ge(T + 1)
    eq = key == T
    gt = key > T
    nt = W // LANES
    eqf = jnp.where(eq, 1.0, 0.0)
    stacked = jnp.concatenate([eqf[:, g * LANES:(g + 1) * LANES] for g in range(nt)], axis=0).astype(BF16)
    pc = _dot(stacked, _prefix_matrix(LANES))
    before = jnp.zeros((R, LANES), F32)
    tiles = []
    for g in range(nt):
        sl = slice(g * LANES, (g + 1) * LANES)
        rank = pc[g * R:(g + 1) * R, :LANES] + before
        tiles.append(jnp.where(gt[:, sl], 1.0, jnp.where(eq[:, sl] & (rank <= room), 1.0, 0.0)))
        before = before + pc[g * R:(g + 1) * R, LANES:]
    o_ref[0] = jnp.concatenate(tiles, axis=1)


def _dsa_s_attend_body(pt_ref, q_ref, kn_ref, vn_ref, mask_ref, maskn_ref, *rest, G, n_new):
    pages = rest[:G]
    o_ref, m_ref, l_ref, acc_ref = rest[G:]
    p = pl.program_id(1)
    W = N_HEADS * DH
    R = 8
    qs = q_ref[0]

    def update(s, v, m, l, acc, v_feature_major=False):
        m_new = jnp.maximum(m, jnp.max(s, axis=1, keepdims=True))
        alpha = jnp.exp(m - m_new)
        pr = jnp.exp(s - m_new)
        pv = _dot_nt(pr.astype(BF16), v) if v_feature_major else _dot(pr.astype(BF16), v)
        return m_new, alpha * l + jnp.sum(pr, axis=1, keepdims=True), alpha * acc + pv

    @pl.when(p == 0)
    def _():
        t = lax.broadcasted_iota(I32, (R, PAGE), 0)
        sidx = lax.broadcasted_iota(I32, (R, PAGE), 1)
        sel = jnp.where((sidx <= t) & (sidx < n_new), maskn_ref[0, :, 0:PAGE], 0.0)
        s = jnp.where(jnp.concatenate([sel] * N_HEADS, axis=0) > 0.5, _dot_nt(qs, kn_ref[0]), NEG_BIG)
        m, l, acc = update(s, vn_ref[0], jnp.full((N_HEADS * R, 1), NEG_BIG, F32),
                           jnp.zeros((N_HEADS * R, 1), F32), jnp.zeros((N_HEADS * R, W), F32))
        m_ref[...] = jnp.broadcast_to(m, m_ref.shape)
        l_ref[...] = jnp.broadcast_to(l, l_ref.shape)
        acc_ref[...] = acc

    kcat = jnp.concatenate([pages[g][:W, :] for g in range(G)], axis=1).astype(BF16)
    vcat = jnp.concatenate([pages[g][W:, :] for g in range(G)], axis=1).astype(BF16)
    sel = jnp.concatenate([mask_ref[0]] * N_HEADS, axis=0)
    s = jnp.where(sel > 0.5, _dot(qs, kcat), NEG_BIG)
    m, l, acc = update(s, vcat, m_ref[:, 0:1], l_ref[:, 0:1], acc_ref[...], v_feature_major=True)
    m_ref[...] = jnp.broadcast_to(m, m_ref.shape)
    l_ref[...] = jnp.broadcast_to(l, l_ref.shape)
    acc_ref[...] = acc

    @pl.when(p == pl.num_programs(1) - 1)
    def _():
        o_ref[0] = _unstack_heads(acc_ref[...] / l_ref[:, 0:1], R).astype(o_ref.dtype)


def _dsa_sample(qi32, w32, kin, qc32, kcn, vcn, cache_kidx, cache_kv, layer, page_table, *, G, n_new, n_sel):
    Bs = qi32.shape[0]
    NP = page_table.shape[1]
    W = N_HEADS * DH
    nstep = NP // G
    width = (nstep + 1) * G * PAGE

    def kidx_spec(g):
        return pl.BlockSpec((None, None, DH, PAGE),
                            lambda b, p, pt: (layer, pt[b, jnp.minimum(p * G + g, NP - 1)], 0, 0))

    scores = pl.pallas_call(
        functools.partial(_dsa_s_score_body, G=G, n_new=n_new),
        grid_spec=pltpu.PrefetchScalarGridSpec(
            num_scalar_prefetch=1, grid=(Bs, nstep + 1),
            in_specs=[pl.BlockSpec((1, 32, DH), lambda b, p, pt: (b, 0, 0)),
                      pl.BlockSpec((1, 32, LANES), lambda b, p, pt: (b, 0, 0)),
                      pl.BlockSpec((1, PAGE, DH), lambda b, p, pt: (b, 0, 0))] + [kidx_spec(g) for g in range(G)],
            out_specs=pl.BlockSpec((1, 8, G * PAGE), lambda b, p, pt: (b, 0, p))),
        out_shape=jax.ShapeDtypeStruct((Bs, 8, width), F32),
        compiler_params=_cparams(("parallel", "arbitrary")),
        name="dsa_sample_score",
    )(page_table, qi32, w32, kin, *([cache_kidx] * G))

    mask = pl.pallas_call(
        functools.partial(_dsa_s_select_body, n_sel=n_sel),
        grid=(Bs,),
        in_specs=[pl.BlockSpec((1, 8, width), lambda b: (b, 0, 0))],
        out_specs=pl.BlockSpec((1, 8, width), lambda b: (b, 0, 0)),
        out_shape=jax.ShapeDtypeStruct((Bs, 8, width), F32),
        compiler_params=_cparams(("parallel",)),
        name="dsa_sample_select",
    )(scores)

    def kv_spec(g):
        return pl.BlockSpec((None, None, 2 * W, PAGE), lambda b, p, pt: (layer, pt[b, p * G + g], 0, 0))

    return pl.pallas_call(
        functools.partial(_dsa_s_attend_body, G=G, n_new=n_new),
        grid_spec=pltpu.PrefetchScalarGridSpec(
            num_scalar_prefetch=1, grid=(Bs, nstep),
            in_specs=[pl.BlockSpec((1, 32, W), lambda b, p, pt: (b, 0, 0)),
                      pl.BlockSpec((1, PAGE, W), lambda b, p, pt: (b, 0, 0)),
                      pl.BlockSpec((1, PAGE, W), lambda b, p, pt: (b, 0, 0)),
                      pl.BlockSpec((1, 8, G * PAGE), lambda b, p, pt: (b, 0, p)),
                      pl.BlockSpec((1, 8, G * PAGE), lambda b, p, pt: (b, 0, nstep))] + [kv_spec(g) for g in range(G)],
            out_specs=pl.BlockSpec((1, 8, W), lambda b, p, pt: (b, 0, 0)),
            scratch_shapes=[pltpu.VMEM((32, LANES), F32), pltpu.VMEM((32, LANES), F32), pltpu.VMEM((32, W), F32)]),
        out_shape=jax.ShapeDtypeStruct((Bs, 8, W), BF16),
        compiler_params=_cparams(("parallel", "arbitrary")),
        name="dsa_sample_attend",
    )(page_table, qc32, kcn, vcn, mask, mask, *([cache_kv] * G))


def _merge_body(oa_ref, ob_ref, oc_ref, g0_ref, g1_ref, g2_ref, x_ref, wa_ref, wb_ref, wc_ref, wo_ref, o_ref):
    merged = (jax.nn.sigmoid(g0_ref[...]) * _dot(oa_ref[...], wa_ref[...])
              + jax.nn.sigmoid(g1_ref[...]) * _dot(ob_ref[...], wb_ref[...])
              + jax.nn.sigmoid(g2_ref[...]) * _dot(oc_ref[...], wc_ref[...]))
    o_ref[...] = x_ref[...] + _dot(merged.astype(BF16), wo_ref[...])


def _merge(oa, ob, oc, pf, x, wa, wb, wc, wo, *, tm, gate_blk):
    T, D = x.shape
    row = lambda i: (i, 0)
    const = lambda i: (0, 0)
    return pl.pallas_call(
        _merge_body,
        grid=(T // tm,),
        in_specs=[pl.BlockSpec((tm, oa.shape[1]), row), pl.BlockSpec((tm, ob.shape[1]), row),
                  pl.BlockSpec((tm, oc.shape[1]), row),
                  pl.BlockSpec((tm, D), lambda i: (i, gate_blk)), pl.BlockSpec((tm, D), lambda i: (i, gate_blk + 1)),
                  pl.BlockSpec((tm, D), lambda i: (i, gate_blk + 2)), pl.BlockSpec((tm, D), row),
                  pl.BlockSpec(wa.shape, const), pl.BlockSpec(wb.shape, const),
                  pl.BlockSpec(wc.shape, const), pl.BlockSpec(wo.shape, const)],
        out_specs=pl.BlockSpec((tm, D), row),
        out_shape=jax.ShapeDtypeStruct((T, D), F32),
        compiler_params=_cparams(("parallel",)),
        name="merge",
    )(oa, ob, oc, pf, pf, pf, x, wa, wb, wc, wo)


def _ffn_down_body(*refs, tm, seq_tiles, expanded, final_norm):
    ua_ref, ub_ref, ha_ref, hb_ref = refs[:4]
    rest = refs[4:]
    if expanded:
        ha2_ref, hb2_ref, t_ref = rest[:3]
        rest = rest[3:]
    cwa_ref, cwb_ref, cba_ref, cbb_ref, wd_ref, x_ref = rest[:6]
    rest = rest[6:]
    if final_norm:
        gf_ref = rest[0]
        rest = rest[1:]
    o_ref, acc_ref = rest
    i = pl.program_id(0)
    k = pl.program_id(1)

    @pl.when(k == 0)
    def _():
        acc_ref[...] = jnp.zeros_like(acc_ref)

    def conv(u_ref, h_ref, h2_ref, cw_ref, cb_ref):
        u = u_ref[...]
        row = lax.broadcasted_iota(I32, u.shape, 0)
        r1 = pltpu.roll(u, 1, axis=0)
        r2 = pltpu.roll(u, 2, axis=0)
        if expanded:
            t = t_ref[...]
            u1 = jnp.where(t >= 1, r1, h_ref[...])
            u2 = jnp.where(t >= 2, r2, h2_ref[...])
        else:
            h = h_ref[...]
            h = jnp.where(i % seq_tiles == 0, jnp.zeros_like(h), h)
            u1 = jnp.where(row == 0, h[7:8], r1)
            u2 = jnp.where(row == 0, h[6:7], jnp.where(row == 1, h[7:8], r2))
        cw = cw_ref[...]
        return cb_ref[...] + cw[0:1] * u2 + cw[1:2] * u1 + cw[2:3] * u

    a = conv(ua_ref, ha_ref, ha2_ref if expanded else None, cwa_ref, cba_ref)
    b = conv(ub_ref, hb_ref, hb2_ref if expanded else None, cwb_ref, cbb_ref)
    acc_ref[...] += _dot((_silu(a) * b).astype(BF16), wd_ref[...])

    @pl.when(k == pl.num_programs(1) - 1)
    def _():
        y = x_ref[...] + acc_ref[...]
        if final_norm:
            ms = jnp.mean(y * y, axis=-1, keepdims=True)
            y = y * lax.rsqrt(ms + EPS) * gf_ref[...]
        o_ref[...] = y


def _ffn_down(u, x, cw, cb, wd, *, tm, tkf, seq_len, prev=None, g_final=None):
    T, D = x.shape
    F = wd.shape[0]
    nk = F // tkf
    expanded = prev is not None
    final_norm = g_final is not None
    seq_tiles = max(seq_len // tm, 1)
    hb8 = tm // 8
    ua = pl.BlockSpec((tm, tkf), lambda i, k: (i, k))
    ub = pl.BlockSpec((tm, tkf), lambda i, k: (i, nk + k))
    in_specs = [ua, ub]
    args = [u, u]
    if expanded:
        p1, p2, tpos = prev
        in_specs += [ua, ub, ua, ub, pl.BlockSpec((tm, 1), lambda i, k: (i, 0))]
        args += [p1, p1, p2, p2, tpos]
    else:
        in_specs += [pl.BlockSpec((8, tkf), lambda i, k: (jnp.maximum(i * hb8 - 1, 0), k)),
                     pl.BlockSpec((8, tkf), lambda i, k: (jnp.maximum(i * hb8 - 1, 0), nk + k))]
        args += [u, u]
    in_specs += [pl.BlockSpec((3, tkf), lambda i, k: (0, k)), pl.BlockSpec((3, tkf), lambda i, k: (0, nk + k)),
                 pl.BlockSpec((1, tkf), lambda i, k: (0, k)), pl.BlockSpec((1, tkf), lambda i, k: (0, nk + k)),
                 pl.BlockSpec((tkf, D), lambda i, k: (k, 0)), pl.BlockSpec((tm, D), lambda i, k: (i, 0))]
    args += [cw, cw, cb.reshape(1, -1), cb.reshape(1, -1), wd, x]
    if final_norm:
        in_specs.append(pl.BlockSpec((1, D), lambda i, k: (0, 0)))
        args.append(g_final.reshape(1, D))
    return pl.pallas_call(
        functools.partial(_ffn_down_body, tm=tm, seq_tiles=seq_tiles, expanded=expanded, final_norm=final_norm),
        grid=(T // tm, nk),
        in_specs=in_specs,
        out_specs=pl.BlockSpec((tm, D), lambda i, k: (i, 0)),
        out_shape=jax.ShapeDtypeStruct((T, D), F32),
        scratch_shapes=[pltpu.VMEM((tm, D), F32)],
        compiler_params=_cparams(("parallel", "arbitrary")),
        name="ffn_down",
    )(*args)


def _rotate_half_cols(w):
    D, N = w.shape
    w4 = w.reshape(D, N // DH, 2, DH // 2)
    return jnp.concatenate([-w4[:, :, 1], w4[:, :, 0]], axis=-1).reshape(D, N)


def _rope_tables(pos, width):
    half = DH // 2
    inv_freq = ROPE_THETA ** (-jnp.arange(half, dtype=F32) / half)
    ang = pos.astype(F32)[:, None] * inv_freq[None, :]
    reps = width // half
    return jnp.tile(jnp.cos(ang), (1, reps)), jnp.tile(jnp.sin(ang), (1, reps))


def _pick(n, prefs):
    for t in prefs:
        if n % t == 0:
            return t
    return n


def kernel(x_prompt, x_sample, cache_kv_a, cache_kv_c, cache_kidx_c, state_hgrn, state_ffn_conv, page_table,
           norm_mix, w_in, hgrn_lb_logits, hgrn_gnorm, w_br_a, w_br_b, w_br_c, w_out, norm_ffn, w_up, conv_w,
           conv_b, w_down, norm_final):
    B, L, D = x_prompt.shape
    Bs, Ls, _ = x_sample.shape
    depth = w_in.shape[0]
    NP = page_table.shape[1]
    past = NP * PAGE
    F = w_down.shape[1]
    WA = N_HEADS * DH
    WB = N_HEADS * DK
    Tp, Ts = B * L, Bs * Ls
    n_pool = cache_kv_a.shape[1]

    lb_soft = jax.nn.softmax(hgrn_lb_logits.astype(F32), axis=0)
    lb_all = jnp.cumsum(lb_soft, axis=0) - lb_soft[0]

    sizes = (WA, WA, WA, WB, WB, WB, WB, WA, WA, WA, N_HEADS * DH, DH, N_HEADS, 3 * D)
    offs = [0]
    for s in sizes:
        offs.append(offs[-1] + s)
    (o_qa, o_ka, o_va, o_qh, o_fh, o_ih, o_gh, o_qc, o_kc, o_vc, o_qi, o_ki, o_wi, o_gt, o_end) = offs
    QA_BLK, KA_BLK, VA_BLK, VC_BLK = (4 * WB) // WA, (4 * WB) // WA + 1, (4 * WB) // WA + 2, (4 * WB) // WA + 3
    gate_off = 4 * WB + 4 * WA
    assert gate_off % D == 0
    GATE_BLK = gate_off // D
    NPJ = gate_off + 3 * D
    N_ROPE_TILES = 4
    NRJ = (N_ROPE_TILES + 1) * WA
    wi_off = N_ROPE_TILES * WA
    WI_BLK = wi_off // LANES

    cache_a = cache_kv_a.transpose(0, 1, 3, 4, 5, 2).reshape(depth, n_pool, 2 * WA, PAGE)
    cache_c = cache_kv_c.transpose(0, 1, 3, 4, 5, 2).reshape(depth, n_pool, 2 * WA, PAGE)
    cache_i = cache_kidx_c.transpose(0, 1, 3, 2)

    cos_p, sin_p = _rope_tables(jnp.arange(L, dtype=I32), WA)
    cos_s, sin_s = _rope_tables(past + (jnp.arange(Ts, dtype=I32) % Ls), WA)

    tm_p = _pick(Tp, (1024, 512, 256, 128))
    tm_p = min(tm_p, L)
    tn_main = _pick(NPJ, (512, 256, 128))
    tn_up = _pick(2 * F, (512, 256, 128))
    tkf = _pick(F, (256, 128))
    tq_a = _pick(L, (256, 128))
    tq_c = _pick(L, (128,))
    tk_c = _pick(L, (512, 256, 128))
    C_h = _pick(L, (64,))
    G = _pick(NP, (8, 4, 2, 1))
    n_sel_p = min(TOPK_MAX, L // 4)
    n_sel_s = min(TOPK_MAX, (past + Ls) // 4)

    xp = x_prompt.reshape(Tp, D)
    xs = x_sample.reshape(Ts, D)
    tpos_s = (jnp.arange(Ts, dtype=I32) % Ls).reshape(Ts, 1)

    def pad_rows(a, rows):
        return jnp.pad(a, ((0, 0), (0, rows - a.shape[1]), (0, 0)))

    outs_p = ([], [], [], [], [])
    outs_s = ([], [], [], [], [])
    for l in range(depth):
        w = w_in[l]
        col = lambda o, n: w[:, o:o + n]
        w_main = jnp.concatenate(
            [col(o_qh, 4 * WB), col(o_qa, 3 * WA), col(o_vc, WA), col(o_gt, 3 * D)], axis=1).astype(BF16)
        w_r = jnp.concatenate([col(o_qc, WA), col(o_kc, WA), col(o_qi, WA)] + [col(o_ki, DH)] * N_HEADS, axis=1)
        w_tail = jnp.concatenate([col(o_wi, N_HEADS), jnp.zeros((D, WA - N_HEADS), w.dtype)], axis=1)
        w_rot = jnp.concatenate([_rotate_half_cols(w_r), jnp.zeros((D, WA), w.dtype)], axis=1).astype(BF16)
        w_r = jnp.concatenate([w_r, w_tail], axis=1).astype(BF16)
        wa_b, wb_b, wc_b, wo_b = (t[l].astype(BF16) for t in (w_br_a, w_br_b, w_br_c, w_out))
        wup_b = w_up[l].astype(BF16)
        wd_b = w_down[l].astype(BF16)
        last = l == depth - 1

        pf, pb = _rms_proj(xp, norm_mix[l], w_main, tm=tm_p, tn=tn_main)
        rf, rb = _rms_proj(xp, norm_mix[l], w_r, tm=tm_p, tn=WA, rope_args=(w_rot, cos_p, sin_p),
                           n_rope_tiles=N_ROPE_TILES)
        oa = _attn_a_prompt(pb, B, L, tq=tq_a, q_blk=QA_BLK, k_blk=KA_BLK, v_blk=VA_BLK)
        ob, s_new = _hgrn(pf.reshape(B, L, NPJ), lb_all[l], hgrn_gnorm[l], None, C=C_h, c=min(16, C_h), valid=C_h)
        oc = _dsa_prompt(rb, pb, rf, B, L, tq=tq_c, tk=tk_c, n_sel=n_sel_p, vc_blk=VC_BLK, wi_blk=WI_BLK)
        x1 = _merge(oa, ob.reshape(Tp, WB), oc, pf, xp, wa_b, wb_b, wc_b, wo_b, tm=min(512, tm_p), gate_blk=GATE_BLK)
        u = _rms_proj(x1, norm_ffn[l], wup_b, tm=tm_p, tn=tn_up, emit_bf16=False)
        xp = _ffn_down(u, x1, conv_w[l], conv_b[l], wd_b, tm=min(512, tm_p), tkf=tkf, seq_len=L,
                       g_final=norm_final if last else None)
        kv_off = (KA_BLK * WA)
        outs_p[0].append(pf[:, kv_off:kv_off + 2 * WA].reshape(B, L, 2, N_HEADS, DH))
        outs_p[1].append(jnp.stack([rf[:, WA:2 * WA].reshape(B, L, N_HEADS, DH),
                                    pf[:, VC_BLK * WA:(VC_BLK + 1) * WA].reshape(B, L, N_HEADS, DH)], axis=2))
        outs_p[2].append(rf[:, 3 * WA:3 * WA + DH].reshape(B, L, DH))
        outs_p[3].append(s_new)
        outs_p[4].append(u.reshape(B, L, 2 * F)[:, L - 2:])

        pf, pb = _rms_proj(xs, norm_mix[l], w_main, tm=Ts, tn=tn_main)
        rf, rb = _rms_proj(xs, norm_mix[l], w_r, tm=Ts, tn=WA, rope_args=(w_rot, cos_s, sin_s),
                           n_rope_tiles=N_ROPE_TILES)
        pb3 = pb.reshape(Bs, Ls, NPJ)
        rb3 = rb.reshape(Bs, Ls, NRJ)
        blk = lambda a, k: a[:, :, k * WA:(k + 1) * WA]
        oa8 = _attn_a_sample(pad_rows(blk(pb3, QA_BLK), 8), pad_rows(blk(pb3, KA_BLK), PAGE),
                             pad_rows(blk(pb3, VA_BLK), PAGE), cache_a, l, page_table, G=G)
        ph = jnp.pad(pf[:, :4 * WB].reshape(Bs, Ls, 4 * WB), ((0, 0), (0, 8 - Ls), (0, 0)))
        ob8, s_new = _hgrn(ph, lb_all[l], hgrn_gnorm[l], state_hgrn[l], C=8, c=8, valid=Ls)
        qi = blk(rb3, 2).reshape(Bs, Ls, N_HEADS, DH).transpose(0, 2, 1, 3)
        qi32 = jnp.pad(qi, ((0, 0), (0, 0), (0, 8 - Ls), (0, 0))).reshape(Bs, 4 * 8, DH)
        wi = rf[:, wi_off:wi_off + N_HEADS].reshape(Bs, Ls, N_HEADS).transpose(0, 2, 1) * (N_HEADS ** -0.5)
        w32 = jnp.broadcast_to(jnp.pad(wi, ((0, 0), (0, 0), (0, 8 - Ls))).reshape(Bs, 32, 1), (Bs, 32, LANES))
        kin = pad_rows(rb3[:, :, 3 * WA:3 * WA + DH], PAGE)
        qc8 = pad_rows(blk(rb3, 0), 8)
        head = (jnp.arange(WA) // DH)[None, None, None, :] == jnp.arange(N_HEADS)[None, :, None, None]
        qc32 = (jnp.where(head, qc8[:, None], 0) * jnp.asarray(DH ** -0.5, BF16)).reshape(Bs, 32, WA).astype(BF16)
        oc8 = _dsa_sample(qi32, w32, kin, qc32, pad_rows(blk(rb3, 1), PAGE), pad_rows(blk(pb3, VC_BLK), PAGE),
                          cache_i, cache_c, l, page_table, G=G, n_new=Ls, n_sel=n_sel_s)
        x1 = _merge(oa8[:, :Ls].reshape(Ts, WA), ob8[:, :Ls].reshape(Ts, WB), oc8[:, :Ls].reshape(Ts, WA), pf, xs,
                    wa_b, wb_b, wc_b, wo_b, tm=Ts, gate_blk=GATE_BLK)
        u = _rms_proj(x1, norm_ffn[l], wup_b, tm=Ts, tn=tn_up, emit_bf16=False)
        prev = state_ffn_conv[l]
        u3 = u.reshape(Bs, Ls, 2 * F)
        p1 = jnp.broadcast_to(prev[:, 1:2], (Bs, Ls, 2 * F)).reshape(Ts, 2 * F)
        p2 = jnp.concatenate([prev, jnp.zeros((Bs, Ls - 2, 2 * F), F32)], axis=1).reshape(Ts, 2 * F)
        xs = _ffn_down(u, x1, conv_w[l], conv_b[l], wd_b, tm=Ts, tkf=tkf, seq_len=Ls, prev=(p1, p2, tpos_s),
                       g_final=norm_final if last else None)
        kv_off = (KA_BLK * WA)
        outs_s[0].append(pf[:, kv_off:kv_off + 2 * WA].reshape(Bs, Ls, 2, N_HEADS, DH))
        outs_s[1].append(jnp.stack([rf[:, WA:2 * WA].reshape(Bs, Ls, N_HEADS, DH),
                                    pf[:, VC_BLK * WA:(VC_BLK + 1) * WA].reshape(Bs, Ls, N_HEADS, DH)], axis=2))
        outs_s[2].append(rf[:, 3 * WA:3 * WA + DH].reshape(Bs, Ls, DH))
        outs_s[3].append(s_new)
        outs_s[4].append(jnp.concatenate([prev, u3], axis=1)[:, Ls:])

    st = lambda xs_: jnp.stack(xs_, axis=0)
    return (xp.reshape(B, L, D), xs.reshape(Bs, Ls, D),
            st(outs_p[0]), st(outs_s[0]), st(outs_p[1]), st(outs_s[1]), st(outs_p[2]), st(outs_s[2]),
            st(outs_p[3]), st(outs_s[3]), st(outs_p[4]), st(outs_s[4]))
```

```python
import functools

import jax
import jax.numpy as jnp
from jax import lax
from jax.experimental import pallas as pl
from jax.experimental.pallas import tpu as pltpu

F32 = jnp.float32
BF16 = jnp.bfloat16
I32 = jnp.int32

EPS = 1e-6
NEG_BIG = -1e30
LOG_F_MIN = -30.0
ROPE_THETA = 10000.0
TOPK_MAX = 256
PAGE = 128
N_HEADS = 4
DH = 64
DK = 128
LANES = 128
VMEM_LIMIT = 56 * 1024 * 1024
INT_MIN = -(2 ** 31)


def _cparams(sem):
    return pltpu.CompilerParams(dimension_semantics=sem, vmem_limit_bytes=VMEM_LIMIT)


def _dot(a, b):
    return jnp.dot(a, b, preferred_element_type=F32)


def _dot_nt(a, b):
    return lax.dot_general(a, b, (((1,), (1,)), ((), ())), preferred_element_type=F32)


def _dot_tn(a, b):
    return lax.dot_general(a, b, (((0,), (0,)), ((), ())), preferred_element_type=F32)


def _silu(x):
    return x * jax.nn.sigmoid(x)


def _stack_heads(q, rows):
    head = lax.broadcasted_iota(I32, (rows, N_HEADS * DH), 1) // DH
    return jnp.concatenate([jnp.where(head == h, q, jnp.zeros_like(q)) for h in range(N_HEADS)], axis=0)


def _unstack_heads(acc, rows):
    head = lax.broadcasted_iota(I32, (rows, N_HEADS * DH), 1) // DH
    out = jnp.zeros((rows, N_HEADS * DH), acc.dtype)
    for h in range(N_HEADS):
        out = jnp.where(head == h, acc[h * rows:(h + 1) * rows], out)
    return out


def _proj_body(*refs, rope, emit_bf16, n_rope_tiles, t_outs):
    x_ref, g_ref, w_ref = refs[:3]
    rest = refs[3:]
    if rope:
        wrot_ref, cos_ref, sin_ref = rest[:3]
        rest = rest[3:]
    of_ref = rest[0]
    ob_ref = rest[1] if emit_bf16 else None
    t_refs = rest[(2 if emit_bf16 else 1):-1]
    h_ref = rest[-1]

    @pl.when(pl.program_id(1) == 0)
    def _():
        x = x_ref[...]
        ms = jnp.mean(x * x, axis=-1, keepdims=True)
        h_ref[...] = (x * lax.rsqrt(ms + EPS) * g_ref[...]).astype(BF16)

    h = h_ref[...]
    acc = _dot(h, w_ref[...])
    if rope:
        roped = acc * cos_ref[...] + _dot(h, wrot_ref[...]) * sin_ref[...]
        acc = jnp.where(pl.program_id(1) < n_rope_tiles, roped, acc)
    of_ref[...] = acc
    if emit_bf16:
        ob_ref[...] = acc.astype(BF16)
    for t_ref, (j_tile, c0, width, keep, sub, dtype) in zip(t_refs, t_outs):
        @pl.when(pl.program_id(1) == j_tile)
        def _(t_ref=t_ref, c0=c0, width=width, keep=keep, sub=sub, dtype=dtype):
            for r in range(acc.shape[0] // sub):
                t_ref[r] = acc[r * sub:(r + 1) * sub, c0:c0 + width].T[:keep].astype(dtype)


def _rms_proj(x, g, w, *, tm, tn, rope_args=None, emit_bf16=True, n_rope_tiles=0, t_outs=()):
    T, D = x.shape
    N = w.shape[1]
    rope = rope_args is not None
    in_specs = [pl.BlockSpec((tm, D), lambda i, j: (i, 0)),
                pl.BlockSpec((1, D), lambda i, j: (0, 0)),
                pl.BlockSpec((D, tn), lambda i, j: (0, j))]
    args = [x, g.reshape(1, D), w]
    if rope:
        w_rot, cos, sin = rope_args
        nblk = cos.shape[0] // tm
        in_specs += [pl.BlockSpec((D, tn), lambda i, j: (0, j)),
                     pl.BlockSpec((tm, tn), lambda i, j: (i % nblk, 0)),
                     pl.BlockSpec((tm, tn), lambda i, j: (i % nblk, 0))]
        args += [w_rot, cos, sin]
    out_shape = [jax.ShapeDtypeStruct((T, N), F32)]
    out_specs = [pl.BlockSpec((tm, tn), lambda i, j: (i, j))]
    if emit_bf16:
        out_shape.append(jax.ShapeDtypeStruct((T, N), BF16))
        out_specs.append(pl.BlockSpec((tm, tn), lambda i, j: (i, j)))
    for (_, _, _, keep, sub, dtype) in t_outs:
        out_shape.append(jax.ShapeDtypeStruct((T // sub, keep, sub), dtype))
        out_specs.append(pl.BlockSpec((tm // sub, keep, sub), lambda i, j: (i, 0, 0)))
    outs = pl.pallas_call(
        functools.partial(_proj_body, rope=rope, emit_bf16=emit_bf16, n_rope_tiles=n_rope_tiles, t_outs=tuple(t_outs)),
        grid=(T // tm, N // tn),
        in_specs=in_specs, out_specs=out_specs, out_shape=out_shape,
        scratch_shapes=[pltpu.VMEM((tm, D), BF16)],
        compiler_params=_cparams(("parallel", "arbitrary")),
        name="rms_proj_rope" if rope else "rms_proj",
    )(*args)
    return outs if (emit_bf16 or t_outs) else outs[0]


def _suffix_matrix():
    j = lax.broadcasted_iota(I32, (2 * LANES, LANES), 0) & (LANES - 1)
    s = lax.broadcasted_iota(I32, (2 * LANES, LANES), 1)
    return jnp.where(j > s, 1.0, 0.0).astype(BF16)


def _log_one_minus_beta(z):
    return -(jnp.maximum(z, 0.0) + jnp.log(1.0 + jnp.exp(-jnp.abs(z))))


def _sb_update(z, carry, suffix, vis):
    M, tk = z.shape
    ls = _log_one_minus_beta(z)
    if vis is not None:
        ls = jnp.where(vis, ls, 0.0)
    hi = ls.astype(BF16)
    lo = (ls - hi.astype(F32)).astype(BF16)
    lz = z + ls
    n = tk // LANES
    outs = [None] * n
    for g in reversed(range(n)):
        sl = slice(g * LANES, (g + 1) * LANES)
        between = _dot(jnp.concatenate([hi[:, sl], lo[:, sl]], axis=1), suffix)
        e = lz[:, sl] + between + carry
        if vis is not None:
            e = jnp.where(vis[:, sl], e, NEG_BIG)
        outs[g] = jnp.exp(e)
        carry = carry + jnp.sum(ls[:, sl], axis=1, keepdims=True)
    a = outs[0] if n == 1 else jnp.concatenate(outs, axis=1)
    return a.astype(BF16), carry


def _attn_a_prompt_body(q_ref, k_ref, v_ref, o_ref, *, tq):
    i = pl.program_id(1)
    M = N_HEADS * tq
    qs = _stack_heads(q_ref[...], tq) * jnp.asarray(DH ** -0.5, BF16)
    suffix = _suffix_matrix()

    def block(k0, carry, acc, vis):
        kb = k_ref[pl.ds(k0, tq), :]
        vb = v_ref[pl.ds(k0, tq), :]
        a, carry = _sb_update(_dot_nt(qs, kb), carry, suffix, vis)
        return carry, acc + _dot(a, vb)

    row_t = lax.broadcasted_iota(I32, (M, tq), 0) & (tq - 1)
    col_s = lax.broadcasted_iota(I32, (M, tq), 1)
    carry, acc = block(pl.multiple_of(i * tq, tq), jnp.zeros((M, 1), F32),
                       jnp.zeros((M, N_HEADS * DH), F32), col_s < row_t)

    def body(step, c):
        k0 = pl.multiple_of((i - 1 - step) * tq, tq)
        return block(k0, c[0], c[1], None)

    carry, acc = lax.fori_loop(0, i, body, (carry, acc))
    o_ref[...] = _unstack_heads(acc, tq).astype(o_ref.dtype)


def _attn_a_prompt(pb, B, L, *, tq, q_blk, k_blk, v_blk):
    W = N_HEADS * DH
    nq = L // tq
    return pl.pallas_call(
        functools.partial(_attn_a_prompt_body, tq=tq),
        grid=(B, nq),
        in_specs=[pl.BlockSpec((tq, W), lambda b, i: (b * nq + i, q_blk)),
                  pl.BlockSpec((L, W), lambda b, i: (b, k_blk)),
                  pl.BlockSpec((L, W), lambda b, i: (b, v_blk))],
        out_specs=pl.BlockSpec((tq, W), lambda b, i: (b * nq + i, 0)),
        out_shape=jax.ShapeDtypeStruct((B * L, W), BF16),
        compiler_params=_cparams(("parallel", "arbitrary")),
        name="attn_a_prompt",
    )(pb, pb, pb)


def _attn_a_sample_body(pt_ref, q_ref, kn_ref, vn_ref, *rest, G):
    pages = rest[:G]
    o_ref, carry_ref, acc_ref = rest[G:]
    p = pl.program_id(1)
    W = N_HEADS * DH
    R = 8
    M = N_HEADS * R
    qs = _stack_heads(q_ref[0], R) * jnp.asarray(DH ** -0.5, BF16)
    suffix = _suffix_matrix()

    @pl.when(p == 0)
    def _():
        row_t = lax.broadcasted_iota(I32, (M, PAGE), 0) & (R - 1)
        col_s = lax.broadcasted_iota(I32, (M, PAGE), 1)
        a, carry = _sb_update(_dot_nt(qs, kn_ref[0]), jnp.zeros((M, 1), F32), suffix, col_s < row_t)
        carry_ref[...] = jnp.broadcast_to(carry, carry_ref.shape)
        acc_ref[...] = _dot(a, vn_ref[0])

    kcat = jnp.concatenate([pages[g][:W, :] for g in reversed(range(G))], axis=1).astype(BF16)
    vcat = jnp.concatenate([pages[g][W:, :] for g in reversed(range(G))], axis=1).astype(BF16)
    a, carry = _sb_update(_dot(qs, kcat), carry_ref[:, 0:1], suffix, None)
    carry_ref[...] = jnp.broadcast_to(carry, carry_ref.shape)
    acc_ref[...] += _dot_nt(a, vcat)

    @pl.when(p == pl.num_programs(1) - 1)
    def _():
        o_ref[0] = _unstack_heads(acc_ref[...], R).astype(o_ref.dtype)


def _attn_a_sample(q8, kn, vn, cache, layer, page_table, *, G):
    Bs = q8.shape[0]
    NP = page_table.shape[1]
    W = N_HEADS * DH

    def page_spec(g):
        return pl.BlockSpec((None, None, 2 * W, PAGE),
                            lambda b, p, pt: (layer, pt[b, NP - 1 - (p * G + g)], 0, 0))

    grid_spec = pltpu.PrefetchScalarGridSpec(
        num_scalar_prefetch=1,
        grid=(Bs, NP // G),
        in_specs=[pl.BlockSpec((1, 8, W), lambda b, p, pt: (b, 0, 0)),
                  pl.BlockSpec((1, PAGE, W), lambda b, p, pt: (b, 0, 0)),
                  pl.BlockSpec((1, PAGE, W), lambda b, p, pt: (b, 0, 0))] + [page_spec(g) for g in range(G)],
        out_specs=pl.BlockSpec((1, 8, W), lambda b, p, pt: (b, 0, 0)),
        scratch_shapes=[pltpu.VMEM((N_HEADS * 8, LANES), F32), pltpu.VMEM((N_HEADS * 8, W), F32)],
    )
    return pl.pallas_call(
        functools.partial(_attn_a_sample_body, G=G),
        grid_spec=grid_spec,
        out_shape=jax.ShapeDtypeStruct((Bs, 8, W), BF16),
        compiler_params=_cparams(("parallel", "arbitrary")),
        name="attn_a_sample",
    )(page_table, q8, kn, vn, *([cache] * G))


def _cumsum_rows(x):
    C = x.shape[0]
    row = lax.broadcasted_iota(I32, x.shape, 0)
    sh = 1
    while sh < C:
        x = x + jnp.where(row >= sh, pltpu.roll(x, sh, axis=0), 0.0)
        sh *= 2
    return x


def _hgrn_body(*refs, C, c, valid, has_s0):
    q_ref, f_ref, i_ref, g_ref, lb_ref, gn_ref = refs[:6]
    rest = refs[6:]
    if has_s0:
        s0_ref = rest[0]
        rest = rest[1:]
    o_ref, sout_ref, st_ref = rest
    ci = pl.program_id(1)

    @pl.when(ci == 0)
    def _():
        for h in range(N_HEADS):
            st_ref[h] = s0_ref[0, h].T if has_s0 else jnp.zeros((DK, DK), F32)

    row = lax.broadcasted_iota(I32, (C, DK), 0)
    rowc = lax.broadcasted_iota(I32, (c, 1), 0)
    for h in range(N_HEADS):
        hs = slice(h * DK, (h + 1) * DK)
        kk = (1.0 - lb_ref[:, hs]) * jax.nn.sigmoid(-f_ref[0, :, hs])
        lg = jnp.maximum(jnp.log1p(-kk), LOG_F_MIN)
        if valid < C:
            kk = jnp.where(row < valid, kk, 0.0)
            lg = jnp.where(row < valid, lg, 0.0)
        qq = _silu(q_ref[0, :, hs])
        vv = i_ref[0, :, hs]
        cum = _cumsum_rows(lg)
        st = st_ref[h]
        o = _dot_nt((qq * jnp.exp(cum)).astype(BF16), st.astype(BF16))
        parts = []
        for blk in range(C // c):
            r0 = blk * c
            q_b = qq[r0:r0 + c]
            cum_b = cum[r0:r0 + c]
            o_b = jnp.zeros((c, DK), F32)
            if blk > 0:
                base = cum[r0 - 1:r0]
                qt = q_b * jnp.exp(cum_b - base)
                kt = kk[:r0] * jnp.exp(base - cum[:r0])
                sc = _dot_nt(qt.astype(BF16), kt.astype(BF16))
                o_b = o_b + _dot(sc.astype(BF16), vv[:r0].astype(BF16))
            for s in range(c):
                r = r0 + s
                d = jnp.minimum(cum_b - cum[r:r + 1], 0.0)
                w = jnp.sum(q_b * kk[r:r + 1] * jnp.exp(d), axis=1, keepdims=True)
                o_b = o_b + jnp.where(rowc >= s, w, 0.0) * vv[r:r + 1]
            parts.append(o_b)
        o = o + (parts[0] if len(parts) == 1 else jnp.concatenate(parts, axis=0))
        last = cum[C - 1:C]
        kd = kk * jnp.exp(last - cum)
        st_ref[h] = st * jnp.exp(last) + _dot_tn(vv.astype(BF16), kd.astype(BF16))
        ms = jnp.mean(o * o, axis=1, keepdims=True)
        y = o * lax.rsqrt(ms + EPS) * gn_ref[...] * _silu(g_ref[0, :, hs])
        o_ref[0, :, hs] = y.astype(o_ref.dtype)

    @pl.when(ci == pl.num_programs(1) - 1)
    def _():
        for h in range(N_HEADS):
            sout_ref[0, h] = st_ref[h].T


def _hgrn(p3, lb, gn, s0, *, C, c, valid, col0=0):
    B, L, _ = p3.shape
    W = N_HEADS * DK
    has_s0 = s0 is not None
    in_specs = [pl.BlockSpec((1, C, W), functools.partial(lambda b, ci, k: (b, ci, col0 + k), k=k)) for k in range(4)]
    in_specs += [pl.BlockSpec((1, W), lambda b, ci: (0, 0)), pl.BlockSpec((1, DK), lambda b, ci: (0, 0))]
    args = [p3, p3, p3, p3, lb.reshape(1, W), gn.reshape(1, DK)]
    if has_s0:
        in_specs.append(pl.BlockSpec((1, N_HEADS, DK, DK), lambda b, ci: (b, 0, 0, 0)))
        args.append(s0)
    return pl.pallas_call(
        functools.partial(_hgrn_body, C=C, c=c, valid=valid, has_s0=has_s0),
        grid=(B, L // C),
        in_specs=in_specs,
        out_specs=[pl.BlockSpec((1, C, W), lambda b, ci: (b, ci, 0)),
                   pl.BlockSpec((1, N_HEADS, DK, DK), lambda b, ci: (b, 0, 0, 0))],
        out_shape=[jax.ShapeDtypeStruct((B, L, W), BF16), jax.ShapeDtypeStruct((B, N_HEADS, DK, DK), F32)],
        scratch_shapes=[pltpu.VMEM((N_HEADS, DK, DK), F32)],
        compiler_params=_cparams(("parallel", "arbitrary")),
        name="hgrn2",
    )(*args)


def _sort_key(score):
    b = pltpu.bitcast(score, I32)
    return jnp.where(b < 0, b ^ jnp.int32(0x7FFFFFFF), b)


def _prefix_matrix(n):
    j = lax.broadcasted_iota(I32, (n, n + LANES), 0)
    s = lax.broadcasted_iota(I32, (n, n + LANES), 1)
    return jnp.where((s >= n) | (j <= s), 1.0, 0.0).astype(BF16)


def _kth_largest_key(count_ge, rows, n_sel):
    def body(t, T):
        cand = T + jnp.left_shift(jnp.int32(1), 31 - t)
        return jnp.where(count_ge(cand) >= n_sel, cand, T)
    return lax.fori_loop(0, 32, body, jnp.full((rows, 1), INT_MIN, I32))


def _dsa_prompt_body(qcT_ref, qiT_ref, wiT_ref, kc_ref, ki_ref, vT_ref, lt_ref, o_ref, keys_ref, kh_ref, kl_ref,
                     *, tq, tk, n_sel):
    i = pl.program_id(1)
    nkb = ((i + 1) * tq + tk - 1) // tk
    M = N_HEADS * tq
    W = N_HEADS * DH
    I16 = jnp.int16
    kpos0 = lax.broadcasted_iota(I32, (tk, tq), 0)
    qpos = i * tq + lax.broadcasted_iota(I32, (tk, tq), 1)
    row_head = lax.broadcasted_iota(I32, (W, tq), 0) // DH

    def stack_lanes(qT):
        return jnp.concatenate([jnp.where(row_head == h, qT, jnp.zeros_like(qT)) for h in range(N_HEADS)], axis=1)

    qisT = stack_lanes(qiT_ref[0])
    wi = wiT_ref[0]
    w_row = jnp.concatenate([wi[j:j + 1, :] for j in range(N_HEADS)], axis=1) * (N_HEADS ** -0.5) * (DH ** -0.5)

    def score_body(j, _):
        k0 = pl.multiple_of(j * tk, tk)
        d = jnp.maximum(_dot(ki_ref[pl.ds(k0, tk), :], qisT), 0.0) * w_row
        score = d[:, 0:tq]
        for jh in range(1, N_HEADS):
            score = score + d[:, jh * tq:(jh + 1) * tq]
        score = jnp.where(kpos0 + k0 <= qpos, score + 0.0, NEG_BIG)
        key = _sort_key(score)
        keys_ref[j] = key
        kh_ref[j] = (key >> 16).astype(I16)
        kl_ref[j] = ((key & 0xFFFF) - 32768).astype(I16)
        return 0

    lax.fori_loop(0, nkb, score_body, 0)

    def count16(ref, cand):
        c16 = cand.astype(I16)

        def body(j, acc):
            m = jnp.where(ref[j] >= c16, jnp.ones((), BF16), jnp.zeros((), BF16))
            parts = [m[16 * r:16 * (r + 1)] for r in range(tk // 16)]
            while len(parts) > 1:
                parts = [parts[a] + parts[a + 1] for a in range(0, len(parts) - 1, 2)] + parts[len(parts) & ~1:]
            return acc + parts[0].astype(F32)

        acc = lax.fori_loop(0, nkb, body, jnp.zeros((16, tq), F32))
        return jnp.sum(acc, axis=0, keepdims=True)

    def kth16(ref, target):
        def body(t, T):
            cand = T + jnp.left_shift(jnp.int32(1), 15 - t)
            return jnp.where(count16(ref, cand) >= target, cand, T)
        return lax.fori_loop(0, 16, body, jnp.full((1, tq), -32768, I32))

    def count_above(ref, T):
        return jnp.where(T >= 32767, 0.0, count16(ref, jnp.minimum(T + 1, 32767)))

    TH = kth16(kh_ref, float(n_sel))
    need = n_sel - count_above(kh_ref, TH)
    th16 = TH.astype(I16)

    def low_body(j, _):
        kl_ref[j] = jnp.where(kh_ref[j] == th16, kl_ref[j], jnp.full((), -32768, I16))
        return 0

    lax.fori_loop(0, nkb, low_body, 0)
    TL = kth16(kl_ref, need)
    room = need - count_above(kl_ref, TL)
    T = TH * 65536 + (TL + 32768)

    qcsT = stack_lanes(qcT_ref[0]) * jnp.asarray(DH ** -0.5, BF16)
    lt = lt_ref[...]

    def att_body(j, c):
        m, l, accT, eq_before = c
        k0 = pl.multiple_of(j * tk, tk)
        key = keys_ref[j]
        eq = key == T
        rank = _dot(lt, jnp.where(eq, 1.0, 0.0).astype(BF16)) + eq_before
        sel = (key > T) | (eq & (rank <= room))
        bias = jnp.where(sel & (kpos0 + k0 <= qpos), 0.0, NEG_BIG)
        s = _dot(kc_ref[pl.ds(k0, tk), :], qcsT) + jnp.concatenate([bias] * N_HEADS, axis=1)
        m_new = jnp.maximum(m, jnp.max(s, axis=0, keepdims=True))
        alpha = jnp.exp(m - m_new)
        p = jnp.exp(s - m_new)
        l = alpha * l + jnp.sum(p, axis=0, keepdims=True)
        accT = alpha * accT + _dot(vT_ref[j], p.astype(BF16))
        return m_new, l, accT, rank[tk - 1:tk, :]

    init = (jnp.full((1, M), NEG_BIG, F32), jnp.zeros((1, M), F32), jnp.zeros((W, M), F32), jnp.zeros((1, tq), F32))
    m, l, accT, _ = lax.fori_loop(0, nkb, att_body, init)
    outT = accT / l
    oT = jnp.zeros((W, tq), F32)
    for h in range(N_HEADS):
        oT = jnp.where(row_head == h, outT[:, h * tq:(h + 1) * tq], oT)
    o_ref[...] = oT.T.astype(o_ref.dtype)


def _dsa_prompt(qcT, qiT, wiT, rb, vT, B, L, *, tq, tk, n_sel):
    W = N_HEADS * DH
    nq = L // tq
    assert tk // 16 <= 256
    lt = jnp.where(jnp.arange(tk)[None, :] <= jnp.arange(tk)[:, None], 1.0, 0.0).astype(BF16)
    return pl.pallas_call(
        functools.partial(_dsa_prompt_body, tq=tq, tk=tk, n_sel=n_sel),
        grid=(B, nq),
        in_specs=[pl.BlockSpec((1, W, tq), lambda b, i: (b * nq + i, 0, 0)),
                  pl.BlockSpec((1, W, tq), lambda b, i: (b * nq + i, 0, 0)),
                  pl.BlockSpec((1, 8, tq), lambda b, i: (b * nq + i, 0, 0)),
                  pl.BlockSpec((L, W), lambda b, i: (b, 1)),
                  pl.BlockSpec((L, W), lambda b, i: (b, 3)),
                  pl.BlockSpec((L // tk, W, tk), lambda b, i: (b, 0, 0)),
                  pl.BlockSpec((tk, tk), lambda b, i: (0, 0))],
        out_specs=pl.BlockSpec((tq, W), lambda b, i: (b * nq + i, 0)),
        out_shape=jax.ShapeDtypeStruct((B * L, W), BF16),
        scratch_shapes=[pltpu.VMEM((L // tk, tk, tq), I32), pltpu.VMEM((L // tk, tk, tq), jnp.int16),
                        pltpu.VMEM((L // tk, tk, tq), jnp.int16)],
        compiler_params=_cparams(("parallel", "arbitrary")),
        name="dsa_prompt",
    )(qcT, qiT, wiT, rb, rb, vT, lt)


def _dsa_s_score_body(pt_ref, qi_ref, w_ref, kn_ref, *rest, G, n_new):
    pages = rest[:G]
    o_ref = rest[G]
    p = pl.program_id(1)
    R = 8
    qi = qi_ref[0]
    w = w_ref[0]

    def score(dots):
        d = jnp.maximum(dots * (DH ** -0.5), 0.0) * w
        return (d[0:R] + d[R:2 * R]) + (d[2 * R:3 * R] + d[3 * R:4 * R]) + 0.0

    @pl.when(p < pl.num_programs(1) - 1)
    def _():
        for g in range(G):
            o_ref[0, :, g * PAGE:(g + 1) * PAGE] = score(_dot(qi, pages[g][...].astype(BF16)))

    @pl.when(p == pl.num_programs(1) - 1)
    def _():
        t = lax.broadcasted_iota(I32, (R, PAGE), 0)
        s = lax.broadcasted_iota(I32, (R, PAGE), 1)
        o_ref[0, :, 0:PAGE] = jnp.where((s <= t) & (s < n_new), score(_dot_nt(qi, kn_ref[0])), NEG_BIG)
        if G > 1:
            o_ref[0, :, PAGE:] = jnp.full((R, (G - 1) * PAGE), NEG_BIG, F32)


def _dsa_s_select_body(s_ref, o_ref, *, n_sel):
    R = 8
    key = _sort_key(s_ref[0])
    W = key.shape[1]

    def count_ge(cand):
        return jnp.sum(jnp.where(key >= cand, 1.0, 0.0), axis=1, keepdims=True)

    T = _kth_largest_key(count_ge, R, n_sel)
    room = n_sel - count_ge(T + 1)
    eq = key == T
    gt = key > T
    nt = W // LANES
    eqf = jnp.where(eq, 1.0, 0.0)
    stacked = jnp.concatenate([eqf[:, g * LANES:(g + 1) * LANES] for g in range(nt)], axis=0).astype(BF16)
    pc = _dot(stacked, _prefix_matrix(LANES))
    before = jnp.zeros((R, LANES), F32)
    tiles = []
    for g in range(nt):
        sl = slice(g * LANES, (g + 1) * LANES)
        rank = pc[g * R:(g + 1) * R, :LANES] + before
        tiles.append(jnp.where(gt[:, sl], 1.0, jnp.where(eq[:, sl] & (rank <= room), 1.0, 0.0)))
        before = before + pc[g * R:(g + 1) * R, LANES:]
    o_ref[0] = jnp.concatenate(tiles, axis=1)


def _dsa_s_attend_body(pt_ref, q_ref, kn_ref, vn_ref, mask_ref, maskn_ref, *rest, G, n_new):
    pages = rest[:G]
    o_ref, m_ref, l_ref, acc_ref = rest[G:]
    p = pl.program_id(1)
    W = N_HEADS * DH
    R = 8
    qs = q_ref[0]

    def update(s, v, m, l, acc, v_feature_major=False):
        m_new = jnp.maximum(m, jnp.max(s, axis=1, keepdims=True))
        alpha = jnp.exp(m - m_new)
        pr = jnp.exp(s - m_new)
        pv = _dot_nt(pr.astype(BF16), v) if v_feature_major else _dot(pr.astype(BF16), v)
        return m_new, alpha * l + jnp.sum(pr, axis=1, keepdims=True), alpha * acc + pv

    @pl.when(p == 0)
    def _():
        t = lax.broadcasted_iota(I32, (R, PAGE), 0)
        sidx = lax.broadcasted_iota(I32, (R, PAGE), 1)
        sel = jnp.where((sidx <= t) & (sidx < n_new), maskn_ref[0, :, 0:PAGE], 0.0)
        s = jnp.where(jnp.concatenate([sel] * N_HEADS, axis=0) > 0.5, _dot_nt(qs, kn_ref[0]), NEG_BIG)
        m, l, acc = update(s, vn_ref[0], jnp.full((N_HEADS * R, 1), NEG_BIG, F32),
                           jnp.zeros((N_HEADS * R, 1), F32), jnp.zeros((N_HEADS * R, W), F32))
        m_ref[...] = jnp.broadcast_to(m, m_ref.shape)
        l_ref[...] = jnp.broadcast_to(l, l_ref.shape)
        acc_ref[...] = acc

    kcat = jnp.concatenate([pages[g][:W, :] for g in range(G)], axis=1).astype(BF16)
    vcat = jnp.concatenate([pages[g][W:, :] for g in range(G)], axis=1).astype(BF16)
    sel = jnp.concatenate([mask_ref[0]] * N_HEADS, axis=0)
    s = jnp.where(sel > 0.5, _dot(qs, kcat), NEG_BIG)
    m, l, acc = update(s, vcat, m_ref[:, 0:1], l_ref[:, 0:1], acc_ref[...], v_feature_major=True)
    m_ref[...] = jnp.broadcast_to(m, m_ref.shape)
    l_ref[...] = jnp.broadcast_to(l, l_ref.shape)
    acc_ref[...] = acc

    @pl.when(p == pl.num_programs(1) - 1)
    def _():
        o_ref[0] = _unstack_heads(acc_ref[...] / l_ref[:, 0:1], R).astype(o_ref.dtype)


def _dsa_sample(qi32, w32, kin, qc32, kcn, vcn, cache_kidx, cache_kv, layer, page_table, *, G, n_new, n_sel):
    Bs = qi32.shape[0]
    NP = page_table.shape[1]
    W = N_HEADS * DH
    nstep = NP // G
    width = (nstep + 1) * G * PAGE

    def kidx_spec(g):
        return pl.BlockSpec((None, None, DH, PAGE),
                            lambda b, p, pt: (layer, pt[b, jnp.minimum(p * G + g, NP - 1)], 0, 0))

    scores = pl.pallas_call(
        functools.partial(_dsa_s_score_body, G=G, n_new=n_new),
        grid_spec=pltpu.PrefetchScalarGridSpec(
            num_scalar_prefetch=1, grid=(Bs, nstep + 1),
            in_specs=[pl.BlockSpec((1, 32, DH), lambda b, p, pt: (b, 0, 0)),
                      pl.BlockSpec((1, 32, LANES), lambda b, p, pt: (b, 0, 0)),
                      pl.BlockSpec((1, PAGE, DH), lambda b, p, pt: (b, 0, 0))] + [kidx_spec(g) for g in range(G)],
            out_specs=pl.BlockSpec((1, 8, G * PAGE), lambda b, p, pt: (b, 0, p))),
        out_shape=jax.ShapeDtypeStruct((Bs, 8, width), F32),
        compiler_params=_cparams(("parallel", "arbitrary")),
        name="dsa_sample_score",
    )(page_table, qi32, w32, kin, *([cache_kidx] * G))

    mask = pl.pallas_call(
        functools.partial(_dsa_s_select_body, n_sel=n_sel),
        grid=(Bs,),
        in_specs=[pl.BlockSpec((1, 8, width), lambda b: (b, 0, 0))],
        out_specs=pl.BlockSpec((1, 8, width), lambda b: (b, 0, 0)),
        out_shape=jax.ShapeDtypeStruct((Bs, 8, width), F32),
        compiler_params=_cparams(("parallel",)),
        name="dsa_sample_select",
    )(scores)

    def kv_spec(g):
        return pl.BlockSpec((None, None, 2 * W, PAGE), lambda b, p, pt: (layer, pt[b, p * G + g], 0, 0))

    return pl.pallas_call(
        functools.partial(_dsa_s_attend_body, G=G, n_new=n_new),
        grid_spec=pltpu.PrefetchScalarGridSpec(
            num_scalar_prefetch=1, grid=(Bs, nstep),
            in_specs=[pl.BlockSpec((1, 32, W), lambda b, p, pt: (b, 0, 0)),
                      pl.BlockSpec((1, PAGE, W), lambda b, p, pt: (b, 0, 0)),
                      pl.BlockSpec((1, PAGE, W), lambda b, p, pt: (b, 0, 0)),
                      pl.BlockSpec((1, 8, G * PAGE), lambda b, p, pt: (b, 0, p)),
                      pl.BlockSpec((1, 8, G * PAGE), lambda b, p, pt: (b, 0, nstep))] + [kv_spec(g) for g in range(G)],
            out_specs=pl.BlockSpec((1, 8, W), lambda b, p, pt: (b, 0, 0)),
            scratch_shapes=[pltpu.VMEM((32, LANES), F32), pltpu.VMEM((32, LANES), F32), pltpu.VMEM((32, W), F32)]),
        out_shape=jax.ShapeDtypeStruct((Bs, 8, W), BF16),
        compiler_params=_cparams(("parallel", "arbitrary")),
        name="dsa_sample_attend",
    )(page_table, qc32, kcn, vcn, mask, mask, *([cache_kv] * G))


def _merge_body(oa_ref, ob_ref, oc_ref, g0_ref, g1_ref, g2_ref, x_ref, wa_ref, wb_ref, wc_ref, wo_ref, o_ref):
    merged = (jax.nn.sigmoid(g0_ref[...]) * _dot(oa_ref[...], wa_ref[...])
              + jax.nn.sigmoid(g1_ref[...]) * _dot(ob_ref[...], wb_ref[...])
              + jax.nn.sigmoid(g2_ref[...]) * _dot(oc_ref[...], wc_ref[...]))
    o_ref[...] = x_ref[...] + _dot(merged.astype(BF16), wo_ref[...])


def _merge(oa, ob, oc, pf, x, wa, wb, wc, wo, *, tm, gate_blk):
    T, D = x.shape
    row = lambda i: (i, 0)
    const = lambda i: (0, 0)
    return pl.pallas_call(
        _merge_body,
        grid=(T // tm,),
        in_specs=[pl.BlockSpec((tm, oa.shape[1]), row), pl.BlockSpec((tm, ob.shape[1]), row),
                  pl.BlockSpec((tm, oc.shape[1]), row),
                  pl.BlockSpec((tm, D), lambda i: (i, gate_blk)), pl.BlockSpec((tm, D), lambda i: (i, gate_blk + 1)),
                  pl.BlockSpec((tm, D), lambda i: (i, gate_blk + 2)), pl.BlockSpec((tm, D), row),
                  pl.BlockSpec(wa.shape, const), pl.BlockSpec(wb.shape, const),
                  pl.BlockSpec(wc.shape, const), pl.BlockSpec(wo.shape, const)],
        out_specs=pl.BlockSpec((tm, D), row),
        out_shape=jax.ShapeDtypeStruct((T, D), F32),
        compiler_params=_cparams(("parallel",)),
        name="merge",
    )(oa, ob, oc, pf, pf, pf, x, wa, wb, wc, wo)


def _ffn_down_body(*refs, tm, seq_tiles, expanded, final_norm):
    ua_ref, ub_ref, ha_ref, hb_ref = refs[:4]
    rest = refs[4:]
    if expanded:
        ha2_ref, hb2_ref, t_ref = rest[:3]
        rest = rest[3:]
    cwa_ref, cwb_ref, cba_ref, cbb_ref, wd_ref, x_ref = rest[:6]
    rest = rest[6:]
    if final_norm:
        gf_ref = rest[0]
        rest = rest[1:]
    o_ref, acc_ref = rest
    i = pl.program_id(0)
    k = pl.program_id(1)

    @pl.when(k == 0)
    def _():
        acc_ref[...] = jnp.zeros_like(acc_ref)

    def conv(u_ref, h_ref, h2_ref, cw_ref, cb_ref):
        u = u_ref[...]
        row = lax.broadcasted_iota(I32, u.shape, 0)
        r1 = pltpu.roll(u, 1, axis=0)
        r2 = pltpu.roll(u, 2, axis=0)
        if expanded:
            t = t_ref[...]
            u1 = jnp.where(t >= 1, r1, h_ref[...])
            u2 = jnp.where(t >= 2, r2, h2_ref[...])
        else:
            h = h_ref[...]
            h = jnp.where(i % seq_tiles == 0, jnp.zeros_like(h), h)
            u1 = jnp.where(row == 0, h[7:8], r1)
            u2 = jnp.where(row == 0, h[6:7], jnp.where(row == 1, h[7:8], r2))
        cw = cw_ref[...]
        return cb_ref[...] + cw[0:1] * u2 + cw[1:2] * u1 + cw[2:3] * u

    a = conv(ua_ref, ha_ref, ha2_ref if expanded else None, cwa_ref, cba_ref)
    b = conv(ub_ref, hb_ref, hb2_ref if expanded else None, cwb_ref, cbb_ref)
    acc_ref[...] += _dot((_silu(a) * b).astype(BF16), wd_ref[...])

    @pl.when(k == pl.num_programs(1) - 1)
    def _():
        y = x_ref[...] + acc_ref[...]
        if final_norm:
            ms = jnp.mean(y * y, axis=-1, keepdims=True)
            y = y * lax.rsqrt(ms + EPS) * gf_ref[...]
        o_ref[...] = y


def _ffn_down(u, x, cw, cb, wd, *, tm, tkf, seq_len, prev=None, g_final=None):
    T, D = x.shape
    F = wd.shape[0]
    nk = F // tkf
    expanded = prev is not None
    final_norm = g_final is not None
    seq_tiles = max(seq_len // tm, 1)
    hb8 = tm // 8
    ua = pl.BlockSpec((tm, tkf), lambda i, k: (i, k))
    ub = pl.BlockSpec((tm, tkf), lambda i, k: (i, nk + k))
    in_specs = [ua, ub]
    args = [u, u]
    if expanded:
        p1, p2, tpos = prev
        in_specs += [ua, ub, ua, ub, pl.BlockSpec((tm, 1), lambda i, k: (i, 0))]
        args += [p1, p1, p2, p2, tpos]
    else:
        in_specs += [pl.BlockSpec((8, tkf), lambda i, k: (jnp.maximum(i * hb8 - 1, 0), k)),
                     pl.BlockSpec((8, tkf), lambda i, k: (jnp.maximum(i * hb8 - 1, 0), nk + k))]
        args += [u, u]
    in_specs += [pl.BlockSpec((3, tkf), lambda i, k: (0, k)), pl.BlockSpec((3, tkf), lambda i, k: (0, nk + k)),
                 pl.BlockSpec((1, tkf), lambda i, k: (0, k)), pl.BlockSpec((1, tkf), lambda i, k: (0, nk + k)),
                 pl.BlockSpec((tkf, D), lambda i, k: (k, 0)), pl.BlockSpec((tm, D), lambda i, k: (i, 0))]
    args += [cw, cw, cb.reshape(1, -1), cb.reshape(1, -1), wd, x]
    if final_norm:
        in_specs.append(pl.BlockSpec((1, D), lambda i, k: (0, 0)))
        args.append(g_final.reshape(1, D))
    return pl.pallas_call(
        functools.partial(_ffn_down_body, tm=tm, seq_tiles=seq_tiles, expanded=expanded, final_norm=final_norm),
        grid=(T // tm, nk),
        in_specs=in_specs,
        out_specs=pl.BlockSpec((tm, D), lambda i, k: (i, 0)),
        out_shape=jax.ShapeDtypeStruct((T, D), F32),
        scratch_shapes=[pltpu.VMEM((tm, D), F32)],
        compiler_params=_cparams(("parallel", "arbitrary")),
        name="ffn_down",
    )(*args)


def _rotate_half_cols(w):
    D, N = w.shape
    w4 = w.reshape(D, N // DH, 2, DH // 2)
    return jnp.concatenate([-w4[:, :, 1], w4[:, :, 0]], axis=-1).reshape(D, N)


def _rope_tables(pos, width):
    half = DH // 2
    inv_freq = ROPE_THETA ** (-jnp.arange(half, dtype=F32) / half)
    ang = pos.astype(F32)[:, None] * inv_freq[None, :]
    reps = width // half
    return jnp.tile(jnp.cos(ang), (1, reps)), jnp.tile(jnp.sin(ang), (1, reps))


def _pick(n, prefs):
    for t in prefs:
        if n % t == 0:
            return t
    return n


def kernel(x_prompt, x_sample, cache_kv_a, cache_kv_c, cache_kidx_c, state_hgrn, state_ffn_conv, page_table,
           norm_mix, w_in, hgrn_lb_logits, hgrn_gnorm, w_br_a, w_br_b, w_br_c, w_out, norm_ffn, w_up, conv_w,
           conv_b, w_down, norm_final):
    B, L, D = x_prompt.shape
    Bs, Ls, _ = x_sample.shape
    depth = w_in.shape[0]
    NP = page_table.shape[1]
    past = NP * PAGE
    F = w_down.shape[1]
    WA = N_HEADS * DH
    WB = N_HEADS * DK
    Tp, Ts = B * L, Bs * Ls
    n_pool = cache_kv_a.shape[1]

    lb_soft = jax.nn.softmax(hgrn_lb_logits.astype(F32), axis=0)
    lb_all = jnp.cumsum(lb_soft, axis=0) - lb_soft[0]

    sizes = (WA, WA, WA, WB, WB, WB, WB, WA, WA, WA, N_HEADS * DH, DH, N_HEADS, 3 * D)
    offs = [0]
    for s in sizes:
        offs.append(offs[-1] + s)
    (o_qa, o_ka, o_va, o_qh, o_fh, o_ih, o_gh, o_qc, o_kc, o_vc, o_qi, o_ki, o_wi, o_gt, o_end) = offs
    QA_BLK, KA_BLK, VA_BLK, VC_BLK = (4 * WB) // WA, (4 * WB) // WA + 1, (4 * WB) // WA + 2, (4 * WB) // WA + 3
    gate_off = 4 * WB + 4 * WA
    assert gate_off % D == 0
    GATE_BLK = gate_off // D
    NPJ = gate_off + 3 * D
    N_ROPE_TILES = 4
    NRJ = (N_ROPE_TILES + 1) * WA
    wi_off = N_ROPE_TILES * WA
    WI_BLK = wi_off // LANES

    cache_a = cache_kv_a.transpose(0, 1, 3, 4, 5, 2).reshape(depth, n_pool, 2 * WA, PAGE)
    cache_c = cache_kv_c.transpose(0, 1, 3, 4, 5, 2).reshape(depth, n_pool, 2 * WA, PAGE)
    cache_i = cache_kidx_c.transpose(0, 1, 3, 2)

    cos_p, sin_p = _rope_tables(jnp.arange(L, dtype=I32), WA)
    cos_s, sin_s = _rope_tables(past + (jnp.arange(Ts, dtype=I32) % Ls), WA)

    tm_p = _pick(Tp, (1024, 512, 256, 128))
    tm_p = min(tm_p, L)
    tn_main = _pick(NPJ, (512, 256, 128))
    tn_up = _pick(2 * F, (512, 256, 128))
    tkf = _pick(F, (1408, 256, 128))
    tq_a = _pick(L, (256, 128))
    tq_c = _pick(L, (128,))
    tk_c = _pick(L, (512, 256, 128))
    C_h = _pick(L, (64,))
    G = _pick(NP, (8, 4, 2, 1))
    n_sel_p = min(TOPK_MAX, L // 4)
    n_sel_s = min(TOPK_MAX, (past + Ls) // 4)

    xp = x_prompt.reshape(Tp, D)
    xs = x_sample.reshape(Ts, D)
    tpos_s = (jnp.arange(Ts, dtype=I32) % Ls).reshape(Ts, 1)

    def pad_rows(a, rows):
        return jnp.pad(a, ((0, 0), (0, rows - a.shape[1]), (0, 0)))

    outs_p = ([], [], [], [], [])
    outs_s = ([], [], [], [], [])
    for l in range(depth):
        w = w_in[l]
        col = lambda o, n: w[:, o:o + n]
        w_main = jnp.concatenate(
            [col(o_qh, 4 * WB), col(o_qa, 3 * WA), col(o_vc, WA), col(o_gt, 3 * D)], axis=1).astype(BF16)
        w_r = jnp.concatenate([col(o_qc, WA), col(o_kc, WA), col(o_qi, WA)] + [col(o_ki, DH)] * N_HEADS, axis=1)
        w_tail = jnp.concatenate([col(o_wi, N_HEADS), jnp.zeros((D, WA - N_HEADS), w.dtype)], axis=1)
        w_rot = jnp.concatenate([_rotate_half_cols(w_r), jnp.zeros((D, WA), w.dtype)], axis=1).astype(BF16)
        w_r = jnp.concatenate([w_r, w_tail], axis=1).astype(BF16)
        wa_b, wb_b, wc_b, wo_b = (t[l].astype(BF16) for t in (w_br_a, w_br_b, w_br_c, w_out))
        wup_b = w_up[l].astype(BF16)
        wd_b = w_down[l].astype(BF16)
        last = l == depth - 1

        vc_col = VC_BLK * WA
        pf, pb, vT = _rms_proj(xp, norm_mix[l], w_main, tm=tm_p, tn=tn_main,
                               t_outs=((vc_col // tn_main, vc_col % tn_main, WA, WA, tk_c, BF16),))
        rf, rb, qcT, qiT, wiT = _rms_proj(
            xp, norm_mix[l], w_r, tm=tm_p, tn=WA, rope_args=(w_rot, cos_p, sin_p), n_rope_tiles=N_ROPE_TILES,
            t_outs=((0, 0, WA, WA, tq_c, BF16), (2, 0, WA, WA, tq_c, BF16), (N_ROPE_TILES, 0, LANES, 8, tq_c, F32)))
        oa = _attn_a_prompt(pb, B, L, tq=tq_a, q_blk=QA_BLK, k_blk=KA_BLK, v_blk=VA_BLK)
        ob, s_new = _hgrn(pf.reshape(B, L, NPJ), lb_all[l], hgrn_gnorm[l], None, C=C_h, c=min(16, C_h), valid=C_h)
        oc = _dsa_prompt(qcT, qiT, wiT, rb, vT, B, L, tq=tq_c, tk=tk_c, n_sel=n_sel_p)
        x1 = _merge(oa, ob.reshape(Tp, WB), oc, pf, xp, wa_b, wb_b, wc_b, wo_b, tm=min(512, tm_p), gate_blk=GATE_BLK)
        u = _rms_proj(x1, norm_ffn[l], wup_b, tm=tm_p, tn=tn_up, emit_bf16=False)
        xp = _ffn_down(u, x1, conv_w[l], conv_b[l], wd_b, tm=min(512, tm_p), tkf=tkf, seq_len=L,
                       g_final=norm_final if last else None)
        kv_off = (KA_BLK * WA)
        outs_p[0].append(pf[:, kv_off:kv_off + 2 * WA].reshape(B, L, 2, N_HEADS, DH))
        outs_p[1].append(jnp.stack([rf[:, WA:2 * WA].reshape(B, L, N_HEADS, DH),
                                    pf[:, VC_BLK * WA:(VC_BLK + 1) * WA].reshape(B, L, N_HEADS, DH)], axis=2))
        outs_p[2].append(rf[:, 3 * WA:3 * WA + DH].reshape(B, L, DH))
        outs_p[3].append(s_new)
        outs_p[4].append(u.reshape(B, L, 2 * F)[:, L - 2:])

        pf, pb = _rms_proj(xs, norm_mix[l], w_main, tm=Ts, tn=tn_main)
        rf, rb = _rms_proj(xs, norm_mix[l], w_r, tm=Ts, tn=WA, rope_args=(w_rot, cos_s, sin_s),
                           n_rope_tiles=N_ROPE_TILES)
        pb3 = pb.reshape(Bs, Ls, NPJ)
        rb3 = rb.reshape(Bs, Ls, NRJ)
        blk = lambda a, k: a[:, :, k * WA:(k + 1) * WA]
        oa8 = _attn_a_sample(pad_rows(blk(pb3, QA_BLK), 8), pad_rows(blk(pb3, KA_BLK), PAGE),
                             pad_rows(blk(pb3, VA_BLK), PAGE), cache_a, l, page_table, G=G)
        ph = jnp.pad(pf[:, :4 * WB].reshape(Bs, Ls, 4 * WB), ((0, 0), (0, 8 - Ls), (0, 0)))
        ob8, s_new = _hgrn(ph, lb_all[l], hgrn_gnorm[l], state_hgrn[l], C=8, c=8, valid=Ls)
        qi = blk(rb3, 2).reshape(Bs, Ls, N_HEADS, DH).transpose(0, 2, 1, 3)
        qi32 = jnp.pad(qi, ((0, 0), (0, 0), (0, 8 - Ls), (0, 0))).reshape(Bs, 4 * 8, DH)
        wi = rf[:, wi_off:wi_off + N_HEADS].reshape(Bs, Ls, N_HEADS).transpose(0, 2, 1) * (N_HEADS ** -0.5)
        w32 = jnp.broadcast_to(jnp.pad(wi, ((0, 0), (0, 0), (0, 8 - Ls))).reshape(Bs, 32, 1), (Bs, 32, LANES))
        kin = pad_rows(rb3[:, :, 3 * WA:3 * WA + DH], PAGE)
        qc8 = pad_rows(blk(rb3, 0), 8)
        head = (jnp.arange(WA) // DH)[None, None, None, :] == jnp.arange(N_HEADS)[None, :, None, None]
        qc32 = (jnp.where(head, qc8[:, None], 0) * jnp.asarray(DH ** -0.5, BF16)).reshape(Bs, 32, WA).astype(BF16)
        oc8 = _dsa_sample(qi32, w32, kin, qc32, pad_rows(blk(rb3, 1), PAGE), pad_rows(blk(pb3, VC_BLK), PAGE),
                          cache_i, cache_c, l, page_table, G=G, n_new=Ls, n_sel=n_sel_s)
        x1 = _merge(oa8[:, :Ls].reshape(Ts, WA), ob8[:, :Ls].reshape(Ts, WB), oc8[:, :Ls].reshape(Ts, WA), pf, xs,
                    wa_b, wb_b, wc_b, wo_b, tm=Ts, gate_blk=GATE_BLK)
        u = _rms_proj(x1, norm_ffn[l], wup_b, tm=Ts, tn=tn_up, emit_bf16=False)
        prev = state_ffn_conv[l]
        u3 = u.reshape(Bs, Ls, 2 * F)
        p1 = jnp.broadcast_to(prev[:, 1:2], (Bs, Ls, 2 * F)).reshape(Ts, 2 * F)
        p2 = jnp.concatenate([prev, jnp.zeros((Bs, Ls - 2, 2 * F), F32)], axis=1).reshape(Ts, 2 * F)
        xs = _ffn_down(u, x1, conv_w[l], conv_b[l], wd_b, tm=Ts, tkf=tkf, seq_len=Ls, prev=(p1, p2, tpos_s),
                       g_final=norm_final if last else None)
        kv_off = (KA_BLK * WA)
        outs_s[0].append(pf[:, kv_off:kv_off + 2 * WA].reshape(Bs, Ls, 2, N_HEADS, DH))
        outs_s[1].append(jnp.stack([rf[:, WA:2 * WA].reshape(Bs, Ls, N_HEADS, DH),
                                    pf[:, VC_BLK * WA:(VC_BLK + 1) * WA].reshape(Bs, Ls, N_HEADS, DH)], axis=2))
        outs_s[2].append(rf[:, 3 * WA:3 * WA + DH].reshape(Bs, Ls, DH))
        outs_s[3].append(s_new)
        outs_s[4].append(jnp.concatenate([prev, u3], axis=1)[:, Ls:])

    st = lambda xs_: jnp.stack(xs_, axis=0)
    return (xp.reshape(B, L, D), xs.reshape(Bs, Ls, D),
            st(outs_p[0]), st(outs_s[0]), st(outs_p[1]), st(outs_s[1]), st(outs_p[2]), st(outs_s[2]),
            st(outs_p[3]), st(outs_s[3]), st(outs_p[4]), st(outs_s[4]))
```

```python
import functools

import jax
import jax.numpy as jnp
from jax import lax
from jax.experimental import pallas as pl
from jax.experimental.pallas import tpu as pltpu

F32 = jnp.float32
BF16 = jnp.bfloat16
I32 = jnp.int32

EPS = 1e-6
NEG_BIG = -1e30
LOG_F_MIN = -30.0
ROPE_THETA = 10000.0
TOPK_MAX = 256
PAGE = 128
N_HEADS = 4
DH = 64
DK = 128
LANES = 128
VMEM_LIMIT = 56 * 1024 * 1024
INT_MIN = -(2 ** 31)


def _cparams(sem):
    return pltpu.CompilerParams(dimension_semantics=sem, vmem_limit_bytes=VMEM_LIMIT)


def _dot(a, b):
    return jnp.dot(a, b, preferred_element_type=F32)


def _dot_nt(a, b):
    return lax.dot_general(a, b, (((1,), (1,)), ((), ())), preferred_element_type=F32)


def _dot_tn(a, b):
    return lax.dot_general(a, b, (((0,), (0,)), ((), ())), preferred_element_type=F32)


def _silu(x):
    return x * jax.nn.sigmoid(x)


def _stack_heads(q, rows):
    head = lax.broadcasted_iota(I32, (rows, N_HEADS * DH), 1) // DH
    return jnp.concatenate([jnp.where(head == h, q, jnp.zeros_like(q)) for h in range(N_HEADS)], axis=0)


def _unstack_heads(acc, rows):
    head = lax.broadcasted_iota(I32, (rows, N_HEADS * DH), 1) // DH
    out = jnp.zeros((rows, N_HEADS * DH), acc.dtype)
    for h in range(N_HEADS):
        out = jnp.where(head == h, acc[h * rows:(h + 1) * rows], out)
    return out


def _proj_body(*refs, rope, emit_bf16, n_rope_tiles, t_outs):
    x_ref, g_ref, w_ref = refs[:3]
    rest = refs[3:]
    if rope:
        wrot_ref, cos_ref, sin_ref = rest[:3]
        rest = rest[3:]
    of_ref = rest[0]
    ob_ref = rest[1] if emit_bf16 else None
    t_refs = rest[(2 if emit_bf16 else 1):-1]
    h_ref = rest[-1]

    @pl.when(pl.program_id(1) == 0)
    def _():
        x = x_ref[...]
        ms = jnp.mean(x * x, axis=-1, keepdims=True)
        h_ref[...] = (x * lax.rsqrt(ms + EPS) * g_ref[...]).astype(BF16)

    h = h_ref[...]
    acc = _dot(h, w_ref[...])
    if rope:
        roped = acc * cos_ref[...] + _dot(h, wrot_ref[...]) * sin_ref[...]
        acc = jnp.where(pl.program_id(1) < n_rope_tiles, roped, acc)
    of_ref[...] = acc
    if emit_bf16:
        ob_ref[...] = acc.astype(BF16)
    for t_ref, (j_tile, c0, width, keep, sub, dtype) in zip(t_refs, t_outs):
        @pl.when(pl.program_id(1) == j_tile)
        def _(t_ref=t_ref, c0=c0, width=width, keep=keep, sub=sub, dtype=dtype):
            for r in range(acc.shape[0] // sub):
                t_ref[r] = acc[r * sub:(r + 1) * sub, c0:c0 + width].T[:keep].astype(dtype)


def _rms_proj(x, g, w, *, tm, tn, rope_args=None, emit_bf16=True, n_rope_tiles=0, t_outs=()):
    T, D = x.shape
    N = w.shape[1]
    rope = rope_args is not None
    in_specs = [pl.BlockSpec((tm, D), lambda i, j: (i, 0)),
                pl.BlockSpec((1, D), lambda i, j: (0, 0)),
                pl.BlockSpec((D, tn), lambda i, j: (0, j))]
    args = [x, g.reshape(1, D), w]
    if rope:
        w_rot, cos, sin = rope_args
        nblk = cos.shape[0] // tm
        in_specs += [pl.BlockSpec((D, tn), lambda i, j: (0, j)),
                     pl.BlockSpec((tm, tn), lambda i, j: (i % nblk, 0)),
                     pl.BlockSpec((tm, tn), lambda i, j: (i % nblk, 0))]
        args += [w_rot, cos, sin]
    out_shape = [jax.ShapeDtypeStruct((T, N), F32)]
    out_specs = [pl.BlockSpec((tm, tn), lambda i, j: (i, j))]
    if emit_bf16:
        out_shape.append(jax.ShapeDtypeStruct((T, N), BF16))
        out_specs.append(pl.BlockSpec((tm, tn), lambda i, j: (i, j)))
    for (_, _, _, keep, sub, dtype) in t_outs:
        out_shape.append(jax.ShapeDtypeStruct((T // sub, keep, sub), dtype))
        out_specs.append(pl.BlockSpec((tm // sub, keep, sub), lambda i, j: (i, 0, 0)))
    outs = pl.pallas_call(
        functools.partial(_proj_body, rope=rope, emit_bf16=emit_bf16, n_rope_tiles=n_rope_tiles, t_outs=tuple(t_outs)),
        grid=(T // tm, N // tn),
        in_specs=in_specs, out_specs=out_specs, out_shape=out_shape,
        scratch_shapes=[pltpu.VMEM((tm, D), BF16)],
        compiler_params=_cparams(("parallel", "arbitrary")),
        name="rms_proj_rope" if rope else "rms_proj",
    )(*args)
    return outs if (emit_bf16 or t_outs) else outs[0]


def _suffix_matrix():
    j = lax.broadcasted_iota(I32, (2 * LANES, LANES), 0) & (LANES - 1)
    s = lax.broadcasted_iota(I32, (2 * LANES, LANES), 1)
    return jnp.where(j > s, 1.0, 0.0).astype(BF16)


def _log_one_minus_beta(z):
    return -(jnp.maximum(z, 0.0) + jnp.log(1.0 + jnp.exp(-jnp.abs(z))))


def _sb_update(z, carry, suffix, vis):
    M, tk = z.shape
    ls = _log_one_minus_beta(z)
    if vis is not None:
        ls = jnp.where(vis, ls, 0.0)
    hi = ls.astype(BF16)
    lo = (ls - hi.astype(F32)).astype(BF16)
    lz = z + ls
    n = tk // LANES
    outs = [None] * n
    for g in reversed(range(n)):
        sl = slice(g * LANES, (g + 1) * LANES)
        between = _dot(jnp.concatenate([hi[:, sl], lo[:, sl]], axis=1), suffix)
        e = lz[:, sl] + between + carry
        if vis is not None:
            e = jnp.where(vis[:, sl], e, NEG_BIG)
        outs[g] = jnp.exp(e)
        carry = carry + jnp.sum(ls[:, sl], axis=1, keepdims=True)
    a = outs[0] if n == 1 else jnp.concatenate(outs, axis=1)
    return a.astype(BF16), carry


def _attn_a_prompt_body(q_ref, k_ref, v_ref, o_ref, *, tq):
    i = pl.program_id(1)
    M = N_HEADS * tq
    qs = _stack_heads(q_ref[...], tq) * jnp.asarray(DH ** -0.5, BF16)
    suffix = _suffix_matrix()

    def block(k0, carry, acc, vis):
        kb = k_ref[pl.ds(k0, tq), :]
        vb = v_ref[pl.ds(k0, tq), :]
        a, carry = _sb_update(_dot_nt(qs, kb), carry, suffix, vis)
        return carry, acc + _dot(a, vb)

    row_t = lax.broadcasted_iota(I32, (M, tq), 0) & (tq - 1)
    col_s = lax.broadcasted_iota(I32, (M, tq), 1)
    carry, acc = block(pl.multiple_of(i * tq, tq), jnp.zeros((M, 1), F32),
                       jnp.zeros((M, N_HEADS * DH), F32), col_s < row_t)

    def body(step, c):
        k0 = pl.multiple_of((i - 1 - step) * tq, tq)
        return block(k0, c[0], c[1], None)

    carry, acc = lax.fori_loop(0, i, body, (carry, acc))
    o_ref[...] = _unstack_heads(acc, tq).astype(o_ref.dtype)


def _attn_a_prompt(pb, B, L, *, tq, q_blk, k_blk, v_blk):
    W = N_HEADS * DH
    nq = L // tq
    return pl.pallas_call(
        functools.partial(_attn_a_prompt_body, tq=tq),
        grid=(B, nq),
        in_specs=[pl.BlockSpec((tq, W), lambda b, i: (b * nq + i, q_blk)),
                  pl.BlockSpec((L, W), lambda b, i: (b, k_blk)),
                  pl.BlockSpec((L, W), lambda b, i: (b, v_blk))],
        out_specs=pl.BlockSpec((tq, W), lambda b, i: (b * nq + i, 0)),
        out_shape=jax.ShapeDtypeStruct((B * L, W), BF16),
        compiler_params=_cparams(("parallel", "arbitrary")),
        name="attn_a_prompt",
    )(pb, pb, pb)


def _attn_a_sample_body(pt_ref, q_ref, kn_ref, vn_ref, *rest, G):
    pages = rest[:G]
    o_ref, carry_ref, acc_ref = rest[G:]
    p = pl.program_id(1)
    W = N_HEADS * DH
    R = 8
    M = N_HEADS * R
    qs = _stack_heads(q_ref[0], R) * jnp.asarray(DH ** -0.5, BF16)
    suffix = _suffix_matrix()

    @pl.when(p == 0)
    def _():
        row_t = lax.broadcasted_iota(I32, (M, PAGE), 0) & (R - 1)
        col_s = lax.broadcasted_iota(I32, (M, PAGE), 1)
        a, carry = _sb_update(_dot_nt(qs, kn_ref[0]), jnp.zeros((M, 1), F32), suffix, col_s < row_t)
        carry_ref[...] = jnp.broadcast_to(carry, carry_ref.shape)
        acc_ref[...] = _dot(a, vn_ref[0])

    kcat = jnp.concatenate([pages[g][:W, :] for g in reversed(range(G))], axis=1).astype(BF16)
    vcat = jnp.concatenate([pages[g][W:, :] for g in reversed(range(G))], axis=1).astype(BF16)
    a, carry = _sb_update(_dot(qs, kcat), carry_ref[:, 0:1], suffix, None)
    carry_ref[...] = jnp.broadcast_to(carry, carry_ref.shape)
    acc_ref[...] += _dot_nt(a, vcat)

    @pl.when(p == pl.num_programs(1) - 1)
    def _():
        o_ref[0] = _unstack_heads(acc_ref[...], R).astype(o_ref.dtype)


def _attn_a_sample(q8, kn, vn, cache, layer, page_table, *, G):
    Bs = q8.shape[0]
    NP = page_table.shape[1]
    W = N_HEADS * DH

    def page_spec(g):
        return pl.BlockSpec((None, None, 2 * W, PAGE),
                            lambda b, p, pt: (layer, pt[b, NP - 1 - (p * G + g)], 0, 0))

    grid_spec = pltpu.PrefetchScalarGridSpec(
        num_scalar_prefetch=1,
        grid=(Bs, NP // G),
        in_specs=[pl.BlockSpec((1, 8, W), lambda b, p, pt: (b, 0, 0)),
                  pl.BlockSpec((1, PAGE, W), lambda b, p, pt: (b, 0, 0)),
                  pl.BlockSpec((1, PAGE, W), lambda b, p, pt: (b, 0, 0))] + [page_spec(g) for g in range(G)],
        out_specs=pl.BlockSpec((1, 8, W), lambda b, p, pt: (b, 0, 0)),
        scratch_shapes=[pltpu.VMEM((N_HEADS * 8, LANES), F32), pltpu.VMEM((N_HEADS * 8, W), F32)],
    )
    return pl.pallas_call(
        functools.partial(_attn_a_sample_body, G=G),
        grid_spec=grid_spec,
        out_shape=jax.ShapeDtypeStruct((Bs, 8, W), BF16),
        compiler_params=_cparams(("parallel", "arbitrary")),
        name="attn_a_sample",
    )(page_table, q8, kn, vn, *([cache] * G))


def _cumsum_rows(x):
    C = x.shape[0]
    row = lax.broadcasted_iota(I32, x.shape, 0)
    sh = 1
    while sh < C:
        x = x + jnp.where(row >= sh, pltpu.roll(x, sh, axis=0), 0.0)
        sh *= 2
    return x


def _hgrn_body(*refs, C, c, valid, has_s0):
    q_ref, f_ref, i_ref, g_ref, lb_ref, gn_ref = refs[:6]
    rest = refs[6:]
    if has_s0:
        s0_ref = rest[0]
        rest = rest[1:]
    o_ref, sout_ref, st_ref = rest
    ci = pl.program_id(1)

    @pl.when(ci == 0)
    def _():
        for h in range(N_HEADS):
            st_ref[h] = s0_ref[0, h].T if has_s0 else jnp.zeros((DK, DK), F32)

    row = lax.broadcasted_iota(I32, (C, DK), 0)
    rowc = lax.broadcasted_iota(I32, (c, 1), 0)
    for h in range(N_HEADS):
        hs = slice(h * DK, (h + 1) * DK)
        kk = (1.0 - lb_ref[:, hs]) * jax.nn.sigmoid(-f_ref[0, :, hs])
        lg = jnp.maximum(jnp.log1p(-kk), LOG_F_MIN)
        if valid < C:
            kk = jnp.where(row < valid, kk, 0.0)
            lg = jnp.where(row < valid, lg, 0.0)
        qq = _silu(q_ref[0, :, hs])
        vv = i_ref[0, :, hs]
        cum = _cumsum_rows(lg)
        st = st_ref[h]
        o = _dot_nt((qq * jnp.exp(cum)).astype(BF16), st.astype(BF16))
        parts = []
        for blk in range(C // c):
            r0 = blk * c
            q_b = qq[r0:r0 + c]
            cum_b = cum[r0:r0 + c]
            o_b = jnp.zeros((c, DK), F32)
            if blk > 0:
                base = cum[r0 - 1:r0]
                qt = q_b * jnp.exp(cum_b - base)
                kt = kk[:r0] * jnp.exp(base - cum[:r0])
                sc = _dot_nt(qt.astype(BF16), kt.astype(BF16))
                o_b = o_b + _dot(sc.astype(BF16), vv[:r0].astype(BF16))
            for s in range(c):
                r = r0 + s
                d = jnp.minimum(cum_b - cum[r:r + 1], 0.0)
                w = jnp.sum(q_b * kk[r:r + 1] * jnp.exp(d), axis=1, keepdims=True)
                o_b = o_b + jnp.where(rowc >= s, w, 0.0) * vv[r:r + 1]
            parts.append(o_b)
        o = o + (parts[0] if len(parts) == 1 else jnp.concatenate(parts, axis=0))
        last = cum[C - 1:C]
        kd = kk * jnp.exp(last - cum)
        st_ref[h] = st * jnp.exp(last) + _dot_tn(vv.astype(BF16), kd.astype(BF16))
        ms = jnp.mean(o * o, axis=1, keepdims=True)
        y = o * lax.rsqrt(ms + EPS) * gn_ref[...] * _silu(g_ref[0, :, hs])
        o_ref[0, :, hs] = y.astype(o_ref.dtype)

    @pl.when(ci == pl.num_programs(1) - 1)
    def _():
        for h in range(N_HEADS):
            sout_ref[0, h] = st_ref[h].T


def _hgrn(p3, lb, gn, s0, *, C, c, valid, col0=0):
    B, L, _ = p3.shape
    W = N_HEADS * DK
    has_s0 = s0 is not None
    in_specs = [pl.BlockSpec((1, C, W), functools.partial(lambda b, ci, k: (b, ci, col0 + k), k=k)) for k in range(4)]
    in_specs += [pl.BlockSpec((1, W), lambda b, ci: (0, 0)), pl.BlockSpec((1, DK), lambda b, ci: (0, 0))]
    args = [p3, p3, p3, p3, lb.reshape(1, W), gn.reshape(1, DK)]
    if has_s0:
        in_specs.append(pl.BlockSpec((1, N_HEADS, DK, DK), lambda b, ci: (b, 0, 0, 0)))
        args.append(s0)
    return pl.pallas_call(
        functools.partial(_hgrn_body, C=C, c=c, valid=valid, has_s0=has_s0),
        grid=(B, L // C),
        in_specs=in_specs,
        out_specs=[pl.BlockSpec((1, C, W), lambda b, ci: (b, ci, 0)),
                   pl.BlockSpec((1, N_HEADS, DK, DK), lambda b, ci: (b, 0, 0, 0))],
        out_shape=[jax.ShapeDtypeStruct((B, L, W), BF16), jax.ShapeDtypeStruct((B, N_HEADS, DK, DK), F32)],
        scratch_shapes=[pltpu.VMEM((N_HEADS, DK, DK), F32)],
        compiler_params=_cparams(("parallel", "arbitrary")),
        name="hgrn2",
    )(*args)


def _sort_key(score):
    b = pltpu.bitcast(score, I32)
    return jnp.where(b < 0, b ^ jnp.int32(0x7FFFFFFF), b)


def _prefix_matrix(n):
    j = lax.broadcasted_iota(I32, (n, n + LANES), 0)
    s = lax.broadcasted_iota(I32, (n, n + LANES), 1)
    return jnp.where((s >= n) | (j <= s), 1.0, 0.0).astype(BF16)


def _kth_largest_key(count_ge, rows, n_sel):
    def body(t, T):
        cand = T + jnp.left_shift(jnp.int32(1), 31 - t)
        return jnp.where(count_ge(cand) >= n_sel, cand, T)
    return lax.fori_loop(0, 32, body, jnp.full((rows, 1), INT_MIN, I32))


def _dsa_prompt_body(qcT_ref, qiT_ref, wiT_ref, kc_ref, ki_ref, vT_ref, lt_ref, o_ref, keys_ref, kh_ref, kl_ref,
                     *, tq, tk, n_sel):
    i = pl.program_id(1)
    nkb = ((i + 1) * tq + tk - 1) // tk
    M = N_HEADS * tq
    W = N_HEADS * DH
    I16 = jnp.int16
    kpos0 = lax.broadcasted_iota(I32, (tk, tq), 0)
    qpos = i * tq + lax.broadcasted_iota(I32, (tk, tq), 1)
    row_head = lax.broadcasted_iota(I32, (W, tq), 0) // DH

    def stack_lanes(qT):
        return jnp.concatenate([jnp.where(row_head == h, qT, jnp.zeros_like(qT)) for h in range(N_HEADS)], axis=1)

    qisT = stack_lanes(qiT_ref[0])
    wi = wiT_ref[0]
    w_row = jnp.concatenate([wi[j:j + 1, :] for j in range(N_HEADS)], axis=1) * (N_HEADS ** -0.5) * (DH ** -0.5)

    def score_body(j, _):
        k0 = pl.multiple_of(j * tk, tk)
        d = jnp.maximum(_dot(ki_ref[pl.ds(k0, tk), :], qisT), 0.0) * w_row
        score = d[:, 0:tq]
        for jh in range(1, N_HEADS):
            score = score + d[:, jh * tq:(jh + 1) * tq]
        score = jnp.where(kpos0 + k0 <= qpos, score + 0.0, NEG_BIG)
        key = _sort_key(score)
        keys_ref[j] = key
        kh_ref[j] = (key >> 16).astype(I16)
        kl_ref[j] = ((key & 0xFFFF) - 32768).astype(I16)
        return 0

    lax.fori_loop(0, nkb, score_body, 0)

    def count16(ref, cand):
        c16 = cand.astype(I16)

        def body(j, acc):
            m = jnp.where(ref[j] >= c16, jnp.ones((), BF16), jnp.zeros((), BF16))
            parts = [m[16 * r:16 * (r + 1)] for r in range(tk // 16)]
            while len(parts) > 1:
                parts = [parts[a] + parts[a + 1] for a in range(0, len(parts) - 1, 2)] + parts[len(parts) & ~1:]
            return acc + parts[0].astype(F32)

        acc = lax.fori_loop(0, nkb, body, jnp.zeros((16, tq), F32))
        return jnp.sum(acc, axis=0, keepdims=True)

    def kth16(ref, target):
        def body(t, T):
            cand = T + jnp.left_shift(jnp.int32(1), 15 - t)
            return jnp.where(count16(ref, cand) >= target, cand, T)
        return lax.fori_loop(0, 16, body, jnp.full((1, tq), -32768, I32))

    def count_above(ref, T):
        return jnp.where(T >= 32767, 0.0, count16(ref, jnp.minimum(T + 1, 32767)))

    TH = kth16(kh_ref, float(n_sel))
    need = n_sel - count_above(kh_ref, TH)
    th16 = TH.astype(I16)

    def low_body(j, _):
        kl_ref[j] = jnp.where(kh_ref[j] == th16, kl_ref[j], jnp.full((), -32768, I16))
        return 0

    lax.fori_loop(0, nkb, low_body, 0)
    TL = kth16(kl_ref, need)
    room = need - count_above(kl_ref, TL)
    T = TH * 65536 + (TL + 32768)

    qcsT = stack_lanes(qcT_ref[0]) * jnp.asarray(DH ** -0.5, BF16)
    lt = lt_ref[...]

    def att_body(j, c):
        m, l, accT, eq_before = c
        k0 = pl.multiple_of(j * tk, tk)
        key = keys_ref[j]
        eq = key == T
        rank = _dot(lt, jnp.where(eq, 1.0, 0.0).astype(BF16)) + eq_before
        sel = (key > T) | (eq & (rank <= room))
        bias = jnp.where(sel & (kpos0 + k0 <= qpos), 0.0, NEG_BIG)
        s = _dot(kc_ref[pl.ds(k0, tk), :], qcsT) + jnp.concatenate([bias] * N_HEADS, axis=1)
        m_new = jnp.maximum(m, jnp.max(s, axis=0, keepdims=True))
        alpha = jnp.exp(m - m_new)
        p = jnp.exp(s - m_new)
        l = alpha * l + jnp.sum(p, axis=0, keepdims=True)
        accT = alpha * accT + _dot(vT_ref[j], p.astype(BF16))
        return m_new, l, accT, rank[tk - 1:tk, :]

    init = (jnp.full((1, M), NEG_BIG, F32), jnp.zeros((1, M), F32), jnp.zeros((W, M), F32), jnp.zeros((1, tq), F32))
    m, l, accT, _ = lax.fori_loop(0, nkb, att_body, init)
    outT = accT / l
    oT = jnp.zeros((W, tq), F32)
    for h in range(N_HEADS):
        oT = jnp.where(row_head == h, outT[:, h * tq:(h + 1) * tq], oT)
    o_ref[...] = oT.T.astype(o_ref.dtype)


def _dsa_prompt(qcT, qiT, wiT, rb, vT, B, L, *, tq, tk, n_sel):
    W = N_HEADS * DH
    nq = L // tq
    assert tk // 16 <= 256
    lt = jnp.where(jnp.arange(tk)[None, :] <= jnp.arange(tk)[:, None], 1.0, 0.0).astype(BF16)
    return pl.pallas_call(
        functools.partial(_dsa_prompt_body, tq=tq, tk=tk, n_sel=n_sel),
        grid=(B, nq),
        in_specs=[pl.BlockSpec((1, W, tq), lambda b, i: (b * nq + i, 0, 0)),
                  pl.BlockSpec((1, W, tq), lambda b, i: (b * nq + i, 0, 0)),
                  pl.BlockSpec((1, 8, tq), lambda b, i: (b * nq + i, 0, 0)),
                  pl.BlockSpec((L, W), lambda b, i: (b, 1)),
                  pl.BlockSpec((L, W), lambda b, i: (b, 3)),
                  pl.BlockSpec((L // tk, W, tk), lambda b, i: (b, 0, 0)),
                  pl.BlockSpec((tk, tk), lambda b, i: (0, 0))],
        out_specs=pl.BlockSpec((tq, W), lambda b, i: (b * nq + i, 0)),
        out_shape=jax.ShapeDtypeStruct((B * L, W), BF16),
        scratch_shapes=[pltpu.VMEM((L // tk, tk, tq), I32), pltpu.VMEM((L // tk, tk, tq), jnp.int16),
                        pltpu.VMEM((L // tk, tk, tq), jnp.int16)],
        compiler_params=_cparams(("parallel", "arbitrary")),
        name="dsa_prompt",
    )(qcT, qiT, wiT, rb, rb, vT, lt)


def _dsa_s_score_body(pt_ref, qi_ref, w_ref, kn_ref, *rest, G, n_new):
    pages = rest[:G]
    o_ref = rest[G]
    p = pl.program_id(1)
    R = 8
    qi = qi_ref[0]
    w = w_ref[0]

    def score(dots):
        d = jnp.maximum(dots * (DH ** -0.5), 0.0) * w
        return (d[0:R] + d[R:2 * R]) + (d[2 * R:3 * R] + d[3 * R:4 * R]) + 0.0

    @pl.when(p < pl.num_programs(1) - 1)
    def _():
        for g in range(G):
            o_ref[0, :, g * PAGE:(g + 1) * PAGE] = score(_dot(qi, pages[g][...].astype(BF16)))

    @pl.when(p == pl.num_programs(1) - 1)
    def _():
        t = lax.broadcasted_iota(I32, (R, PAGE), 0)
        s = lax.broadcasted_iota(I32, (R, PAGE), 1)
        o_ref[0, :, 0:PAGE] = jnp.where((s <= t) & (s < n_new), score(_dot_nt(qi, kn_ref[0])), NEG_BIG)
        if G > 1:
            o_ref[0, :, PAGE:] = jnp.full((R, (G - 1) * PAGE), NEG_BIG, F32)


def _dsa_s_select_body(s_ref, o_ref, *, n_sel):
    R = 8
    key = _sort_key(s_ref[0])
    W = key.shape[1]

    def count_ge(cand):
        return jnp.sum(jnp.where(key >= cand, 1.0, 0.0), axis=1, keepdims=True)

    T = _kth_largest_key(count_ge, R, n_sel)
    room = n_sel - count_ge(T + 1)
    eq = key == T
    gt = key > T
    nt = W // LANES
    eqf = jnp.where(eq, 1.0, 0.0)
    stacked = jnp.concatenate([eqf[:, g * LANES:(g + 1) * LANES] for g in range(nt)], axis=0).astype(BF16)
    pc = _dot(stacked, _prefix_matrix(LANES))
    before = jnp.zeros((R, LANES), F32)
    tiles = []
    for g in range(nt):
        sl = slice(g * LANES, (g + 1) * LANES)
        rank = pc[g * R:(g + 1) * R, :LANES] + before
        tiles.append(jnp.where(gt[:, sl], 1.0, jnp.where(eq[:, sl] & (rank <= room), 1.0, 0.0)))
        before = before + pc[g * R:(g + 1) * R, LANES:]
    o_ref[0] = jnp.concatenate(tiles, axis=1)


def _dsa_s_attend_body(pt_ref, q_ref, kn_ref, vn_ref, mask_ref, maskn_ref, *rest, G, n_new):
    pages = rest[:G]
    o_ref, m_ref, l_ref, acc_ref = rest[G:]
    p = pl.program_id(1)
    W = N_HEADS * DH
    R = 8
    qs = q_ref[0]

    def update(s, v, m, l, acc, v_feature_major=False):
        m_new = jnp.maximum(m, jnp.max(s, axis=1, keepdims=True))
        alpha = jnp.exp(m - m_new)
        pr = jnp.exp(s - m_new)
        pv = _dot_nt(pr.astype(BF16), v) if v_feature_major else _dot(pr.astype(BF16), v)
        return m_new, alpha * l + jnp.sum(pr, axis=1, keepdims=True), alpha * acc + pv

    @pl.when(p == 0)
    def _():
        t = lax.broadcasted_iota(I32, (R, PAGE), 0)
        sidx = lax.broadcasted_iota(I32, (R, PAGE), 1)
        sel = jnp.where((sidx <= t) & (sidx < n_new), maskn_ref[0, :, 0:PAGE], 0.0)
        s = jnp.where(jnp.concatenate([sel] * N_HEADS, axis=0) > 0.5, _dot_nt(qs, kn_ref[0]), NEG_BIG)
        m, l, acc = update(s, vn_ref[0], jnp.full((N_HEADS * R, 1), NEG_BIG, F32),
                           jnp.zeros((N_HEADS * R, 1), F32), jnp.zeros((N_HEADS * R, W), F32))
        m_ref[...] = jnp.broadcast_to(m, m_ref.shape)
        l_ref[...] = jnp.broadcast_to(l, l_ref.shape)
        acc_ref[...] = acc

    kcat = jnp.concatenate([pages[g][:W, :] for g in range(G)], axis=1).astype(BF16)
    vcat = jnp.concatenate([pages[g][W:, :] for g in range(G)], axis=1).astype(BF16)
    sel = jnp.concatenate([mask_ref[0]] * N_HEADS, axis=0)
    s = jnp.where(sel > 0.5, _dot(qs, kcat), NEG_BIG)
    m, l, acc = update(s, vcat, m_ref[:, 0:1], l_ref[:, 0:1], acc_ref[...], v_feature_major=True)
    m_ref[...] = jnp.broadcast_to(m, m_ref.shape)
    l_ref[...] = jnp.broadcast_to(l, l_ref.shape)
    acc_ref[...] = acc

    @pl.when(p == pl.num_programs(1) - 1)
    def _():
        o_ref[0] = _unstack_heads(acc_ref[...] / l_ref[:, 0:1], R).astype(o_ref.dtype)


def _dsa_sample(qi32, w32, kin, qc32, kcn, vcn, cache_kidx, cache_kv, layer, page_table, *, G, n_new, n_sel):
    Bs = qi32.shape[0]
    NP = page_table.shape[1]
    W = N_HEADS * DH
    nstep = NP // G
    width = (nstep + 1) * G * PAGE

    def kidx_spec(g):
        return pl.BlockSpec((None, None, DH, PAGE),
                            lambda b, p, pt: (layer, pt[b, jnp.minimum(p * G + g, NP - 1)], 0, 0))

    scores = pl.pallas_call(
        functools.partial(_dsa_s_score_body, G=G, n_new=n_new),
        grid_spec=pltpu.PrefetchScalarGridSpec(
            num_scalar_prefetch=1, grid=(Bs, nstep + 1),
            in_specs=[pl.BlockSpec((1, 32, DH), lambda b, p, pt: (b, 0, 0)),
                      pl.BlockSpec((1, 32, LANES), lambda b, p, pt: (b, 0, 0)),
                      pl.BlockSpec((1, PAGE, DH), lambda b, p, pt: (b, 0, 0))] + [kidx_spec(g) for g in range(G)],
            out_specs=pl.BlockSpec((1, 8, G * PAGE), lambda b, p, pt: (b, 0, p))),
        out_shape=jax.ShapeDtypeStruct((Bs, 8, width), F32),
        compiler_params=_cparams(("parallel", "arbitrary")),
        name="dsa_sample_score",
    )(page_table, qi32, w32, kin, *([cache_kidx] * G))

    mask = pl.pallas_call(
        functools.partial(_dsa_s_select_body, n_sel=n_sel),
        grid=(Bs,),
        in_specs=[pl.BlockSpec((1, 8, width), lambda b: (b, 0, 0))],
        out_specs=pl.BlockSpec((1, 8, width), lambda b: (b, 0, 0)),
        out_shape=jax.ShapeDtypeStruct((Bs, 8, width), F32),
        compiler_params=_cparams(("parallel",)),
        name="dsa_sample_select",
    )(scores)

    def kv_spec(g):
        return pl.BlockSpec((None, None, 2 * W, PAGE), lambda b, p, pt: (layer, pt[b, p * G + g], 0, 0))

    return pl.pallas_call(
        functools.partial(_dsa_s_attend_body, G=G, n_new=n_new),
        grid_spec=pltpu.PrefetchScalarGridSpec(
            num_scalar_prefetch=1, grid=(Bs, nstep),
            in_specs=[pl.BlockSpec((1, 32, W), lambda b, p, pt: (b, 0, 0)),
                      pl.BlockSpec((1, PAGE, W), lambda b, p, pt: (b, 0, 0)),
                      pl.BlockSpec((1, PAGE, W), lambda b, p, pt: (b, 0, 0)),
                      pl.BlockSpec((1, 8, G * PAGE), lambda b, p, pt: (b, 0, p)),
                      pl.BlockSpec((1, 8, G * PAGE), lambda b, p, pt: (b, 0, nstep))] + [kv_spec(g) for g in range(G)],
            out_specs=pl.BlockSpec((1, 8, W), lambda b, p, pt: (b, 0, 0)),
            scratch_shapes=[pltpu.VMEM((32, LANES), F32), pltpu.VMEM((32, LANES), F32), pltpu.VMEM((32, W), F32)]),
        out_shape=jax.ShapeDtypeStruct((Bs, 8, W), BF16),
        compiler_params=_cparams(("parallel", "arbitrary")),
        name="dsa_sample_attend",
    )(page_table, qc32, kcn, vcn, mask, mask, *([cache_kv] * G))


def _merge_body(oa_ref, ob_ref, oc_ref, g0_ref, g1_ref, g2_ref, x_ref, wa_ref, wb_ref, wc_ref, wo_ref, o_ref):
    merged = (jax.nn.sigmoid(g0_ref[...]) * _dot(oa_ref[...], wa_ref[...])
              + jax.nn.sigmoid(g1_ref[...]) * _dot(ob_ref[...], wb_ref[...])
              + jax.nn.sigmoid(g2_ref[...]) * _dot(oc_ref[...], wc_ref[...]))
    o_ref[...] = x_ref[...] + _dot(merged.astype(BF16), wo_ref[...])


def _merge(oa, ob, oc, pf, x, wa, wb, wc, wo, *, tm, gate_blk):
    T, D = x.shape
    row = lambda i: (i, 0)
    const = lambda i: (0, 0)
    return pl.pallas_call(
        _merge_body,
        grid=(T // tm,),
        in_specs=[pl.BlockSpec((tm, oa.shape[1]), row), pl.BlockSpec((tm, ob.shape[1]), row),
                  pl.BlockSpec((tm, oc.shape[1]), row),
                  pl.BlockSpec((tm, D), lambda i: (i, gate_blk)), pl.BlockSpec((tm, D), lambda i: (i, gate_blk + 1)),
                  pl.BlockSpec((tm, D), lambda i: (i, gate_blk + 2)), pl.BlockSpec((tm, D), row),
                  pl.BlockSpec(wa.shape, const), pl.BlockSpec(wb.shape, const),
                  pl.BlockSpec(wc.shape, const), pl.BlockSpec(wo.shape, const)],
        out_specs=pl.BlockSpec((tm, D), row),
        out_shape=jax.ShapeDtypeStruct((T, D), F32),
        compiler_params=_cparams(("parallel",)),
        name="merge",
    )(oa, ob, oc, pf, pf, pf, x, wa, wb, wc, wo)


def _ffn_down_body(*refs, tm, seq_tiles, expanded, final_norm):
    ua_ref, ub_ref, ha_ref, hb_ref = refs[:4]
    rest = refs[4:]
    if expanded:
        ha2_ref, hb2_ref, t_ref = rest[:3]
        rest = rest[3:]
    cwa_ref, cwb_ref, cba_ref, cbb_ref, wd_ref, x_ref = rest[:6]
    rest = rest[6:]
    if final_norm:
        gf_ref = rest[0]
        rest = rest[1:]
    o_ref, acc_ref = rest
    i = pl.program_id(0)
    k = pl.program_id(1)

    @pl.when(k == 0)
    def _():
        acc_ref[...] = jnp.zeros_like(acc_ref)

    def conv(u_ref, h_ref, h2_ref, cw_ref, cb_ref):
        u = u_ref[...]
        row = lax.broadcasted_iota(I32, u.shape, 0)
        r1 = pltpu.roll(u, 1, axis=0)
        r2 = pltpu.roll(u, 2, axis=0)
        if expanded:
            t = t_ref[...]
            u1 = jnp.where(t >= 1, r1, h_ref[...])
            u2 = jnp.where(t >= 2, r2, h2_ref[...])
        else:
            h = h_ref[...]
            h = jnp.where(i % seq_tiles == 0, jnp.zeros_like(h), h)
            u1 = jnp.where(row == 0, h[7:8], r1)
            u2 = jnp.where(row == 0, h[6:7], jnp.where(row == 1, h[7:8], r2))
        cw = cw_ref[...]
        return cb_ref[...] + cw[0:1] * u2 + cw[1:2] * u1 + cw[2:3] * u

    a = conv(ua_ref, ha_ref, ha2_ref if expanded else None, cwa_ref, cba_ref)
    b = conv(ub_ref, hb_ref, hb2_ref if expanded else None, cwb_ref, cbb_ref)
    acc_ref[...] += _dot((_silu(a) * b).astype(BF16), wd_ref[...])

    @pl.when(k == pl.num_programs(1) - 1)
    def _():
        y = x_ref[...] + acc_ref[...]
        if final_norm:
            ms = jnp.mean(y * y, axis=-1, keepdims=True)
            y = y * lax.rsqrt(ms + EPS) * gf_ref[...]
        o_ref[...] = y


def _ffn_down(u, x, cw, cb, wd, *, tm, tkf, seq_len, prev=None, g_final=None):
    T, D = x.shape
    F = wd.shape[0]
    nk = F // tkf
    expanded = prev is not None
    final_norm = g_final is not None
    seq_tiles = max(seq_len // tm, 1)
    hb8 = tm // 8
    ua = pl.BlockSpec((tm, tkf), lambda i, k: (i, k))
    ub = pl.BlockSpec((tm, tkf), lambda i, k: (i, nk + k))
    in_specs = [ua, ub]
    args = [u, u]
    if expanded:
        p1, p2, tpos = prev
        in_specs += [ua, ub, ua, ub, pl.BlockSpec((tm, 1), lambda i, k: (i, 0))]
        args += [p1, p1, p2, p2, tpos]
    else:
        in_specs += [pl.BlockSpec((8, tkf), lambda i, k: (jnp.maximum(i * hb8 - 1, 0), k)),
                     pl.BlockSpec((8, tkf), lambda i, k: (jnp.maximum(i * hb8 - 1, 0), nk + k))]
        args += [u, u]
    in_specs += [pl.BlockSpec((3, tkf), lambda i, k: (0, k)), pl.BlockSpec((3, tkf), lambda i, k: (0, nk + k)),
                 pl.BlockSpec((1, tkf), lambda i, k: (0, k)), pl.BlockSpec((1, tkf), lambda i, k: (0, nk + k)),
                 pl.BlockSpec((tkf, D), lambda i, k: (k, 0)), pl.BlockSpec((tm, D), lambda i, k: (i, 0))]
    args += [cw, cw, cb.reshape(1, -1), cb.reshape(1, -1), wd, x]
    if final_norm:
        in_specs.append(pl.BlockSpec((1, D), lambda i, k: (0, 0)))
        args.append(g_final.reshape(1, D))
    return pl.pallas_call(
        functools.partial(_ffn_down_body, tm=tm, seq_tiles=seq_tiles, expanded=expanded, final_norm=final_norm),
        grid=(T // tm, nk),
        in_specs=in_specs,
        out_specs=pl.BlockSpec((tm, D), lambda i, k: (i, 0)),
        out_shape=jax.ShapeDtypeStruct((T, D), F32),
        scratch_shapes=[pltpu.VMEM((tm, D), F32)],
        compiler_params=_cparams(("parallel", "arbitrary")),
        name="ffn_down",
    )(*args)


def _rotate_half_cols(w):
    D, N = w.shape
    w4 = w.reshape(D, N // DH, 2, DH // 2)
    return jnp.concatenate([-w4[:, :, 1], w4[:, :, 0]], axis=-1).reshape(D, N)


def _rope_tables(pos, width):
    half = DH // 2
    inv_freq = ROPE_THETA ** (-jnp.arange(half, dtype=F32) / half)
    ang = pos.astype(F32)[:, None] * inv_freq[None, :]
    reps = width // half
    return jnp.tile(jnp.cos(ang), (1, reps)), jnp.tile(jnp.sin(ang), (1, reps))


def _pick(n, prefs):
    for t in prefs:
        if n % t == 0:
            return t
    return n


def kernel(x_prompt, x_sample, cache_kv_a, cache_kv_c, cache_kidx_c, state_hgrn, state_ffn_conv, page_table,
           norm_mix, w_in, hgrn_lb_logits, hgrn_gnorm, w_br_a, w_br_b, w_br_c, w_out, norm_ffn, w_up, conv_w,
           conv_b, w_down, norm_final):
    B, L, D = x_prompt.shape
    Bs, Ls, _ = x_sample.shape
    depth = w_in.shape[0]
    NP = page_table.shape[1]
    past = NP * PAGE
    F = w_down.shape[1]
    WA = N_HEADS * DH
    WB = N_HEADS * DK
    Tp, Ts = B * L, Bs * Ls
    n_pool = cache_kv_a.shape[1]

    lb_soft = jax.nn.softmax(hgrn_lb_logits.astype(F32), axis=0)
    lb_all = jnp.cumsum(lb_soft, axis=0) - lb_soft[0]

    sizes = (WA, WA, WA, WB, WB, WB, WB, WA, WA, WA, N_HEADS * DH, DH, N_HEADS, 3 * D)
    offs = [0]
    for s in sizes:
        offs.append(offs[-1] + s)
    (o_qa, o_ka, o_va, o_qh, o_fh, o_ih, o_gh, o_qc, o_kc, o_vc, o_qi, o_ki, o_wi, o_gt, o_end) = offs
    QA_BLK, KA_BLK, VA_BLK, VC_BLK = (4 * WB) // WA, (4 * WB) // WA + 1, (4 * WB) // WA + 2, (4 * WB) // WA + 3
    gate_off = 4 * WB + 4 * WA
    assert gate_off % D == 0
    GATE_BLK = gate_off // D
    NPJ = gate_off + 3 * D
    N_ROPE_TILES = 4
    NRJ = (N_ROPE_TILES + 1) * WA
    wi_off = N_ROPE_TILES * WA
    WI_BLK = wi_off // LANES

    cache_a = cache_kv_a.transpose(0, 1, 3, 4, 5, 2).reshape(depth, n_pool, 2 * WA, PAGE)
    cache_c = cache_kv_c.transpose(0, 1, 3, 4, 5, 2).reshape(depth, n_pool, 2 * WA, PAGE)
    cache_i = cache_kidx_c.transpose(0, 1, 3, 2)

    cos_p, sin_p = _rope_tables(jnp.arange(L, dtype=I32), WA)
    cos_s, sin_s = _rope_tables(past + (jnp.arange(Ts, dtype=I32) % Ls), WA)

    tm_p = _pick(Tp, (1024, 512, 256, 128))
    tm_p = min(tm_p, L)
    tn_main = _pick(NPJ, (512, 256, 128))
    tn_up = _pick(2 * F, (512, 256, 128))
    tkf = _pick(F, (1408, 256, 128))
    tq_a = _pick(L, (256, 128))
    tq_c = _pick(L, (256, 128))
    tk_c = _pick(L, (512, 256, 128))
    C_h = _pick(L, (64,))
    G = _pick(NP, (16, 8, 4, 2, 1))
    n_sel_p = min(TOPK_MAX, L // 4)
    n_sel_s = min(TOPK_MAX, (past + Ls) // 4)

    xp = x_prompt.reshape(Tp, D)
    xs = x_sample.reshape(Ts, D)
    tpos_s = (jnp.arange(Ts, dtype=I32) % Ls).reshape(Ts, 1)

    def pad_rows(a, rows):
        return jnp.pad(a, ((0, 0), (0, rows - a.shape[1]), (0, 0)))

    outs_p = ([], [], [], [], [])
    outs_s = ([], [], [], [], [])
    for l in range(depth):
        w = w_in[l]
        col = lambda o, n: w[:, o:o + n]
        w_main = jnp.concatenate(
            [col(o_qh, 4 * WB), col(o_qa, 3 * WA), col(o_vc, WA), col(o_gt, 3 * D)], axis=1).astype(BF16)
        w_r = jnp.concatenate([col(o_qc, WA), col(o_kc, WA), col(o_qi, WA)] + [col(o_ki, DH)] * N_HEADS, axis=1)
        w_tail = jnp.concatenate([col(o_wi, N_HEADS), jnp.zeros((D, WA - N_HEADS), w.dtype)], axis=1)
        w_rot = jnp.concatenate([_rotate_half_cols(w_r), jnp.zeros((D, WA), w.dtype)], axis=1).astype(BF16)
        w_r = jnp.concatenate([w_r, w_tail], axis=1).astype(BF16)
        wa_b, wb_b, wc_b, wo_b = (t[l].astype(BF16) for t in (w_br_a, w_br_b, w_br_c, w_out))
        wup_b = w_up[l].astype(BF16)
        wd_b = w_down[l].astype(BF16)
        last = l == depth - 1

        vc_col = VC_BLK * WA
        pf, pb, vT = _rms_proj(xp, norm_mix[l], w_main, tm=tm_p, tn=tn_main,
                               t_outs=((vc_col // tn_main, vc_col % tn_main, WA, WA, tk_c, BF16),))
        rf, rb, qcT, qiT, wiT = _rms_proj(
            xp, norm_mix[l], w_r, tm=tm_p, tn=WA, rope_args=(w_rot, cos_p, sin_p), n_rope_tiles=N_ROPE_TILES,
            t_outs=((0, 0, WA, WA, tq_c, BF16), (2, 0, WA, WA, tq_c, BF16), (N_ROPE_TILES, 0, LANES, 8, tq_c, F32)))
        oa = _attn_a_prompt(pb, B, L, tq=tq_a, q_blk=QA_BLK, k_blk=KA_BLK, v_blk=VA_BLK)
        ob, s_new = _hgrn(pf.reshape(B, L, NPJ), lb_all[l], hgrn_gnorm[l], None, C=C_h, c=min(16, C_h), valid=C_h)
        oc = _dsa_prompt(qcT, qiT, wiT, rb, vT, B, L, tq=tq_c, tk=tk_c, n_sel=n_sel_p)
        x1 = _merge(oa, ob.reshape(Tp, WB), oc, pf, xp, wa_b, wb_b, wc_b, wo_b, tm=min(512, tm_p), gate_blk=GATE_BLK)
        u = _rms_proj(x1, norm_ffn[l], wup_b, tm=tm_p, tn=tn_up, emit_bf16=False)
        xp = _ffn_down(u, x1, conv_w[l], conv_b[l], wd_b, tm=min(512, tm_p), tkf=tkf, seq_len=L,
                       g_final=norm_final if last else None)
        kv_off = (KA_BLK * WA)
        outs_p[0].append(pf[:, kv_off:kv_off + 2 * WA].reshape(B, L, 2, N_HEADS, DH))
        outs_p[1].append(jnp.stack([rf[:, WA:2 * WA].reshape(B, L, N_HEADS, DH),
                                    pf[:, VC_BLK * WA:(VC_BLK + 1) * WA].reshape(B, L, N_HEADS, DH)], axis=2))
        outs_p[2].append(rf[:, 3 * WA:3 * WA + DH].reshape(B, L, DH))
        outs_p[3].append(s_new)
        outs_p[4].append(u.reshape(B, L, 2 * F)[:, L - 2:])

        pf, pb = _rms_proj(xs, norm_mix[l], w_main, tm=Ts, tn=tn_main)
        rf, rb = _rms_proj(xs, norm_mix[l], w_r, tm=Ts, tn=WA, rope_args=(w_rot, cos_s, sin_s),
                           n_rope_tiles=N_ROPE_TILES)
        pb3 = pb.reshape(Bs, Ls, NPJ)
        rb3 = rb.reshape(Bs, Ls, NRJ)
        blk = lambda a, k: a[:, :, k * WA:(k + 1) * WA]
        oa8 = _attn_a_sample(pad_rows(blk(pb3, QA_BLK), 8), pad_rows(blk(pb3, KA_BLK), PAGE),
                             pad_rows(blk(pb3, VA_BLK), PAGE), cache_a, l, page_table, G=G)
        ph = jnp.pad(pf[:, :4 * WB].reshape(Bs, Ls, 4 * WB), ((0, 0), (0, 8 - Ls), (0, 0)))
        ob8, s_new = _hgrn(ph, lb_all[l], hgrn_gnorm[l], state_hgrn[l], C=8, c=8, valid=Ls)
        qi = blk(rb3, 2).reshape(Bs, Ls, N_HEADS, DH).transpose(0, 2, 1, 3)
        qi32 = jnp.pad(qi, ((0, 0), (0, 0), (0, 8 - Ls), (0, 0))).reshape(Bs, 4 * 8, DH)
        wi = rf[:, wi_off:wi_off + N_HEADS].reshape(Bs, Ls, N_HEADS).transpose(0, 2, 1) * (N_HEADS ** -0.5)
        w32 = jnp.broadcast_to(jnp.pad(wi, ((0, 0), (0, 0), (0, 8 - Ls))).reshape(Bs, 32, 1), (Bs, 32, LANES))
        kin = pad_rows(rb3[:, :, 3 * WA:3 * WA + DH], PAGE)
        qc8 = pad_rows(blk(rb3, 0), 8)
        head = (jnp.arange(WA) // DH)[None, None, None, :] == jnp.arange(N_HEADS)[None, :, None, None]
        qc32 = (jnp.where(head, qc8[:, None], 0) * jnp.asarray(DH ** -0.5, BF16)).reshape(Bs, 32, WA).astype(BF16)
        oc8 = _dsa_sample(qi32, w32, kin, qc32, pad_rows(blk(rb3, 1), PAGE), pad_rows(blk(pb3, VC_BLK), PAGE),
                          cache_i, cache_c, l, page_table, G=G, n_new=Ls, n_sel=n_sel_s)
        x1 = _merge(oa8[:, :Ls].reshape(Ts, WA), ob8[:, :Ls].reshape(Ts, WB), oc8[:, :Ls].reshape(Ts, WA), pf, xs,
                    wa_b, wb_b, wc_b, wo_b, tm=Ts, gate_blk=GATE_BLK)
        u = _rms_proj(x1, norm_ffn[l], wup_b, tm=Ts, tn=tn_up, emit_bf16=False)
        prev = state_ffn_conv[l]
        u3 = u.reshape(Bs, Ls, 2 * F)
        p1 = jnp.broadcast_to(prev[:, 1:2], (Bs, Ls, 2 * F)).reshape(Ts, 2 * F)
        p2 = jnp.concatenate([prev, jnp.zeros((Bs, Ls - 2, 2 * F), F32)], axis=1).reshape(Ts, 2 * F)
        xs = _ffn_down(u, x1, conv_w[l], conv_b[l], wd_b, tm=Ts, tkf=tkf, seq_len=Ls, prev=(p1, p2, tpos_s),
                       g_final=norm_final if last else None)
        kv_off = (KA_BLK * WA)
        outs_s[0].append(pf[:, kv_off:kv_off + 2 * WA].reshape(Bs, Ls, 2, N_HEADS, DH))
        outs_s[1].append(jnp.stack([rf[:, WA:2 * WA].reshape(Bs, Ls, N_HEADS, DH),
                                    pf[:, VC_BLK * WA:(VC_BLK + 1) * WA].reshape(Bs, Ls, N_HEADS, DH)], axis=2))
        outs_s[2].append(rf[:, 3 * WA:3 * WA + DH].reshape(Bs, Ls, DH))
        outs_s[3].append(s_new)
        outs_s[4].append(jnp.concatenate([prev, u3], axis=1)[:, Ls:])

    st = lambda xs_: jnp.stack(xs_, axis=0)
    return (xp.reshape(B, L, D), xs.reshape(Bs, Ls, D),
            st(outs_p[0]), st(outs_s[0]), st(outs_p[1]), st(outs_s[1]), st(outs_p[2]), st(outs_s[2]),
            st(outs_p[3]), st(outs_s[3]), st(outs_p[4]), st(outs_s[4]))
```

```python
import functools

import jax
import jax.numpy as jnp
from jax import lax
from jax.experimental import pallas as pl
from jax.experimental.pallas import tpu as pltpu

F32 = jnp.float32
BF16 = jnp.bfloat16
I32 = jnp.int32

EPS = 1e-6
NEG_BIG = -1e30
LOG_F_MIN = -30.0
ROPE_THETA = 10000.0
TOPK_MAX = 256
PAGE = 128
N_HEADS = 4
DH = 64
DK = 128
LANES = 128
VMEM_LIMIT = 56 * 1024 * 1024
INT_MIN = -(2 ** 31)
EXP_UNDERFLOW = -110.0


def _cparams(sem):
    return pltpu.CompilerParams(dimension_semantics=sem, vmem_limit_bytes=VMEM_LIMIT)


def _dot(a, b):
    return jnp.dot(a, b, preferred_element_type=F32)


def _dot_nt(a, b):
    return lax.dot_general(a, b, (((1,), (1,)), ((), ())), preferred_element_type=F32)


def _dot_tn(a, b):
    return lax.dot_general(a, b, (((0,), (0,)), ((), ())), preferred_element_type=F32)


def _silu(x):
    return x * jax.nn.sigmoid(x)


def _stack_heads(q, rows):
    head = lax.broadcasted_iota(I32, (rows, N_HEADS * DH), 1) // DH
    return jnp.concatenate([jnp.where(head == h, q, jnp.zeros_like(q)) for h in range(N_HEADS)], axis=0)


def _unstack_heads(acc, rows):
    head = lax.broadcasted_iota(I32, (rows, N_HEADS * DH), 1) // DH
    out = jnp.zeros((rows, N_HEADS * DH), acc.dtype)
    for h in range(N_HEADS):
        out = jnp.where(head == h, acc[h * rows:(h + 1) * rows], out)
    return out


def _proj_body(*refs, rope, emit_bf16, n_rope_tiles, t_outs):
    x_ref, g_ref, w_ref = refs[:3]
    rest = refs[3:]
    if rope:
        wrot_ref, cos_ref, sin_ref = rest[:3]
        rest = rest[3:]
    of_ref = rest[0]
    ob_ref = rest[1] if emit_bf16 else None
    t_refs = rest[(2 if emit_bf16 else 1):-1]
    h_ref = rest[-1]

    @pl.when(pl.program_id(1) == 0)
    def _():
        x = x_ref[...]
        ms = jnp.mean(x * x, axis=-1, keepdims=True)
        h_ref[...] = (x * lax.rsqrt(ms + EPS) * g_ref[...]).astype(BF16)

    h = h_ref[...]
    acc = _dot(h, w_ref[...])
    if rope:
        roped = acc * cos_ref[...] + _dot(h, wrot_ref[...]) * sin_ref[...]
        acc = jnp.where(pl.program_id(1) < n_rope_tiles, roped, acc)
    of_ref[...] = acc
    if emit_bf16:
        ob_ref[...] = acc.astype(BF16)
    for t_ref, (j_tile, c0, width, keep, sub, dtype) in zip(t_refs, t_outs):
        @pl.when(pl.program_id(1) == j_tile)
        def _(t_ref=t_ref, c0=c0, width=width, keep=keep, sub=sub, dtype=dtype):
            for r in range(acc.shape[0] // sub):
                t_ref[r] = acc[r * sub:(r + 1) * sub, c0:c0 + width].T[:keep].astype(dtype)


def _rms_proj(x, g, w, *, tm, tn, rope_args=None, emit_bf16=True, n_rope_tiles=0, t_outs=()):
    T, D = x.shape
    N = w.shape[1]
    rope = rope_args is not None
    in_specs = [pl.BlockSpec((tm, D), lambda i, j: (i, 0)),
                pl.BlockSpec((1, D), lambda i, j: (0, 0)),
                pl.BlockSpec((D, tn), lambda i, j: (0, j))]
    args = [x, g.reshape(1, D), w]
    if rope:
        w_rot, cos, sin = rope_args
        nblk = cos.shape[0] // tm
        in_specs += [pl.BlockSpec((D, tn), lambda i, j: (0, j)),
                     pl.BlockSpec((tm, tn), lambda i, j: (i % nblk, 0)),
                     pl.BlockSpec((tm, tn), lambda i, j: (i % nblk, 0))]
        args += [w_rot, cos, sin]
    out_shape = [jax.ShapeDtypeStruct((T, N), F32)]
    out_specs = [pl.BlockSpec((tm, tn), lambda i, j: (i, j))]
    if emit_bf16:
        out_shape.append(jax.ShapeDtypeStruct((T, N), BF16))
        out_specs.append(pl.BlockSpec((tm, tn), lambda i, j: (i, j)))
    for (_, _, _, keep, sub, dtype) in t_outs:
        out_shape.append(jax.ShapeDtypeStruct((T // sub, keep, sub), dtype))
        out_specs.append(pl.BlockSpec((tm // sub, keep, sub), lambda i, j: (i, 0, 0)))
    outs = pl.pallas_call(
        functools.partial(_proj_body, rope=rope, emit_bf16=emit_bf16, n_rope_tiles=n_rope_tiles, t_outs=tuple(t_outs)),
        grid=(T // tm, N // tn),
        in_specs=in_specs, out_specs=out_specs, out_shape=out_shape,
        scratch_shapes=[pltpu.VMEM((tm, D), BF16)],
        compiler_params=_cparams(("parallel", "arbitrary")),
        name="rms_proj_rope" if rope else "rms_proj",
    )(*args)
    return outs if (emit_bf16 or t_outs) else outs[0]


def _suffix_matrix():
    j = lax.broadcasted_iota(I32, (2 * LANES, LANES), 0) & (LANES - 1)
    s = lax.broadcasted_iota(I32, (2 * LANES, LANES), 1)
    return jnp.where(j > s, 1.0, 0.0).astype(BF16)


def _log_one_minus_beta(z):
    return -(jnp.maximum(z, 0.0) + jnp.log(1.0 + jnp.exp(-jnp.abs(z))))


def _sb_update(z, carry, suffix, vis):
    M, tk = z.shape
    ls = _log_one_minus_beta(z)
    if vis is not None:
        ls = jnp.where(vis, ls, 0.0)
    hi = ls.astype(BF16)
    lo = (ls - hi.astype(F32)).astype(BF16)
    lz = z + ls
    n = tk // LANES
    outs = [None] * n
    for g in reversed(range(n)):
        sl = slice(g * LANES, (g + 1) * LANES)
        between = _dot(jnp.concatenate([hi[:, sl], lo[:, sl]], axis=1), suffix)
        e = lz[:, sl] + between + carry
        if vis is not None:
            e = jnp.where(vis[:, sl], e, NEG_BIG)
        outs[g] = jnp.exp(e)
        carry = carry + jnp.sum(ls[:, sl], axis=1, keepdims=True)
    a = outs[0] if n == 1 else jnp.concatenate(outs, axis=1)
    return a.astype(BF16), carry


def _attn_a_prompt_body(q_ref, k_ref, v_ref, o_ref, *, tq):
    i = pl.program_id(1)
    M = N_HEADS * tq
    qs = _stack_heads(q_ref[...], tq) * jnp.asarray(DH ** -0.5, BF16)
    suffix = _suffix_matrix()

    def block(k0, carry, acc, vis):
        kb = k_ref[pl.ds(k0, tq), :]
        vb = v_ref[pl.ds(k0, tq), :]
        a, carry = _sb_update(_dot_nt(qs, kb), carry, suffix, vis)
        return carry, acc + _dot(a, vb)

    row_t = lax.broadcasted_iota(I32, (M, tq), 0) & (tq - 1)
    col_s = lax.broadcasted_iota(I32, (M, tq), 1)
    carry, acc = block(pl.multiple_of(i * tq, tq), jnp.zeros((M, 1), F32),
                       jnp.zeros((M, N_HEADS * DH), F32), col_s < row_t)

    def alive(carry):
        return (jnp.max(carry) > EXP_UNDERFLOW).astype(I32)

    def cond(c):
        return (c[0] < i) & (c[1] > 0)

    def body(c):
        step = c[0]
        k0 = pl.multiple_of((i - 1 - step) * tq, tq)
        carry, acc = block(k0, c[2], c[3], None)
        return step + 1, alive(carry), carry, acc

    _, _, carry, acc = lax.while_loop(cond, body, (jnp.int32(0), alive(carry), carry, acc))
    o_ref[...] = _unstack_heads(acc, tq).astype(o_ref.dtype)


def _attn_a_prompt(pb, B, L, *, tq, q_blk, k_blk, v_blk):
    W = N_HEADS * DH
    nq = L // tq
    return pl.pallas_call(
        functools.partial(_attn_a_prompt_body, tq=tq),
        grid=(B, nq),
        in_specs=[pl.BlockSpec((tq, W), lambda b, i: (b * nq + i, q_blk)),
                  pl.BlockSpec((L, W), lambda b, i: (b, k_blk)),
                  pl.BlockSpec((L, W), lambda b, i: (b, v_blk))],
        out_specs=pl.BlockSpec((tq, W), lambda b, i: (b * nq + i, 0)),
        out_shape=jax.ShapeDtypeStruct((B * L, W), BF16),
        compiler_params=_cparams(("parallel", "arbitrary")),
        name="attn_a_prompt",
    )(pb, pb, pb)


def _attn_a_sample_body(pt_ref, q_ref, kn_ref, vn_ref, *rest, G):
    pages = rest[:G]
    o_ref, carry_ref, acc_ref = rest[G:]
    p = pl.program_id(1)
    W = N_HEADS * DH
    R = 8
    M = N_HEADS * R
    qs = _stack_heads(q_ref[0], R) * jnp.asarray(DH ** -0.5, BF16)
    suffix = _suffix_matrix()

    @pl.when(p == 0)
    def _():
        row_t = lax.broadcasted_iota(I32, (M, PAGE), 0) & (R - 1)
        col_s = lax.broadcasted_iota(I32, (M, PAGE), 1)
        a, carry = _sb_update(_dot_nt(qs, kn_ref[0]), jnp.zeros((M, 1), F32), suffix, col_s < row_t)
        carry_ref[...] = jnp.broadcast_to(carry, carry_ref.shape)
        acc_ref[...] = _dot(a, vn_ref[0])

    kcat = jnp.concatenate([pages[g][:W, :] for g in reversed(range(G))], axis=1).astype(BF16)
    vcat = jnp.concatenate([pages[g][W:, :] for g in reversed(range(G))], axis=1).astype(BF16)
    a, carry = _sb_update(_dot(qs, kcat), carry_ref[:, 0:1], suffix, None)
    carry_ref[...] = jnp.broadcast_to(carry, carry_ref.shape)
    acc_ref[...] += _dot_nt(a, vcat)

    @pl.when(p == pl.num_programs(1) - 1)
    def _():
        o_ref[0] = _unstack_heads(acc_ref[...], R).astype(o_ref.dtype)


def _attn_a_sample(q8, kn, vn, cache, layer, page_table, *, G):
    Bs = q8.shape[0]
    NP = page_table.shape[1]
    W = N_HEADS * DH

    def page_spec(g):
        return pl.BlockSpec((None, None, 2 * W, PAGE),
                            lambda b, p, pt: (layer, pt[b, NP - 1 - (p * G + g)], 0, 0))

    grid_spec = pltpu.PrefetchScalarGridSpec(
        num_scalar_prefetch=1,
        grid=(Bs, NP // G),
        in_specs=[pl.BlockSpec((1, 8, W), lambda b, p, pt: (b, 0, 0)),
                  pl.BlockSpec((1, PAGE, W), lambda b, p, pt: (b, 0, 0)),
                  pl.BlockSpec((1, PAGE, W), lambda b, p, pt: (b, 0, 0))] + [page_spec(g) for g in range(G)],
        out_specs=pl.BlockSpec((1, 8, W), lambda b, p, pt: (b, 0, 0)),
        scratch_shapes=[pltpu.VMEM((N_HEADS * 8, LANES), F32), pltpu.VMEM((N_HEADS * 8, W), F32)],
    )
    return pl.pallas_call(
        functools.partial(_attn_a_sample_body, G=G),
        grid_spec=grid_spec,
        out_shape=jax.ShapeDtypeStruct((Bs, 8, W), BF16),
        compiler_params=_cparams(("parallel", "arbitrary")),
        name="attn_a_sample",
    )(page_table, q8, kn, vn, *([cache] * G))


def _cumsum_rows(x):
    C = x.shape[0]
    row = lax.broadcasted_iota(I32, x.shape, 0)
    sh = 1
    while sh < C:
        x = x + jnp.where(row >= sh, pltpu.roll(x, sh, axis=0), 0.0)
        sh *= 2
    return x


def _hgrn_body(*refs, C, c, valid, has_s0):
    q_ref, f_ref, i_ref, g_ref, lb_ref, gn_ref = refs[:6]
    rest = refs[6:]
    if has_s0:
        s0_ref = rest[0]
        rest = rest[1:]
    o_ref, sout_ref, st_ref = rest
    ci = pl.program_id(1)

    @pl.when(ci == 0)
    def _():
        for h in range(N_HEADS):
            st_ref[h] = s0_ref[0, h].T if has_s0 else jnp.zeros((DK, DK), F32)

    row = lax.broadcasted_iota(I32, (C, DK), 0)
    rowc = lax.broadcasted_iota(I32, (c, 1), 0)
    for h in range(N_HEADS):
        hs = slice(h * DK, (h + 1) * DK)
        kk = (1.0 - lb_ref[:, hs]) * jax.nn.sigmoid(-f_ref[0, :, hs])
        lg = jnp.maximum(jnp.log1p(-kk), LOG_F_MIN)
        if valid < C:
            kk = jnp.where(row < valid, kk, 0.0)
            lg = jnp.where(row < valid, lg, 0.0)
        qq = _silu(q_ref[0, :, hs])
        vv = i_ref[0, :, hs]
        cum = _cumsum_rows(lg)
        st = st_ref[h]
        o = _dot_nt((qq * jnp.exp(cum)).astype(BF16), st.astype(BF16))
        parts = []
        for blk in range(C // c):
            r0 = blk * c
            q_b = qq[r0:r0 + c]
            cum_b = cum[r0:r0 + c]
            o_b = jnp.zeros((c, DK), F32)
            if blk > 0:
                base = cum[r0 - 1:r0]
                qt = q_b * jnp.exp(cum_b - base)
                kt = kk[:r0] * jnp.exp(base - cum[:r0])
                sc = _dot_nt(qt.astype(BF16), kt.astype(BF16))
                o_b = o_b + _dot(sc.astype(BF16), vv[:r0].astype(BF16))
            for s in range(c):
                r = r0 + s
                d = jnp.minimum(cum_b - cum[r:r + 1], 0.0)
                w = jnp.sum(q_b * kk[r:r + 1] * jnp.exp(d), axis=1, keepdims=True)
                o_b = o_b + jnp.where(rowc >= s, w, 0.0) * vv[r:r + 1]
            parts.append(o_b)
        o = o + (parts[0] if len(parts) == 1 else jnp.concatenate(parts, axis=0))
        last = cum[C - 1:C]
        kd = kk * jnp.exp(last - cum)
        st_ref[h] = st * jnp.exp(last) + _dot_tn(vv.astype(BF16), kd.astype(BF16))
        ms = jnp.mean(o * o, axis=1, keepdims=True)
        y = o * lax.rsqrt(ms + EPS) * gn_ref[...] * _silu(g_ref[0, :, hs])
        o_ref[0, :, hs] = y.astype(o_ref.dtype)

    @pl.when(ci == pl.num_programs(1) - 1)
    def _():
        for h in range(N_HEADS):
            sout_ref[0, h] = st_ref[h].T


def _hgrn(p3, lb, gn, s0, *, C, c, valid, col0=0):
    B, L, _ = p3.shape
    W = N_HEADS * DK
    has_s0 = s0 is not None
    in_specs = [pl.BlockSpec((1, C, W), functools.partial(lambda b, ci, k: (b, ci, col0 + k), k=k)) for k in range(4)]
    in_specs += [pl.BlockSpec((1, W), lambda b, ci: (0, 0)), pl.BlockSpec((1, DK), lambda b, ci: (0, 0))]
    args = [p3, p3, p3, p3, lb.reshape(1, W), gn.reshape(1, DK)]
    if has_s0:
        in_specs.append(pl.BlockSpec((1, N_HEADS, DK, DK), lambda b, ci: (b, 0, 0, 0)))
        args.append(s0)
    return pl.pallas_call(
        functools.partial(_hgrn_body, C=C, c=c, valid=valid, has_s0=has_s0),
        grid=(B, L // C),
        in_specs=in_specs,
        out_specs=[pl.BlockSpec((1, C, W), lambda b, ci: (b, ci, 0)),
                   pl.BlockSpec((1, N_HEADS, DK, DK), lambda b, ci: (b, 0, 0, 0))],
        out_shape=[jax.ShapeDtypeStruct((B, L, W), BF16), jax.ShapeDtypeStruct((B, N_HEADS, DK, DK), F32)],
        scratch_shapes=[pltpu.VMEM((N_HEADS, DK, DK), F32)],
        compiler_params=_cparams(("parallel", "arbitrary")),
        name="hgrn2",
    )(*args)


def _sort_key(score):
    b = pltpu.bitcast(score, I32)
    return jnp.where(b < 0, b ^ jnp.int32(0x7FFFFFFF), b)


def _prefix_matrix(n):
    j = lax.broadcasted_iota(I32, (n, n + LANES), 0)
    s = lax.broadcasted_iota(I32, (n, n + LANES), 1)
    return jnp.where((s >= n) | (j <= s), 1.0, 0.0).astype(BF16)


def _kth_largest_key(count_ge, rows, n_sel):
    def body(t, T):
        cand = T + jnp.left_shift(jnp.int32(1), 31 - t)
        return jnp.where(count_ge(cand) >= n_sel, cand, T)
    return lax.fori_loop(0, 32, body, jnp.full((rows, 1), INT_MIN, I32))


def _dsa_prompt_body(qcT_ref, qiT_ref, wiT_ref, kc_ref, ki_ref, vT_ref, lt_ref, o_ref, keys_ref, kh_ref, kl_ref,
                     *, tq, tk, n_sel):
    i = pl.program_id(1)
    nkb = ((i + 1) * tq + tk - 1) // tk
    M = N_HEADS * tq
    W = N_HEADS * DH
    I16 = jnp.int16
    kpos0 = lax.broadcasted_iota(I32, (tk, tq), 0)
    qpos = i * tq + lax.broadcasted_iota(I32, (tk, tq), 1)
    row_head = lax.broadcasted_iota(I32, (W, tq), 0) // DH

    def stack_lanes(qT):
        return jnp.concatenate([jnp.where(row_head == h, qT, jnp.zeros_like(qT)) for h in range(N_HEADS)], axis=1)

    qisT = stack_lanes(qiT_ref[0])
    wi = wiT_ref[0]
    w_row = jnp.concatenate([wi[j:j + 1, :] for j in range(N_HEADS)], axis=1) * (N_HEADS ** -0.5) * (DH ** -0.5)

    def score_body(j, _):
        k0 = pl.multiple_of(j * tk, tk)
        d = jnp.maximum(_dot(ki_ref[pl.ds(k0, tk), :], qisT), 0.0) * w_row
        score = d[:, 0:tq]
        for jh in range(1, N_HEADS):
            score = score + d[:, jh * tq:(jh + 1) * tq]
        score = jnp.where(kpos0 + k0 <= qpos, score + 0.0, NEG_BIG)
        key = _sort_key(score)
        keys_ref[j] = key
        kh_ref[j] = (key >> 16).astype(I16)
        kl_ref[j] = ((key & 0xFFFF) - 32768).astype(I16)
        return 0

    lax.fori_loop(0, nkb, score_body, 0)

    def count16(ref, cand):
        c16 = cand.astype(I16)

        def body(j, acc):
            m = jnp.where(ref[j] >= c16, jnp.ones((), BF16), jnp.zeros((), BF16))
            parts = [m[16 * r:16 * (r + 1)] for r in range(tk // 16)]
            while len(parts) > 1:
                parts = [parts[a] + parts[a + 1] for a in range(0, len(parts) - 1, 2)] + parts[len(parts) & ~1:]
            return acc + parts[0].astype(F32)

        acc = lax.fori_loop(0, nkb, body, jnp.zeros((16, tq), F32))
        return jnp.sum(acc, axis=0, keepdims=True)

    def kth16(ref, target):
        def body(t, T):
            cand = T + jnp.left_shift(jnp.int32(1), 15 - t)
            return jnp.where(count16(ref, cand) >= target, cand, T)
        return lax.fori_loop(0, 16, body, jnp.full((1, tq), -32768, I32))

    def count_above(ref, T):
        return jnp.where(T >= 32767, 0.0, count16(ref, jnp.minimum(T + 1, 32767)))

    TH = kth16(kh_ref, float(n_sel))
    need = n_sel - count_above(kh_ref, TH)
    th16 = TH.astype(I16)

    def low_body(j, _):
        kl_ref[j] = jnp.where(kh_ref[j] == th16, kl_ref[j], jnp.full((), -32768, I16))
        return 0

    lax.fori_loop(0, nkb, low_body, 0)
    TL = kth16(kl_ref, need)
    room = need - count_above(kl_ref, TL)
    T = TH * 65536 + (TL + 32768)

    qcsT = stack_lanes(qcT_ref[0]) * jnp.asarray(DH ** -0.5, BF16)
    lt = lt_ref[...]

    def att_body(j, c):
        m, l, accT, eq_before = c
        k0 = pl.multiple_of(j * tk, tk)
        key = keys_ref[j]
        eq = key == T
        rank = _dot(lt, jnp.where(eq, 1.0, 0.0).astype(BF16)) + eq_before
        sel = (key > T) | (eq & (rank <= room))
        bias = jnp.where(sel & (kpos0 + k0 <= qpos), 0.0, NEG_BIG)
        s = _dot(kc_ref[pl.ds(k0, tk), :], qcsT) + jnp.concatenate([bias] * N_HEADS, axis=1)
        m_new = jnp.maximum(m, jnp.max(s, axis=0, keepdims=True))
        alpha = jnp.exp(m - m_new)
        p = jnp.exp(s - m_new)
        l = alpha * l + jnp.sum(p, axis=0, keepdims=True)
        accT = alpha * accT + _dot(vT_ref[j], p.astype(BF16))
        return m_new, l, accT, rank[tk - 1:tk, :]

    init = (jnp.full((1, M), NEG_BIG, F32), jnp.zeros((1, M), F32), jnp.zeros((W, M), F32), jnp.zeros((1, tq), F32))
    m, l, accT, _ = lax.fori_loop(0, nkb, att_body, init)
    outT = accT / l
    oT = jnp.zeros((W, tq), F32)
    for h in range(N_HEADS):
        oT = jnp.where(row_head == h, outT[:, h * tq:(h + 1) * tq], oT)
    o_ref[...] = oT.T.astype(o_ref.dtype)


def _dsa_prompt(qcT, qiT, wiT, rb, vT, B, L, *, tq, tk, n_sel):
    W = N_HEADS * DH
    nq = L // tq
    assert tk // 16 <= 256
    lt = jnp.where(jnp.arange(tk)[None, :] <= jnp.arange(tk)[:, None], 1.0, 0.0).astype(BF16)
    return pl.pallas_call(
        functools.partial(_dsa_prompt_body, tq=tq, tk=tk, n_sel=n_sel),
        grid=(B, nq),
        in_specs=[pl.BlockSpec((1, W, tq), lambda b, i: (b * nq + i, 0, 0)),
                  pl.BlockSpec((1, W, tq), lambda b, i: (b * nq + i, 0, 0)),
                  pl.BlockSpec((1, 8, tq), lambda b, i: (b * nq + i, 0, 0)),
                  pl.BlockSpec((L, W), lambda b, i: (b, 1)),
                  pl.BlockSpec((L, W), lambda b, i: (b, 3)),
                  pl.BlockSpec((L // tk, W, tk), lambda b, i: (b, 0, 0)),
                  pl.BlockSpec((tk, tk), lambda b, i: (0, 0))],
        out_specs=pl.BlockSpec((tq, W), lambda b, i: (b * nq + i, 0)),
        out_shape=jax.ShapeDtypeStruct((B * L, W), BF16),
        scratch_shapes=[pltpu.VMEM((L // tk, tk, tq), I32), pltpu.VMEM((L // tk, tk, tq), jnp.int16),
                        pltpu.VMEM((L // tk, tk, tq), jnp.int16)],
        compiler_params=_cparams(("parallel", "arbitrary")),
        name="dsa_prompt",
    )(qcT, qiT, wiT, rb, rb, vT, lt)


def _dsa_s_score_body(pt_ref, qi_ref, w_ref, kn_ref, *rest, G, n_new):
    pages = rest[:G]
    o_ref = rest[G]
    p = pl.program_id(1)
    R = 8
    qi = qi_ref[0]
    w = w_ref[0]

    def score(dots):
        d = jnp.maximum(dots * (DH ** -0.5), 0.0) * w
        return (d[0:R] + d[R:2 * R]) + (d[2 * R:3 * R] + d[3 * R:4 * R]) + 0.0

    @pl.when(p < pl.num_programs(1) - 1)
    def _():
        for g in range(G):
            o_ref[0, :, g * PAGE:(g + 1) * PAGE] = score(_dot(qi, pages[g][...].astype(BF16)))

    @pl.when(p == pl.num_programs(1) - 1)
    def _():
        t = lax.broadcasted_iota(I32, (R, PAGE), 0)
        s = lax.broadcasted_iota(I32, (R, PAGE), 1)
        o_ref[0, :, 0:PAGE] = jnp.where((s <= t) & (s < n_new), score(_dot_nt(qi, kn_ref[0])), NEG_BIG)
        if G > 1:
            o_ref[0, :, PAGE:] = jnp.full((R, (G - 1) * PAGE), NEG_BIG, F32)


def _dsa_s_select_body(s_ref, o_ref, *, n_sel):
    R = 8
    key = _sort_key(s_ref[0])
    W = key.shape[1]

    def count_ge(cand):
        return jnp.sum(jnp.where(key >= cand, 1.0, 0.0), axis=1, keepdims=True)

    T = _kth_largest_key(count_ge, R, n_sel)
    room = n_sel - count_ge(T + 1)
    eq = key == T
    gt = key > T
    nt = W // LANES
    eqf = jnp.where(eq, 1.0, 0.0)
    stacked = jnp.concatenate([eqf[:, g * LANES:(g + 1) * LANES] for g in range(nt)], axis=0).astype(BF16)
    pc = _dot(stacked, _prefix_matrix(LANES))
    before = jnp.zeros((R, LANES), F32)
    tiles = []
    for g in range(nt):
        sl = slice(g * LANES, (g + 1) * LANES)
        rank = pc[g * R:(g + 1) * R, :LANES] + before
        tiles.append(jnp.where(gt[:, sl], 1.0, jnp.where(eq[:, sl] & (rank <= room), 1.0, 0.0)))
        before = before + pc[g * R:(g + 1) * R, LANES:]
    o_ref[0] = jnp.concatenate(tiles, axis=1)


def _dsa_s_attend_body(pt_ref, q_ref, kn_ref, vn_ref, mask_ref, maskn_ref, *rest, G, n_new):
    pages = rest[:G]
    o_ref, m_ref, l_ref, acc_ref = rest[G:]
    p = pl.program_id(1)
    W = N_HEADS * DH
    R = 8
    qs = q_ref[0]

    def update(s, v, m, l, acc, v_feature_major=False):
        m_new = jnp.maximum(m, jnp.max(s, axis=1, keepdims=True))
        alpha = jnp.exp(m - m_new)
        pr = jnp.exp(s - m_new)
        pv = _dot_nt(pr.astype(BF16), v) if v_feature_major else _dot(pr.astype(BF16), v)
        return m_new, alpha * l + jnp.sum(pr, axis=1, keepdims=True), alpha * acc + pv

    @pl.when(p == 0)
    def _():
        t = lax.broadcasted_iota(I32, (R, PAGE), 0)
        sidx = lax.broadcasted_iota(I32, (R, PAGE), 1)
        sel = jnp.where((sidx <= t) & (sidx < n_new), maskn_ref[0, :, 0:PAGE], 0.0)
        s = jnp.where(jnp.concatenate([sel] * N_HEADS, axis=0) > 0.5, _dot_nt(qs, kn_ref[0]), NEG_BIG)
        m, l, acc = update(s, vn_ref[0], jnp.full((N_HEADS * R, 1), NEG_BIG, F32),
                           jnp.zeros((N_HEADS * R, 1), F32), jnp.zeros((N_HEADS * R, W), F32))
        m_ref[...] = jnp.broadcast_to(m, m_ref.shape)
        l_ref[...] = jnp.broadcast_to(l, l_ref.shape)
        acc_ref[...] = acc

    kcat = jnp.concatenate([pages[g][:W, :] for g in range(G)], axis=1).astype(BF16)
    vcat = jnp.concatenate([pages[g][W:, :] for g in range(G)], axis=1).astype(BF16)
    sel = jnp.concatenate([mask_ref[0]] * N_HEADS, axis=0)
    s = jnp.where(sel > 0.5, _dot(qs, kcat), NEG_BIG)
    m, l, acc = update(s, vcat, m_ref[:, 0:1], l_ref[:, 0:1], acc_ref[...], v_feature_major=True)
    m_ref[...] = jnp.broadcast_to(m, m_ref.shape)
    l_ref[...] = jnp.broadcast_to(l, l_ref.shape)
    acc_ref[...] = acc

    @pl.when(p == pl.num_programs(1) - 1)
    def _():
        o_ref[0] = _unstack_heads(acc_ref[...] / l_ref[:, 0:1], R).astype(o_ref.dtype)


def _dsa_sample(qi32, w32, kin, qc32, kcn, vcn, cache_kidx, cache_kv, layer, page_table, *, G, n_new, n_sel):
    Bs = qi32.shape[0]
    NP = page_table.shape[1]
    W = N_HEADS * DH
    nstep = NP // G
    width = (nstep + 1) * G * PAGE

    def kidx_spec(g):
        return pl.BlockSpec((None, None, DH, PAGE),
                            lambda b, p, pt: (layer, pt[b, jnp.minimum(p * G + g, NP - 1)], 0, 0))

    scores = pl.pallas_call(
        functools.partial(_dsa_s_score_body, G=G, n_new=n_new),
        grid_spec=pltpu.PrefetchScalarGridSpec(
            num_scalar_prefetch=1, grid=(Bs, nstep + 1),
            in_specs=[pl.BlockSpec((1, 32, DH), lambda b, p, pt: (b, 0, 0)),
                      pl.BlockSpec((1, 32, LANES), lambda b, p, pt: (b, 0, 0)),
                      pl.BlockSpec((1, PAGE, DH), lambda b, p, pt: (b, 0, 0))] + [kidx_spec(g) for g in range(G)],
            out_specs=pl.BlockSpec((1, 8, G * PAGE), lambda b, p, pt: (b, 0, p))),
        out_shape=jax.ShapeDtypeStruct((Bs, 8, width), F32),
        compiler_params=_cparams(("parallel", "arbitrary")),
        name="dsa_sample_score",
    )(page_table, qi32, w32, kin, *([cache_kidx] * G))

    mask = pl.pallas_call(
        functools.partial(_dsa_s_select_body, n_sel=n_sel),
        grid=(Bs,),
        in_specs=[pl.BlockSpec((1, 8, width), lambda b: (b, 0, 0))],
        out_specs=pl.BlockSpec((1, 8, width), lambda b: (b, 0, 0)),
        out_shape=jax.ShapeDtypeStruct((Bs, 8, width), F32),
        compiler_params=_cparams(("parallel",)),
        name="dsa_sample_select",
    )(scores)

    def kv_spec(g):
        return pl.BlockSpec((None, None, 2 * W, PAGE), lambda b, p, pt: (layer, pt[b, p * G + g], 0, 0))

    return pl.pallas_call(
        functools.partial(_dsa_s_attend_body, G=G, n_new=n_new),
        grid_spec=pltpu.PrefetchScalarGridSpec(
            num_scalar_prefetch=1, grid=(Bs, nstep),
            in_specs=[pl.BlockSpec((1, 32, W), lambda b, p, pt: (b, 0, 0)),
                      pl.BlockSpec((1, PAGE, W), lambda b, p, pt: (b, 0, 0)),
                      pl.BlockSpec((1, PAGE, W), lambda b, p, pt: (b, 0, 0)),
                      pl.BlockSpec((1, 8, G * PAGE), lambda b, p, pt: (b, 0, p)),
                      pl.BlockSpec((1, 8, G * PAGE), lambda b, p, pt: (b, 0, nstep))] + [kv_spec(g) for g in range(G)],
            out_specs=pl.BlockSpec((1, 8, W), lambda b, p, pt: (b, 0, 0)),
            scratch_shapes=[pltpu.VMEM((32, LANES), F32), pltpu.VMEM((32, LANES), F32), pltpu.VMEM((32, W), F32)]),
        out_shape=jax.ShapeDtypeStruct((Bs, 8, W), BF16),
        compiler_params=_cparams(("parallel", "arbitrary")),
        name="dsa_sample_attend",
    )(page_table, qc32, kcn, vcn, mask, mask, *([cache_kv] * G))


def _merge_body(oa_ref, ob_ref, oc_ref, g0_ref, g1_ref, g2_ref, x_ref, wa_ref, wb_ref, wc_ref, wo_ref, o_ref):
    merged = (jax.nn.sigmoid(g0_ref[...]) * _dot(oa_ref[...], wa_ref[...])
              + jax.nn.sigmoid(g1_ref[...]) * _dot(ob_ref[...], wb_ref[...])
              + jax.nn.sigmoid(g2_ref[...]) * _dot(oc_ref[...], wc_ref[...]))
    o_ref[...] = x_ref[...] + _dot(merged.astype(BF16), wo_ref[...])


def _merge(oa, ob, oc, pf, x, wa, wb, wc, wo, *, tm, gate_blk):
    T, D = x.shape
    row = lambda i: (i, 0)
    const = lambda i: (0, 0)
    return pl.pallas_call(
        _merge_body,
        grid=(T // tm,),
        in_specs=[pl.BlockSpec((tm, oa.shape[1]), row), pl.BlockSpec((tm, ob.shape[1]), row),
                  pl.BlockSpec((tm, oc.shape[1]), row),
                  pl.BlockSpec((tm, D), lambda i: (i, gate_blk)), pl.BlockSpec((tm, D), lambda i: (i, gate_blk + 1)),
                  pl.BlockSpec((tm, D), lambda i: (i, gate_blk + 2)), pl.BlockSpec((tm, D), row),
                  pl.BlockSpec(wa.shape, const), pl.BlockSpec(wb.shape, const),
                  pl.BlockSpec(wc.shape, const), pl.BlockSpec(wo.shape, const)],
        out_specs=pl.BlockSpec((tm, D), row),
        out_shape=jax.ShapeDtypeStruct((T, D), F32),
        compiler_params=_cparams(("parallel",)),
        name="merge",
    )(oa, ob, oc, pf, pf, pf, x, wa, wb, wc, wo)


def _ffn_down_body(*refs, tm, seq_tiles, expanded, final_norm):
    ua_ref, ub_ref, ha_ref, hb_ref = refs[:4]
    rest = refs[4:]
    if expanded:
        ha2_ref, hb2_ref, t_ref = rest[:3]
        rest = rest[3:]
    cwa_ref, cwb_ref, cba_ref, cbb_ref, wd_ref, x_ref = rest[:6]
    rest = rest[6:]
    if final_norm:
        gf_ref = rest[0]
        rest = rest[1:]
    o_ref, acc_ref = rest
    i = pl.program_id(0)
    k = pl.program_id(1)

    @pl.when(k == 0)
    def _():
        acc_ref[...] = jnp.zeros_like(acc_ref)

    def conv(u_ref, h_ref, h2_ref, cw_ref, cb_ref):
        u = u_ref[...]
        row = lax.broadcasted_iota(I32, u.shape, 0)
        r1 = pltpu.roll(u, 1, axis=0)
        r2 = pltpu.roll(u, 2, axis=0)
        if expanded:
            t = t_ref[...]
            u1 = jnp.where(t >= 1, r1, h_ref[...])
            u2 = jnp.where(t >= 2, r2, h2_ref[...])
        else:
            h = h_ref[...]
            h = jnp.where(i % seq_tiles == 0, jnp.zeros_like(h), h)
            u1 = jnp.where(row == 0, h[7:8], r1)
            u2 = jnp.where(row == 0, h[6:7], jnp.where(row == 1, h[7:8], r2))
        cw = cw_ref[...]
        return cb_ref[...] + cw[0:1] * u2 + cw[1:2] * u1 + cw[2:3] * u

    a = conv(ua_ref, ha_ref, ha2_ref if expanded else None, cwa_ref, cba_ref)
    b = conv(ub_ref, hb_ref, hb2_ref if expanded else None, cwb_ref, cbb_ref)
    acc_ref[...] += _dot((_silu(a) * b).astype(BF16), wd_ref[...])

    @pl.when(k == pl.num_programs(1) - 1)
    def _():
        y = x_ref[...] + acc_ref[...]
        if final_norm:
            ms = jnp.mean(y * y, axis=-1, keepdims=True)
            y = y * lax.rsqrt(ms + EPS) * gf_ref[...]
        o_ref[...] = y


def _ffn_down(u, x, cw, cb, wd, *, tm, tkf, seq_len, prev=None, g_final=None):
    T, D = x.shape
    F = wd.shape[0]
    nk = F // tkf
    expanded = prev is not None
    final_norm = g_final is not None
    seq_tiles = max(seq_len // tm, 1)
    hb8 = tm // 8
    ua = pl.BlockSpec((tm, tkf), lambda i, k: (i, k))
    ub = pl.BlockSpec((tm, tkf), lambda i, k: (i, nk + k))
    in_specs = [ua, ub]
    args = [u, u]
    if expanded:
        p1, p2, tpos = prev
        in_specs += [ua, ub, ua, ub, pl.BlockSpec((tm, 1), lambda i, k: (i, 0))]
        args += [p1, p1, p2, p2, tpos]
    else:
        in_specs += [pl.BlockSpec((8, tkf), lambda i, k: (jnp.maximum(i * hb8 - 1, 0), k)),
                     pl.BlockSpec((8, tkf), lambda i, k: (jnp.maximum(i * hb8 - 1, 0), nk + k))]
        args += [u, u]
    in_specs += [pl.BlockSpec((3, tkf), lambda i, k: (0, k)), pl.BlockSpec((3, tkf), lambda i, k: (0, nk + k)),
                 pl.BlockSpec((1, tkf), lambda i, k: (0, k)), pl.BlockSpec((1, tkf), lambda i, k: (0, nk + k)),
                 pl.BlockSpec((tkf, D), lambda i, k: (k, 0)), pl.BlockSpec((tm, D), lambda i, k: (i, 0))]
    args += [cw, cw, cb.reshape(1, -1), cb.reshape(1, -1), wd, x]
    if final_norm:
        in_specs.append(pl.BlockSpec((1, D), lambda i, k: (0, 0)))
        args.append(g_final.reshape(1, D))
    return pl.pallas_call(
        functools.partial(_ffn_down_body, tm=tm, seq_tiles=seq_tiles, expanded=expanded, final_norm=final_norm),
        grid=(T // tm, nk),
        in_specs=in_specs,
        out_specs=pl.BlockSpec((tm, D), lambda i, k: (i, 0)),
        out_shape=jax.ShapeDtypeStruct((T, D), F32),
        scratch_shapes=[pltpu.VMEM((tm, D), F32)],
        compiler_params=_cparams(("parallel", "arbitrary")),
        name="ffn_down",
    )(*args)


def _rotate_half_cols(w):
    D, N = w.shape
    w4 = w.reshape(D, N // DH, 2, DH // 2)
    return jnp.concatenate([-w4[:, :, 1], w4[:, :, 0]], axis=-1).reshape(D, N)


def _rope_tables(pos, width):
    half = DH // 2
    inv_freq = ROPE_THETA ** (-jnp.arange(half, dtype=F32) / half)
    ang = pos.astype(F32)[:, None] * inv_freq[None, :]
    reps = width // half
    return jnp.tile(jnp.cos(ang), (1, reps)), jnp.tile(jnp.sin(ang), (1, reps))


def _pick(n, prefs):
    for t in prefs:
        if n % t == 0:
            return t
    return n


def kernel(x_prompt, x_sample, cache_kv_a, cache_kv_c, cache_kidx_c, state_hgrn, state_ffn_conv, page_table,
           norm_mix, w_in, hgrn_lb_logits, hgrn_gnorm, w_br_a, w_br_b, w_br_c, w_out, norm_ffn, w_up, conv_w,
           conv_b, w_down, norm_final):
    B, L, D = x_prompt.shape
    Bs, Ls, _ = x_sample.shape
    depth = w_in.shape[0]
    NP = page_table.shape[1]
    past = NP * PAGE
    F = w_down.shape[1]
    WA = N_HEADS * DH
    WB = N_HEADS * DK
    Tp, Ts = B * L, Bs * Ls
    n_pool = cache_kv_a.shape[1]

    lb_soft = jax.nn.softmax(hgrn_lb_logits.astype(F32), axis=0)
    lb_all = jnp.cumsum(lb_soft, axis=0) - lb_soft[0]

    sizes = (WA, WA, WA, WB, WB, WB, WB, WA, WA, WA, N_HEADS * DH, DH, N_HEADS, 3 * D)
    offs = [0]
    for s in sizes:
        offs.append(offs[-1] + s)
    (o_qa, o_ka, o_va, o_qh, o_fh, o_ih, o_gh, o_qc, o_kc, o_vc, o_qi, o_ki, o_wi, o_gt, o_end) = offs
    QA_BLK, KA_BLK, VA_BLK, VC_BLK = (4 * WB) // WA, (4 * WB) // WA + 1, (4 * WB) // WA + 2, (4 * WB) // WA + 3
    gate_off = 4 * WB + 4 * WA
    assert gate_off % D == 0
    GATE_BLK = gate_off // D
    NPJ = gate_off + 3 * D
    N_ROPE_TILES = 4
    NRJ = (N_ROPE_TILES + 1) * WA
    wi_off = N_ROPE_TILES * WA
    WI_BLK = wi_off // LANES

    cache_a = cache_kv_a.transpose(0, 1, 3, 4, 5, 2).reshape(depth, n_pool, 2 * WA, PAGE)
    cache_c = cache_kv_c.transpose(0, 1, 3, 4, 5, 2).reshape(depth, n_pool, 2 * WA, PAGE)
    cache_i = cache_kidx_c.transpose(0, 1, 3, 2)

    cos_p, sin_p = _rope_tables(jnp.arange(L, dtype=I32), WA)
    cos_s, sin_s = _rope_tables(past + (jnp.arange(Ts, dtype=I32) % Ls), WA)

    tm_p = _pick(Tp, (1024, 512, 256, 128))
    tm_p = min(tm_p, L)
    tn_main = _pick(NPJ, (512, 256, 128))
    tn_up = _pick(2 * F, (512, 256, 128))
    tkf = _pick(F, (1408, 256, 128))
    tq_a = _pick(L, (256, 128))
    tq_c = _pick(L, (256, 128))
    tk_c = _pick(L, (512, 256, 128))
    C_h = _pick(L, (64,))
    G = _pick(NP, (16, 8, 4, 2, 1))
    n_sel_p = min(TOPK_MAX, L // 4)
    n_sel_s = min(TOPK_MAX, (past + Ls) // 4)

    xp = x_prompt.reshape(Tp, D)
    xs = x_sample.reshape(Ts, D)
    tpos_s = (jnp.arange(Ts, dtype=I32) % Ls).reshape(Ts, 1)

    def pad_rows(a, rows):
        return jnp.pad(a, ((0, 0), (0, rows - a.shape[1]), (0, 0)))

    outs_p = ([], [], [], [], [])
    outs_s = ([], [], [], [], [])
    for l in range(depth):
        w = w_in[l]
        col = lambda o, n: w[:, o:o + n]
        w_main = jnp.concatenate(
            [col(o_qh, 4 * WB), col(o_qa, 3 * WA), col(o_vc, WA), col(o_gt, 3 * D)], axis=1).astype(BF16)
        w_r = jnp.concatenate([col(o_qc, WA), col(o_kc, WA), col(o_qi, WA)] + [col(o_ki, DH)] * N_HEADS, axis=1)
        w_tail = jnp.concatenate([col(o_wi, N_HEADS), jnp.zeros((D, WA - N_HEADS), w.dtype)], axis=1)
        w_rot = jnp.concatenate([_rotate_half_cols(w_r), jnp.zeros((D, WA), w.dtype)], axis=1).astype(BF16)
        w_r = jnp.concatenate([w_r, w_tail], axis=1).astype(BF16)
        wa_b, wb_b, wc_b, wo_b = (t[l].astype(BF16) for t in (w_br_a, w_br_b, w_br_c, w_out))
        wup_b = w_up[l].astype(BF16)
        wd_b = w_down[l].astype(BF16)
        last = l == depth - 1

        vc_col = VC_BLK * WA
        pf, pb, vT = _rms_proj(xp, norm_mix[l], w_main, tm=tm_p, tn=tn_main,
                               t_outs=((vc_col // tn_main, vc_col % tn_main, WA, WA, tk_c, BF16),))
        rf, rb, qcT, qiT, wiT = _rms_proj(
            xp, norm_mix[l], w_r, tm=tm_p, tn=WA, rope_args=(w_rot, cos_p, sin_p), n_rope_tiles=N_ROPE_TILES,
            t_outs=((0, 0, WA, WA, tq_c, BF16), (2, 0, WA, WA, tq_c, BF16), (N_ROPE_TILES, 0, LANES, 8, tq_c, F32)))
        oa = _attn_a_prompt(pb, B, L, tq=tq_a, q_blk=QA_BLK, k_blk=KA_BLK, v_blk=VA_BLK)
        ob, s_new = _hgrn(pf.reshape(B, L, NPJ), lb_all[l], hgrn_gnorm[l], None, C=C_h, c=min(16, C_h), valid=C_h)
        oc = _dsa_prompt(qcT, qiT, wiT, rb, vT, B, L, tq=tq_c, tk=tk_c, n_sel=n_sel_p)
        x1 = _merge(oa, ob.reshape(Tp, WB), oc, pf, xp, wa_b, wb_b, wc_b, wo_b, tm=min(512, tm_p), gate_blk=GATE_BLK)
        u = _rms_proj(x1, norm_ffn[l], wup_b, tm=tm_p, tn=tn_up, emit_bf16=False)
        xp = _ffn_down(u, x1, conv_w[l], conv_b[l], wd_b, tm=min(512, tm_p), tkf=tkf, seq_len=L,
                       g_final=norm_final if last else None)
        kv_off = (KA_BLK * WA)
        outs_p[0].append(pf[:, kv_off:kv_off + 2 * WA].reshape(B, L, 2, N_HEADS, DH))
        outs_p[1].append(jnp.stack([rf[:, WA:2 * WA].reshape(B, L, N_HEADS, DH),
                                    pf[:, VC_BLK * WA:(VC_BLK + 1) * WA].reshape(B, L, N_HEADS, DH)], axis=2))
        outs_p[2].append(rf[:, 3 * WA:3 * WA + DH].reshape(B, L, DH))
        outs_p[3].append(s_new)
        outs_p[4].append(u.reshape(B, L, 2 * F)[:, L - 2:])

        pf, pb = _rms_proj(xs, norm_mix[l], w_main, tm=Ts, tn=tn_main)
        rf, rb = _rms_proj(xs, norm_mix[l], w_r, tm=Ts, tn=WA, rope_args=(w_rot, cos_s, sin_s),
                           n_rope_tiles=N_ROPE_TILES)
        pb3 = pb.reshape(Bs, Ls, NPJ)
        rb3 = rb.reshape(Bs, Ls, NRJ)
        blk = lambda a, k: a[:, :, k * WA:(k + 1) * WA]
        oa8 = _attn_a_sample(pad_rows(blk(pb3, QA_BLK), 8), pad_rows(blk(pb3, KA_BLK), PAGE),
                             pad_rows(blk(pb3, VA_BLK), PAGE), cache_a, l, page_table, G=G)
        ph = jnp.pad(pf[:, :4 * WB].reshape(Bs, Ls, 4 * WB), ((0, 0), (0, 8 - Ls), (0, 0)))
        ob8, s_new = _hgrn(ph, lb_all[l], hgrn_gnorm[l], state_hgrn[l], C=8, c=8, valid=Ls)
        qi = blk(rb3, 2).reshape(Bs, Ls, N_HEADS, DH).transpose(0, 2, 1, 3)
        qi32 = jnp.pad(qi, ((0, 0), (0, 0), (0, 8 - Ls), (0, 0))).reshape(Bs, 4 * 8, DH)
        wi = rf[:, wi_off:wi_off + N_HEADS].reshape(Bs, Ls, N_HEADS).transpose(0, 2, 1) * (N_HEADS ** -0.5)
        w32 = jnp.broadcast_to(jnp.pad(wi, ((0, 0), (0, 0), (0, 8 - Ls))).reshape(Bs, 32, 1), (Bs, 32, LANES))
        kin = pad_rows(rb3[:, :, 3 * WA:3 * WA + DH], PAGE)
        qc8 = pad_rows(blk(rb3, 0), 8)
        head = (jnp.arange(WA) // DH)[None, None, None, :] == jnp.arange(N_HEADS)[None, :, None, None]
        qc32 = (jnp.where(head, qc8[:, None], 0) * jnp.asarray(DH ** -0.5, BF16)).reshape(Bs, 32, WA).astype(BF16)
        oc8 = _dsa_sample(qi32, w32, kin, qc32, pad_rows(blk(rb3, 1), PAGE), pad_rows(blk(pb3, VC_BLK), PAGE),
                          cache_i, cache_c, l, page_table, G=G, n_new=Ls, n_sel=n_sel_s)
        x1 = _merge(oa8[:, :Ls].reshape(Ts, WA), ob8[:, :Ls].reshape(Ts, WB), oc8[:, :Ls].reshape(Ts, WA), pf, xs,
                    wa_b, wb_b, wc_b, wo_b, tm=Ts, gate_blk=GATE_BLK)
        u = _rms_proj(x1, norm_ffn[l], wup_b, tm=Ts, tn=tn_up, emit_bf16=False)
        prev = state_ffn_conv[l]
        u3 = u.reshape(Bs, Ls, 2 * F)
        p1 = jnp.broadcast_to(prev[:, 1:2], (Bs, Ls, 2 * F)).reshape(Ts, 2 * F)
        p2 = jnp.concatenate([prev, jnp.zeros((Bs, Ls - 2, 2 * F), F32)], axis=1).reshape(Ts, 2 * F)
        xs = _ffn_down(u, x1, conv_w[l], conv_b[l], wd_b, tm=Ts, tkf=tkf, seq_len=Ls, prev=(p1, p2, tpos_s),
                       g_final=norm_final if last else None)
        kv_off = (KA_BLK * WA)
        outs_s[0].append(pf[:, kv_off:kv_off + 2 * WA].reshape(Bs, Ls, 2, N_HEADS, DH))
        outs_s[1].append(jnp.stack([rf[:, WA:2 * WA].reshape(Bs, Ls, N_HEADS, DH),
                                    pf[:, VC_BLK * WA:(VC_BLK + 1) * WA].reshape(Bs, Ls, N_HEADS, DH)], axis=2))
        outs_s[2].append(rf[:, 3 * WA:3 * WA + DH].reshape(Bs, Ls, DH))
        outs_s[3].append(s_new)
        outs_s[4].append(jnp.concatenate([prev, u3], axis=1)[:, Ls:])

    st = lambda xs_: jnp.stack(xs_, axis=0)
    return (xp.reshape(B, L, D), xs.reshape(Bs, Ls, D),
            st(outs_p[0]), st(outs_s[0]), st(outs_p[1]), st(outs_s[1]), st(outs_p[2]), st(outs_s[2]),
            st(outs_p[3]), st(outs_s[3]), st(outs_p[4]), st(outs_s[4]))
```

```python
import functools

import jax
import jax.numpy as jnp
from jax import lax
from jax.experimental import pallas as pl
from jax.experimental.pallas import tpu as pltpu

F32 = jnp.float32
BF16 = jnp.bfloat16
I32 = jnp.int32

EPS = 1e-6
NEG_BIG = -1e30
LOG_F_MIN = -30.0
ROPE_THETA = 10000.0
TOPK_MAX = 256
PAGE = 128
N_HEADS = 4
DH = 64
DK = 128
LANES = 128
VMEM_LIMIT = 56 * 1024 * 1024
INT_MIN = -(2 ** 31)
EXP_UNDERFLOW = -110.0


def _cparams(sem):
    return pltpu.CompilerParams(dimension_semantics=sem, vmem_limit_bytes=VMEM_LIMIT)


def _dot(a, b):
    return jnp.dot(a, b, preferred_element_type=F32)


def _dot_nt(a, b):
    return lax.dot_general(a, b, (((1,), (1,)), ((), ())), preferred_element_type=F32)


def _dot_tn(a, b):
    return lax.dot_general(a, b, (((0,), (0,)), ((), ())), preferred_element_type=F32)


def _silu(x):
    return x * jax.nn.sigmoid(x)


def _stack_heads(q, rows):
    head = lax.broadcasted_iota(I32, (rows, N_HEADS * DH), 1) // DH
    return jnp.concatenate([jnp.where(head == h, q, jnp.zeros_like(q)) for h in range(N_HEADS)], axis=0)


def _unstack_heads(acc, rows):
    head = lax.broadcasted_iota(I32, (rows, N_HEADS * DH), 1) // DH
    out = jnp.zeros((rows, N_HEADS * DH), acc.dtype)
    for h in range(N_HEADS):
        out = jnp.where(head == h, acc[h * rows:(h + 1) * rows], out)
    return out


def _proj_body(*refs, rope, emit_bf16, n_rope_tiles, t_outs):
    x_ref, g_ref, w_ref = refs[:3]
    rest = refs[3:]
    if rope:
        wrot_ref, cos_ref, sin_ref = rest[:3]
        rest = rest[3:]
    of_ref = rest[0]
    ob_ref = rest[1] if emit_bf16 else None
    t_refs = rest[(2 if emit_bf16 else 1):-1]
    h_ref = rest[-1]

    @pl.when(pl.program_id(1) == 0)
    def _():
        x = x_ref[...]
        ms = jnp.mean(x * x, axis=-1, keepdims=True)
        h_ref[...] = (x * lax.rsqrt(ms + EPS) * g_ref[...]).astype(BF16)

    h = h_ref[...]
    acc = _dot(h, w_ref[...])
    if rope:
        roped = acc * cos_ref[...] + _dot(h, wrot_ref[...]) * sin_ref[...]
        acc = jnp.where(pl.program_id(1) < n_rope_tiles, roped, acc)
    of_ref[...] = acc
    if emit_bf16:
        ob_ref[...] = acc.astype(BF16)
    for t_ref, (j_tile, c0, width, keep, sub, dtype) in zip(t_refs, t_outs):
        @pl.when(pl.program_id(1) == j_tile)
        def _(t_ref=t_ref, c0=c0, width=width, keep=keep, sub=sub, dtype=dtype):
            for r in range(acc.shape[0] // sub):
                t_ref[r] = acc[r * sub:(r + 1) * sub, c0:c0 + width].T[:keep].astype(dtype)


def _rms_proj(x, g, w, *, tm, tn, rope_args=None, emit_bf16=True, n_rope_tiles=0, t_outs=()):
    T, D = x.shape
    N = w.shape[1]
    rope = rope_args is not None
    in_specs = [pl.BlockSpec((tm, D), lambda i, j: (i, 0)),
                pl.BlockSpec((1, D), lambda i, j: (0, 0)),
                pl.BlockSpec((D, tn), lambda i, j: (0, j))]
    args = [x, g.reshape(1, D), w]
    if rope:
        w_rot, cos, sin = rope_args
        nblk = cos.shape[0] // tm
        in_specs += [pl.BlockSpec((D, tn), lambda i, j: (0, j)),
                     pl.BlockSpec((tm, tn), lambda i, j: (i % nblk, 0)),
                     pl.BlockSpec((tm, tn), lambda i, j: (i % nblk, 0))]
        args += [w_rot, cos, sin]
    out_shape = [jax.ShapeDtypeStruct((T, N), F32)]
    out_specs = [pl.BlockSpec((tm, tn), lambda i, j: (i, j))]
    if emit_bf16:
        out_shape.append(jax.ShapeDtypeStruct((T, N), BF16))
        out_specs.append(pl.BlockSpec((tm, tn), lambda i, j: (i, j)))
    for (_, _, _, keep, sub, dtype) in t_outs:
        out_shape.append(jax.ShapeDtypeStruct((T // sub, keep, sub), dtype))
        out_specs.append(pl.BlockSpec((tm // sub, keep, sub), lambda i, j: (i, 0, 0)))
    outs = pl.pallas_call(
        functools.partial(_proj_body, rope=rope, emit_bf16=emit_bf16, n_rope_tiles=n_rope_tiles, t_outs=tuple(t_outs)),
        grid=(T // tm, N // tn),
        in_specs=in_specs, out_specs=out_specs, out_shape=out_shape,
        scratch_shapes=[pltpu.VMEM((tm, D), BF16)],
        compiler_params=_cparams(("parallel", "arbitrary")),
        name="rms_proj_rope" if rope else "rms_proj",
    )(*args)
    return outs if (emit_bf16 or t_outs) else outs[0]


def _suffix_matrix():
    j = lax.broadcasted_iota(I32, (2 * LANES, LANES), 0) & (LANES - 1)
    s = lax.broadcasted_iota(I32, (2 * LANES, LANES), 1)
    return jnp.where(j > s, 1.0, 0.0).astype(BF16)


def _log_one_minus_beta(z):
    return -(jnp.maximum(z, 0.0) + jnp.log(1.0 + jnp.exp(-jnp.abs(z))))


def _sb_update(z, carry, suffix, vis):
    M, tk = z.shape
    ls = _log_one_minus_beta(z)
    if vis is not None:
        ls = jnp.where(vis, ls, 0.0)
    hi = ls.astype(BF16)
    lo = (ls - hi.astype(F32)).astype(BF16)
    lz = z + ls
    n = tk // LANES
    outs = [None] * n
    for g in reversed(range(n)):
        sl = slice(g * LANES, (g + 1) * LANES)
        between = _dot(jnp.concatenate([hi[:, sl], lo[:, sl]], axis=1), suffix)
        e = lz[:, sl] + between + carry
        if vis is not None:
            e = jnp.where(vis[:, sl], e, NEG_BIG)
        outs[g] = jnp.exp(e)
        carry = carry + jnp.sum(ls[:, sl], axis=1, keepdims=True)
    a = outs[0] if n == 1 else jnp.concatenate(outs, axis=1)
    return a.astype(BF16), carry


def _attn_a_prompt_body(q_ref, k_ref, v_ref, o_ref, *, tq):
    i = pl.program_id(1)
    M = N_HEADS * tq
    qs = _stack_heads(q_ref[...], tq) * jnp.asarray(DH ** -0.5, BF16)
    suffix = _suffix_matrix()

    def block(k0, carry, acc, vis):
        kb = k_ref[pl.ds(k0, tq), :]
        vb = v_ref[pl.ds(k0, tq), :]
        a, carry = _sb_update(_dot_nt(qs, kb), carry, suffix, vis)
        return carry, acc + _dot(a, vb)

    row_t = lax.broadcasted_iota(I32, (M, tq), 0) & (tq - 1)
    col_s = lax.broadcasted_iota(I32, (M, tq), 1)
    carry, acc = block(pl.multiple_of(i * tq, tq), jnp.zeros((M, 1), F32),
                       jnp.zeros((M, N_HEADS * DH), F32), col_s < row_t)

    def alive(carry):
        return (jnp.max(carry) > EXP_UNDERFLOW).astype(I32)

    def cond(c):
        return (c[0] < i) & (c[1] > 0)

    def body(c):
        step = c[0]
        k0 = pl.multiple_of((i - 1 - step) * tq, tq)
        carry, acc = block(k0, c[2], c[3], None)
        return step + 1, alive(carry), carry, acc

    _, _, carry, acc = lax.while_loop(cond, body, (jnp.int32(0), alive(carry), carry, acc))
    o_ref[...] = _unstack_heads(acc, tq).astype(o_ref.dtype)


def _attn_a_prompt(pb, B, L, *, tq, q_blk, k_blk, v_blk):
    W = N_HEADS * DH
    nq = L // tq
    return pl.pallas_call(
        functools.partial(_attn_a_prompt_body, tq=tq),
        grid=(B, nq),
        in_specs=[pl.BlockSpec((tq, W), lambda b, i: (b * nq + i, q_blk)),
                  pl.BlockSpec((L, W), lambda b, i: (b, k_blk)),
                  pl.BlockSpec((L, W), lambda b, i: (b, v_blk))],
        out_specs=pl.BlockSpec((tq, W), lambda b, i: (b * nq + i, 0)),
        out_shape=jax.ShapeDtypeStruct((B * L, W), BF16),
        compiler_params=_cparams(("parallel", "arbitrary")),
        name="attn_a_prompt",
    )(pb, pb, pb)


def _attn_a_sample_body(pt_ref, q_ref, kn_ref, vn_ref, *rest, G):
    pages = rest[:G]
    o_ref, carry_ref, acc_ref = rest[G:]
    p = pl.program_id(1)
    W = N_HEADS * DH
    R = 8
    M = N_HEADS * R
    qs = _stack_heads(q_ref[0], R) * jnp.asarray(DH ** -0.5, BF16)
    suffix = _suffix_matrix()

    @pl.when(p == 0)
    def _():
        row_t = lax.broadcasted_iota(I32, (M, PAGE), 0) & (R - 1)
        col_s = lax.broadcasted_iota(I32, (M, PAGE), 1)
        a, carry = _sb_update(_dot_nt(qs, kn_ref[0]), jnp.zeros((M, 1), F32), suffix, col_s < row_t)
        carry_ref[...] = jnp.broadcast_to(carry, carry_ref.shape)
        acc_ref[...] = _dot(a, vn_ref[0])

    @pl.when(jnp.max(carry_ref[...]) > EXP_UNDERFLOW)
    def _():
        kcat = jnp.concatenate([pages[g][:W, :] for g in reversed(range(G))], axis=1).astype(BF16)
        vcat = jnp.concatenate([pages[g][W:, :] for g in reversed(range(G))], axis=1).astype(BF16)
        a, carry = _sb_update(_dot(qs, kcat), carry_ref[:, 0:1], suffix, None)
        carry_ref[...] = jnp.broadcast_to(carry, carry_ref.shape)
        acc_ref[...] += _dot_nt(a, vcat)

    @pl.when(p == pl.num_programs(1) - 1)
    def _():
        o_ref[0] = _unstack_heads(acc_ref[...], R).astype(o_ref.dtype)


def _attn_a_sample(q8, kn, vn, cache, layer, page_table, *, G):
    Bs = q8.shape[0]
    NP = page_table.shape[1]
    W = N_HEADS * DH

    def page_spec(g):
        return pl.BlockSpec((None, None, 2 * W, PAGE),
                            lambda b, p, pt: (layer, pt[b, NP - 1 - (p * G + g)], 0, 0))

    grid_spec = pltpu.PrefetchScalarGridSpec(
        num_scalar_prefetch=1,
        grid=(Bs, NP // G),
        in_specs=[pl.BlockSpec((1, 8, W), lambda b, p, pt: (b, 0, 0)),
                  pl.BlockSpec((1, PAGE, W), lambda b, p, pt: (b, 0, 0)),
                  pl.BlockSpec((1, PAGE, W), lambda b, p, pt: (b, 0, 0))] + [page_spec(g) for g in range(G)],
        out_specs=pl.BlockSpec((1, 8, W), lambda b, p, pt: (b, 0, 0)),
        scratch_shapes=[pltpu.VMEM((N_HEADS * 8, LANES), F32), pltpu.VMEM((N_HEADS * 8, W), F32)],
    )
    return pl.pallas_call(
        functools.partial(_attn_a_sample_body, G=G),
        grid_spec=grid_spec,
        out_shape=jax.ShapeDtypeStruct((Bs, 8, W), BF16),
        compiler_params=_cparams(("parallel", "arbitrary")),
        name="attn_a_sample",
    )(page_table, q8, kn, vn, *([cache] * G))


def _cumsum_rows(x):
    C = x.shape[0]
    row = lax.broadcasted_iota(I32, x.shape, 0)
    sh = 1
    while sh < C:
        x = x + jnp.where(row >= sh, pltpu.roll(x, sh, axis=0), 0.0)
        sh *= 2
    return x


def _hgrn_body(*refs, C, c, valid, has_s0):
    q_ref, f_ref, i_ref, g_ref, lb_ref, gn_ref = refs[:6]
    rest = refs[6:]
    if has_s0:
        s0_ref = rest[0]
        rest = rest[1:]
    o_ref, sout_ref, st_ref = rest
    ci = pl.program_id(1)

    @pl.when(ci == 0)
    def _():
        for h in range(N_HEADS):
            st_ref[h] = s0_ref[0, h].T if has_s0 else jnp.zeros((DK, DK), F32)

    row = lax.broadcasted_iota(I32, (C, DK), 0)
    rowc = lax.broadcasted_iota(I32, (c, 1), 0)
    for h in range(N_HEADS):
        hs = slice(h * DK, (h + 1) * DK)
        kk = (1.0 - lb_ref[:, hs]) * jax.nn.sigmoid(-f_ref[0, :, hs])
        lg = jnp.maximum(jnp.log1p(-kk), LOG_F_MIN)
        if valid < C:
            kk = jnp.where(row < valid, kk, 0.0)
            lg = jnp.where(row < valid, lg, 0.0)
        qq = _silu(q_ref[0, :, hs])
        vv = i_ref[0, :, hs]
        cum = _cumsum_rows(lg)
        st = st_ref[h]
        o = _dot_nt((qq * jnp.exp(cum)).astype(BF16), st.astype(BF16))
        parts = []
        for blk in range(C // c):
            r0 = blk * c
            q_b = qq[r0:r0 + c]
            cum_b = cum[r0:r0 + c]
            o_b = jnp.zeros((c, DK), F32)
            if blk > 0:
                base = cum[r0 - 1:r0]
                qt = q_b * jnp.exp(cum_b - base)
                kt = kk[:r0] * jnp.exp(base - cum[:r0])
                sc = _dot_nt(qt.astype(BF16), kt.astype(BF16))
                o_b = o_b + _dot(sc.astype(BF16), vv[:r0].astype(BF16))
            for s in range(c):
                r = r0 + s
                d = jnp.minimum(cum_b - cum[r:r + 1], 0.0)
                w = jnp.sum(q_b * kk[r:r + 1] * jnp.exp(d), axis=1, keepdims=True)
                o_b = o_b + jnp.where(rowc >= s, w, 0.0) * vv[r:r + 1]
            parts.append(o_b)
        o = o + (parts[0] if len(parts) == 1 else jnp.concatenate(parts, axis=0))
        last = cum[C - 1:C]
        kd = kk * jnp.exp(last - cum)
        st_ref[h] = st * jnp.exp(last) + _dot_tn(vv.astype(BF16), kd.astype(BF16))
        ms = jnp.mean(o * o, axis=1, keepdims=True)
        y = o * lax.rsqrt(ms + EPS) * gn_ref[...] * _silu(g_ref[0, :, hs])
        o_ref[0, :, hs] = y.astype(o_ref.dtype)

    @pl.when(ci == pl.num_programs(1) - 1)
    def _():
        for h in range(N_HEADS):
            sout_ref[0, h] = st_ref[h].T


def _hgrn(p3, lb, gn, s0, *, C, c, valid, col0=0):
    B, L, _ = p3.shape
    W = N_HEADS * DK
    has_s0 = s0 is not None
    in_specs = [pl.BlockSpec((1, C, W), functools.partial(lambda b, ci, k: (b, ci, col0 + k), k=k)) for k in range(4)]
    in_specs += [pl.BlockSpec((1, W), lambda b, ci: (0, 0)), pl.BlockSpec((1, DK), lambda b, ci: (0, 0))]
    args = [p3, p3, p3, p3, lb.reshape(1, W), gn.reshape(1, DK)]
    if has_s0:
        in_specs.append(pl.BlockSpec((1, N_HEADS, DK, DK), lambda b, ci: (b, 0, 0, 0)))
        args.append(s0)
    return pl.pallas_call(
        functools.partial(_hgrn_body, C=C, c=c, valid=valid, has_s0=has_s0),
        grid=(B, L // C),
        in_specs=in_specs,
        out_specs=[pl.BlockSpec((1, C, W), lambda b, ci: (b, ci, 0)),
                   pl.BlockSpec((1, N_HEADS, DK, DK), lambda b, ci: (b, 0, 0, 0))],
        out_shape=[jax.ShapeDtypeStruct((B, L, W), BF16), jax.ShapeDtypeStruct((B, N_HEADS, DK, DK), F32)],
        scratch_shapes=[pltpu.VMEM((N_HEADS, DK, DK), F32)],
        compiler_params=_cparams(("parallel", "arbitrary")),
        name="hgrn2",
    )(*args)


def _sort_key(score):
    b = pltpu.bitcast(score, I32)
    return jnp.where(b < 0, b ^ jnp.int32(0x7FFFFFFF), b)


def _prefix_matrix(n):
    j = lax.broadcasted_iota(I32, (n, n + LANES), 0)
    s = lax.broadcasted_iota(I32, (n, n + LANES), 1)
    return jnp.where((s >= n) | (j <= s), 1.0, 0.0).astype(BF16)


def _kth_largest_key(count_ge, rows, n_sel):
    def body(t, T):
        cand = T + jnp.left_shift(jnp.int32(1), 31 - t)
        return jnp.where(count_ge(cand) >= n_sel, cand, T)
    return lax.fori_loop(0, 32, body, jnp.full((rows, 1), INT_MIN, I32))


def _dsa_prompt_body(qcT_ref, qiT_ref, wiT_ref, kc_ref, ki_ref, vT_ref, lt_ref, o_ref, keys_ref, kh_ref, kl_ref,
                     *, tq, tk, n_sel):
    i = pl.program_id(1)
    nkb = ((i + 1) * tq + tk - 1) // tk
    M = N_HEADS * tq
    W = N_HEADS * DH
    I16 = jnp.int16
    kpos0 = lax.broadcasted_iota(I32, (tk, tq), 0)
    qpos = i * tq + lax.broadcasted_iota(I32, (tk, tq), 1)
    row_head = lax.broadcasted_iota(I32, (W, tq), 0) // DH

    def stack_lanes(qT):
        return jnp.concatenate([jnp.where(row_head == h, qT, jnp.zeros_like(qT)) for h in range(N_HEADS)], axis=1)

    qisT = stack_lanes(qiT_ref[0])
    wi = wiT_ref[0]
    w_row = jnp.concatenate([wi[j:j + 1, :] for j in range(N_HEADS)], axis=1) * (N_HEADS ** -0.5) * (DH ** -0.5)

    def score_body(j, _):
        k0 = pl.multiple_of(j * tk, tk)
        d = jnp.maximum(_dot(ki_ref[pl.ds(k0, tk), :], qisT), 0.0) * w_row
        score = d[:, 0:tq]
        for jh in range(1, N_HEADS):
            score = score + d[:, jh * tq:(jh + 1) * tq]
        score = jnp.where(kpos0 + k0 <= qpos, score + 0.0, NEG_BIG)
        key = _sort_key(score)
        keys_ref[j] = key
        kh_ref[j] = (key >> 16).astype(I16)
        kl_ref[j] = ((key & 0xFFFF) - 32768).astype(I16)
        return 0

    lax.fori_loop(0, nkb, score_body, 0)

    def count16(ref, cand):
        c16 = cand.astype(I16)

        def body(j, acc):
            m = jnp.where(ref[j] >= c16, jnp.ones((), BF16), jnp.zeros((), BF16))
            parts = [m[16 * r:16 * (r + 1)] for r in range(tk // 16)]
            while len(parts) > 1:
                parts = [parts[a] + parts[a + 1] for a in range(0, len(parts) - 1, 2)] + parts[len(parts) & ~1:]
            return acc + parts[0].astype(F32)

        acc = lax.fori_loop(0, nkb, body, jnp.zeros((16, tq), F32))
        return jnp.sum(acc, axis=0, keepdims=True)

    def kth16(ref, target):
        def body(t, T):
            cand = T + jnp.left_shift(jnp.int32(1), 15 - t)
            return jnp.where(count16(ref, cand) >= target, cand, T)
        return lax.fori_loop(0, 16, body, jnp.full((1, tq), -32768, I32))

    def count_above(ref, T):
        return jnp.where(T >= 32767, 0.0, count16(ref, jnp.minimum(T + 1, 32767)))

    TH = kth16(kh_ref, float(n_sel))
    need = n_sel - count_above(kh_ref, TH)
    th16 = TH.astype(I16)

    def low_body(j, _):
        kl_ref[j] = jnp.where(kh_ref[j] == th16, kl_ref[j], jnp.full((), -32768, I16))
        return 0

    lax.fori_loop(0, nkb, low_body, 0)
    TL = kth16(kl_ref, need)
    room = need - count_above(kl_ref, TL)
    T = TH * 65536 + (TL + 32768)

    qcsT = stack_lanes(qcT_ref[0]) * jnp.asarray(DH ** -0.5, BF16)
    lt = lt_ref[...]

    def att_body(j, c):
        m, l, accT, eq_before = c
        k0 = pl.multiple_of(j * tk, tk)
        key = keys_ref[j]
        eq = key == T
        rank = _dot(lt, jnp.where(eq, 1.0, 0.0).astype(BF16)) + eq_before
        sel = (key > T) | (eq & (rank <= room))
        bias = jnp.where(sel & (kpos0 + k0 <= qpos), 0.0, NEG_BIG)
        s = _dot(kc_ref[pl.ds(k0, tk), :], qcsT) + jnp.concatenate([bias] * N_HEADS, axis=1)
        m_new = jnp.maximum(m, jnp.max(s, axis=0, keepdims=True))
        alpha = jnp.exp(m - m_new)
        p = jnp.exp(s - m_new)
        l = alpha * l + jnp.sum(p, axis=0, keepdims=True)
        accT = alpha * accT + _dot(vT_ref[j], p.astype(BF16))
        return m_new, l, accT, rank[tk - 1:tk, :]

    init = (jnp.full((1, M), NEG_BIG, F32), jnp.zeros((1, M), F32), jnp.zeros((W, M), F32), jnp.zeros((1, tq), F32))
    m, l, accT, _ = lax.fori_loop(0, nkb, att_body, init)
    outT = accT / l
    oT = jnp.zeros((W, tq), F32)
    for h in range(N_HEADS):
        oT = jnp.where(row_head == h, outT[:, h * tq:(h + 1) * tq], oT)
    o_ref[...] = oT.T.astype(o_ref.dtype)


def _dsa_prompt(qcT, qiT, wiT, rb, vT, B, L, *, tq, tk, n_sel):
    W = N_HEADS * DH
    nq = L // tq
    assert tk // 16 <= 256
    lt = jnp.where(jnp.arange(tk)[None, :] <= jnp.arange(tk)[:, None], 1.0, 0.0).astype(BF16)
    return pl.pallas_call(
        functools.partial(_dsa_prompt_body, tq=tq, tk=tk, n_sel=n_sel),
        grid=(B, nq),
        in_specs=[pl.BlockSpec((1, W, tq), lambda b, i: (b * nq + i, 0, 0)),
                  pl.BlockSpec((1, W, tq), lambda b, i: (b * nq + i, 0, 0)),
                  pl.BlockSpec((1, 8, tq), lambda b, i: (b * nq + i, 0, 0)),
                  pl.BlockSpec((L, W), lambda b, i: (b, 1)),
                  pl.BlockSpec((L, W), lambda b, i: (b, 3)),
                  pl.BlockSpec((L // tk, W, tk), lambda b, i: (b, 0, 0)),
                  pl.BlockSpec((tk, tk), lambda b, i: (0, 0))],
        out_specs=pl.BlockSpec((tq, W), lambda b, i: (b * nq + i, 0)),
        out_shape=jax.ShapeDtypeStruct((B * L, W), BF16),
        scratch_shapes=[pltpu.VMEM((L // tk, tk, tq), I32), pltpu.VMEM((L // tk, tk, tq), jnp.int16),
                        pltpu.VMEM((L // tk, tk, tq), jnp.int16)],
        compiler_params=_cparams(("parallel", "arbitrary")),
        name="dsa_prompt",
    )(qcT, qiT, wiT, rb, rb, vT, lt)


def _dsa_s_score_body(pt_ref, qi_ref, w_ref, kn_ref, *rest, G, n_new):
    pages = rest[:G]
    o_ref = rest[G]
    p = pl.program_id(1)
    R = 8
    qi = qi_ref[0]
    w = w_ref[0]

    def score(dots):
        d = jnp.maximum(dots * (DH ** -0.5), 0.0) * w
        return (d[0:R] + d[R:2 * R]) + (d[2 * R:3 * R] + d[3 * R:4 * R]) + 0.0

    @pl.when(p < pl.num_programs(1) - 1)
    def _():
        for g in range(G):
            o_ref[0, :, g * PAGE:(g + 1) * PAGE] = score(_dot(qi, pages[g][...].astype(BF16)))

    @pl.when(p == pl.num_programs(1) - 1)
    def _():
        t = lax.broadcasted_iota(I32, (R, PAGE), 0)
        s = lax.broadcasted_iota(I32, (R, PAGE), 1)
        o_ref[0, :, 0:PAGE] = jnp.where((s <= t) & (s < n_new), score(_dot_nt(qi, kn_ref[0])), NEG_BIG)
        if G > 1:
            o_ref[0, :, PAGE:] = jnp.full((R, (G - 1) * PAGE), NEG_BIG, F32)


def _dsa_s_select_body(s_ref, o_ref, *, n_sel):
    R = 8
    key = _sort_key(s_ref[0])
    W = key.shape[1]

    def count_ge(cand):
        m = jnp.where(key >= cand, 1.0, 0.0)
        parts = [m[:, g * LANES:(g + 1) * LANES] for g in range(W // LANES)]
        while len(parts) > 1:
            parts = [parts[a] + parts[a + 1] for a in range(0, len(parts) - 1, 2)] + parts[len(parts) & ~1:]
        return jnp.sum(parts[0], axis=1, keepdims=True)

    T = _kth_largest_key(count_ge, R, n_sel)
    room = n_sel - count_ge(T + 1)
    eq = key == T
    gt = key > T
    nt = W // LANES
    eqf = jnp.where(eq, 1.0, 0.0)
    stacked = jnp.concatenate([eqf[:, g * LANES:(g + 1) * LANES] for g in range(nt)], axis=0).astype(BF16)
    pc = _dot(stacked, _prefix_matrix(LANES))
    before = jnp.zeros((R, LANES), F32)
    tiles = []
    for g in range(nt):
        sl = slice(g * LANES, (g + 1) * LANES)
        rank = pc[g * R:(g + 1) * R, :LANES] + before
        tiles.append(jnp.where(gt[:, sl], 1.0, jnp.where(eq[:, sl] & (rank <= room), 1.0, 0.0)))
        before = before + pc[g * R:(g + 1) * R, LANES:]
    o_ref[0] = jnp.concatenate(tiles, axis=1)


def _dsa_s_attend_body(pt_ref, q_ref, kn_ref, vn_ref, mask_ref, maskn_ref, *rest, G, n_new):
    pages = rest[:G]
    o_ref, m_ref, l_ref, acc_ref = rest[G:]
    p = pl.program_id(1)
    W = N_HEADS * DH
    R = 8
    qs = q_ref[0]

    def update(s, v, m, l, acc, v_feature_major=False):
        m_new = jnp.maximum(m, jnp.max(s, axis=1, keepdims=True))
        alpha = jnp.exp(m - m_new)
        pr = jnp.exp(s - m_new)
        pv = _dot_nt(pr.astype(BF16), v) if v_feature_major else _dot(pr.astype(BF16), v)
        return m_new, alpha * l + jnp.sum(pr, axis=1, keepdims=True), alpha * acc + pv

    @pl.when(p == 0)
    def _():
        t = lax.broadcasted_iota(I32, (R, PAGE), 0)
        sidx = lax.broadcasted_iota(I32, (R, PAGE), 1)
        sel = jnp.where((sidx <= t) & (sidx < n_new), maskn_ref[0, :, 0:PAGE], 0.0)
        s = jnp.where(jnp.concatenate([sel] * N_HEADS, axis=0) > 0.5, _dot_nt(qs, kn_ref[0]), NEG_BIG)
        m, l, acc = update(s, vn_ref[0], jnp.full((N_HEADS * R, 1), NEG_BIG, F32),
                           jnp.zeros((N_HEADS * R, 1), F32), jnp.zeros((N_HEADS * R, W), F32))
        m_ref[...] = jnp.broadcast_to(m, m_ref.shape)
        l_ref[...] = jnp.broadcast_to(l, l_ref.shape)
        acc_ref[...] = acc

    kcat = jnp.concatenate([pages[g][:W, :] for g in range(G)], axis=1).astype(BF16)
    vcat = jnp.concatenate([pages[g][W:, :] for g in range(G)], axis=1).astype(BF16)
    sel = jnp.concatenate([mask_ref[0]] * N_HEADS, axis=0)
    s = jnp.where(sel > 0.5, _dot(qs, kcat), NEG_BIG)
    m, l, acc = update(s, vcat, m_ref[:, 0:1], l_ref[:, 0:1], acc_ref[...], v_feature_major=True)
    m_ref[...] = jnp.broadcast_to(m, m_ref.shape)
    l_ref[...] = jnp.broadcast_to(l, l_ref.shape)
    acc_ref[...] = acc

    @pl.when(p == pl.num_programs(1) - 1)
    def _():
        o_ref[0] = _unstack_heads(acc_ref[...] / l_ref[:, 0:1], R).astype(o_ref.dtype)


def _dsa_sample(qi32, w32, kin, qc32, kcn, vcn, cache_kidx, cache_kv, layer, page_table, *, G, n_new, n_sel):
    Bs = qi32.shape[0]
    NP = page_table.shape[1]
    W = N_HEADS * DH
    nstep = NP // G
    width = (nstep + 1) * G * PAGE

    def kidx_spec(g):
        return pl.BlockSpec((None, None, DH, PAGE),
                            lambda b, p, pt: (layer, pt[b, jnp.minimum(p * G + g, NP - 1)], 0, 0))

    scores = pl.pallas_call(
        functools.partial(_dsa_s_score_body, G=G, n_new=n_new),
        grid_spec=pltpu.PrefetchScalarGridSpec(
            num_scalar_prefetch=1, grid=(Bs, nstep + 1),
            in_specs=[pl.BlockSpec((1, 32, DH), lambda b, p, pt: (b, 0, 0)),
                      pl.BlockSpec((1, 32, LANES), lambda b, p, pt: (b, 0, 0)),
                      pl.BlockSpec((1, PAGE, DH), lambda b, p, pt: (b, 0, 0))] + [kidx_spec(g) for g in range(G)],
            out_specs=pl.BlockSpec((1, 8, G * PAGE), lambda b, p, pt: (b, 0, p))),
        out_shape=jax.ShapeDtypeStruct((Bs, 8, width), F32),
        compiler_params=_cparams(("parallel", "arbitrary")),
        name="dsa_sample_score",
    )(page_table, qi32, w32, kin, *([cache_kidx] * G))

    mask = pl.pallas_call(
        functools.partial(_dsa_s_select_body, n_sel=n_sel),
        grid=(Bs,),
        in_specs=[pl.BlockSpec((1, 8, width), lambda b: (b, 0, 0))],
        out_specs=pl.BlockSpec((1, 8, width), lambda b: (b, 0, 0)),
        out_shape=jax.ShapeDtypeStruct((Bs, 8, width), F32),
        compiler_params=_cparams(("parallel",)),
        name="dsa_sample_select",
    )(scores)

    def kv_spec(g):
        return pl.BlockSpec((None, None, 2 * W, PAGE), lambda b, p, pt: (layer, pt[b, p * G + g], 0, 0))

    return pl.pallas_call(
        functools.partial(_dsa_s_attend_body, G=G, n_new=n_new),
        grid_spec=pltpu.PrefetchScalarGridSpec(
            num_scalar_prefetch=1, grid=(Bs, nstep),
            in_specs=[pl.BlockSpec((1, 32, W), lambda b, p, pt: (b, 0, 0)),
                      pl.BlockSpec((1, PAGE, W), lambda b, p, pt: (b, 0, 0)),
                      pl.BlockSpec((1, PAGE, W), lambda b, p, pt: (b, 0, 0)),
                      pl.BlockSpec((1, 8, G * PAGE), lambda b, p, pt: (b, 0, p)),
                      pl.BlockSpec((1, 8, G * PAGE), lambda b, p, pt: (b, 0, nstep))] + [kv_spec(g) for g in range(G)],
            out_specs=pl.BlockSpec((1, 8, W), lambda b, p, pt: (b, 0, 0)),
            scratch_shapes=[pltpu.VMEM((32, LANES), F32), pltpu.VMEM((32, LANES), F32), pltpu.VMEM((32, W), F32)]),
        out_shape=jax.ShapeDtypeStruct((Bs, 8, W), BF16),
        compiler_params=_cparams(("parallel", "arbitrary")),
        name="dsa_sample_attend",
    )(page_table, qc32, kcn, vcn, mask, mask, *([cache_kv] * G))


def _merge_body(oa_ref, ob_ref, oc_ref, g0_ref, g1_ref, g2_ref, x_ref, wa_ref, wb_ref, wc_ref, wo_ref, o_ref):
    merged = (jax.nn.sigmoid(g0_ref[...]) * _dot(oa_ref[...], wa_ref[...])
              + jax.nn.sigmoid(g1_ref[...]) * _dot(ob_ref[...], wb_ref[...])
              + jax.nn.sigmoid(g2_ref[...]) * _dot(oc_ref[...], wc_ref[...]))
    o_ref[...] = x_ref[...] + _dot(merged.astype(BF16), wo_ref[...])


def _merge(oa, ob, oc, pf, x, wa, wb, wc, wo, *, tm, gate_blk):
    T, D = x.shape
    row = lambda i: (i, 0)
    const = lambda i: (0, 0)
    return pl.pallas_call(
        _merge_body,
        grid=(T // tm,),
        in_specs=[pl.BlockSpec((tm, oa.shape[1]), row), pl.BlockSpec((tm, ob.shape[1]), row),
                  pl.BlockSpec((tm, oc.shape[1]), row),
                  pl.BlockSpec((tm, D), lambda i: (i, gate_blk)), pl.BlockSpec((tm, D), lambda i: (i, gate_blk + 1)),
                  pl.BlockSpec((tm, D), lambda i: (i, gate_blk + 2)), pl.BlockSpec((tm, D), row),
                  pl.BlockSpec(wa.shape, const), pl.BlockSpec(wb.shape, const),
                  pl.BlockSpec(wc.shape, const), pl.BlockSpec(wo.shape, const)],
        out_specs=pl.BlockSpec((tm, D), row),
        out_shape=jax.ShapeDtypeStruct((T, D), F32),
        compiler_params=_cparams(("parallel",)),
        name="merge",
    )(oa, ob, oc, pf, pf, pf, x, wa, wb, wc, wo)


def _ffn_down_body(*refs, tm, seq_tiles, expanded, final_norm):
    ua_ref, ub_ref, ha_ref, hb_ref = refs[:4]
    rest = refs[4:]
    if expanded:
        ha2_ref, hb2_ref, t_ref = rest[:3]
        rest = rest[3:]
    cwa_ref, cwb_ref, cba_ref, cbb_ref, wd_ref, x_ref = rest[:6]
    rest = rest[6:]
    if final_norm:
        gf_ref = rest[0]
        rest = rest[1:]
    o_ref, acc_ref = rest
    i = pl.program_id(0)
    k = pl.program_id(1)

    @pl.when(k == 0)
    def _():
        acc_ref[...] = jnp.zeros_like(acc_ref)

    def conv(u_ref, h_ref, h2_ref, cw_ref, cb_ref):
        u = u_ref[...]
        row = lax.broadcasted_iota(I32, u.shape, 0)
        r1 = pltpu.roll(u, 1, axis=0)
        r2 = pltpu.roll(u, 2, axis=0)
        if expanded:
            t = t_ref[...]
            u1 = jnp.where(t >= 1, r1, h_ref[...])
            u2 = jnp.where(t >= 2, r2, h2_ref[...])
        else:
            h = h_ref[...]
            h = jnp.where(i % seq_tiles == 0, jnp.zeros_like(h), h)
            u1 = jnp.where(row == 0, h[7:8], r1)
            u2 = jnp.where(row == 0, h[6:7], jnp.where(row == 1, h[7:8], r2))
        cw = cw_ref[...]
        return cb_ref[...] + cw[0:1] * u2 + cw[1:2] * u1 + cw[2:3] * u

    a = conv(ua_ref, ha_ref, ha2_ref if expanded else None, cwa_ref, cba_ref)
    b = conv(ub_ref, hb_ref, hb2_ref if expanded else None, cwb_ref, cbb_ref)
    acc_ref[...] += _dot((_silu(a) * b).astype(BF16), wd_ref[...])

    @pl.when(k == pl.num_programs(1) - 1)
    def _():
        y = x_ref[...] + acc_ref[...]
        if final_norm:
            ms = jnp.mean(y * y, axis=-1, keepdims=True)
            y = y * lax.rsqrt(ms + EPS) * gf_ref[...]
        o_ref[...] = y


def _ffn_down(u, x, cw, cb, wd, *, tm, tkf, seq_len, prev=None, g_final=None):
    T, D = x.shape
    F = wd.shape[0]
    nk = F // tkf
    expanded = prev is not None
    final_norm = g_final is not None
    seq_tiles = max(seq_len // tm, 1)
    hb8 = tm // 8
    ua = pl.BlockSpec((tm, tkf), lambda i, k: (i, k))
    ub = pl.BlockSpec((tm, tkf), lambda i, k: (i, nk + k))
    in_specs = [ua, ub]
    args = [u, u]
    if expanded:
        p1, p2, tpos = prev
        in_specs += [ua, ub, ua, ub, pl.BlockSpec((tm, 1), lambda i, k: (i, 0))]
        args += [p1, p1, p2, p2, tpos]
    else:
        in_specs += [pl.BlockSpec((8, tkf), lambda i, k: (jnp.maximum(i * hb8 - 1, 0), k)),
                     pl.BlockSpec((8, tkf), lambda i, k: (jnp.maximum(i * hb8 - 1, 0), nk + k))]
        args += [u, u]
    in_specs += [pl.BlockSpec((3, tkf), lambda i, k: (0, k)), pl.BlockSpec((3, tkf), lambda i, k: (0, nk + k)),
                 pl.BlockSpec((1, tkf), lambda i, k: (0, k)), pl.BlockSpec((1, tkf), lambda i, k: (0, nk + k)),
                 pl.BlockSpec((tkf, D), lambda i, k: (k, 0)), pl.BlockSpec((tm, D), lambda i, k: (i, 0))]
    args += [cw, cw, cb.reshape(1, -1), cb.reshape(1, -1), wd, x]
    if final_norm:
        in_specs.append(pl.BlockSpec((1, D), lambda i, k: (0, 0)))
        args.append(g_final.reshape(1, D))
    return pl.pallas_call(
        functools.partial(_ffn_down_body, tm=tm, seq_tiles=seq_tiles, expanded=expanded, final_norm=final_norm),
        grid=(T // tm, nk),
        in_specs=in_specs,
        out_specs=pl.BlockSpec((tm, D), lambda i, k: (i, 0)),
        out_shape=jax.ShapeDtypeStruct((T, D), F32),
        scratch_shapes=[pltpu.VMEM((tm, D), F32)],
        compiler_params=_cparams(("parallel", "arbitrary")),
        name="ffn_down",
    )(*args)


def _rotate_half_cols(w):
    D, N = w.shape
    w4 = w.reshape(D, N // DH, 2, DH // 2)
    return jnp.concatenate([-w4[:, :, 1], w4[:, :, 0]], axis=-1).reshape(D, N)


def _rope_tables(pos, width):
    half = DH // 2
    inv_freq = ROPE_THETA ** (-jnp.arange(half, dtype=F32) / half)
    ang = pos.astype(F32)[:, None] * inv_freq[None, :]
    reps = width // half
    return jnp.tile(jnp.cos(ang), (1, reps)), jnp.tile(jnp.sin(ang), (1, reps))


def _pick(n, prefs):
    for t in prefs:
        if n % t == 0:
            return t
    return n


def kernel(x_prompt, x_sample, cache_kv_a, cache_kv_c, cache_kidx_c, state_hgrn, state_ffn_conv, page_table,
           norm_mix, w_in, hgrn_lb_logits, hgrn_gnorm, w_br_a, w_br_b, w_br_c, w_out, norm_ffn, w_up, conv_w,
           conv_b, w_down, norm_final):
    B, L, D = x_prompt.shape
    Bs, Ls, _ = x_sample.shape
    depth = w_in.shape[0]
    NP = page_table.shape[1]
    past = NP * PAGE
    F = w_down.shape[1]
    WA = N_HEADS * DH
    WB = N_HEADS * DK
    Tp, Ts = B * L, Bs * Ls
    n_pool = cache_kv_a.shape[1]

    lb_soft = jax.nn.softmax(hgrn_lb_logits.astype(F32), axis=0)
    lb_all = jnp.cumsum(lb_soft, axis=0) - lb_soft[0]

    sizes = (WA, WA, WA, WB, WB, WB, WB, WA, WA, WA, N_HEADS * DH, DH, N_HEADS, 3 * D)
    offs = [0]
    for s in sizes:
        offs.append(offs[-1] + s)
    (o_qa, o_ka, o_va, o_qh, o_fh, o_ih, o_gh, o_qc, o_kc, o_vc, o_qi, o_ki, o_wi, o_gt, o_end) = offs
    QA_BLK, KA_BLK, VA_BLK, VC_BLK = (4 * WB) // WA, (4 * WB) // WA + 1, (4 * WB) // WA + 2, (4 * WB) // WA + 3
    gate_off = 4 * WB + 4 * WA
    assert gate_off % D == 0
    GATE_BLK = gate_off // D
    NPJ = gate_off + 3 * D
    N_ROPE_TILES = 4
    NRJ = (N_ROPE_TILES + 1) * WA
    wi_off = N_ROPE_TILES * WA
    WI_BLK = wi_off // LANES

    cache_a = cache_kv_a.transpose(0, 1, 3, 4, 5, 2).reshape(depth, n_pool, 2 * WA, PAGE)
    cache_c = cache_kv_c.transpose(0, 1, 3, 4, 5, 2).reshape(depth, n_pool, 2 * WA, PAGE)
    cache_i = cache_kidx_c.transpose(0, 1, 3, 2)

    cos_p, sin_p = _rope_tables(jnp.arange(L, dtype=I32), WA)
    cos_s, sin_s = _rope_tables(past + (jnp.arange(Ts, dtype=I32) % Ls), WA)

    tm_p = _pick(Tp, (1024, 512, 256, 128))
    tm_p = min(tm_p, L)
    tn_main = _pick(NPJ, (512, 256, 128))
    tn_up = _pick(2 * F, (512, 256, 128))
    tkf = _pick(F, (1408, 256, 128))
    tq_a = _pick(L, (256, 128))
    tq_c = _pick(L, (256, 128))
    tk_c = _pick(L, (512, 256, 128))
    C_h = _pick(L, (64,))
    G = _pick(NP, (16, 8, 4, 2, 1))
    n_sel_p = min(TOPK_MAX, L // 4)
    n_sel_s = min(TOPK_MAX, (past + Ls) // 4)

    xp = x_prompt.reshape(Tp, D)
    xs = x_sample.reshape(Ts, D)
    tpos_s = (jnp.arange(Ts, dtype=I32) % Ls).reshape(Ts, 1)

    def pad_rows(a, rows):
        return jnp.pad(a, ((0, 0), (0, rows - a.shape[1]), (0, 0)))

    outs_p = ([], [], [], [], [])
    outs_s = ([], [], [], [], [])
    for l in range(depth):
        w = w_in[l]
        col = lambda o, n: w[:, o:o + n]
        w_main = jnp.concatenate(
            [col(o_qh, 4 * WB), col(o_qa, 3 * WA), col(o_vc, WA), col(o_gt, 3 * D)], axis=1).astype(BF16)
        w_r = jnp.concatenate([col(o_qc, WA), col(o_kc, WA), col(o_qi, WA)] + [col(o_ki, DH)] * N_HEADS, axis=1)
        w_tail = jnp.concatenate([col(o_wi, N_HEADS), jnp.zeros((D, WA - N_HEADS), w.dtype)], axis=1)
        w_rot = jnp.concatenate([_rotate_half_cols(w_r), jnp.zeros((D, WA), w.dtype)], axis=1).astype(BF16)
        w_r = jnp.concatenate([w_r, w_tail], axis=1).astype(BF16)
        wa_b, wb_b, wc_b, wo_b = (t[l].astype(BF16) for t in (w_br_a, w_br_b, w_br_c, w_out))
        wup_b = w_up[l].astype(BF16)
        wd_b = w_down[l].astype(BF16)
        last = l == depth - 1

        vc_col = VC_BLK * WA
        pf, pb, vT = _rms_proj(xp, norm_mix[l], w_main, tm=tm_p, tn=tn_main,
                               t_outs=((vc_col // tn_main, vc_col % tn_main, WA, WA, tk_c, BF16),))
        rf, rb, qcT, qiT, wiT = _rms_proj(
            xp, norm_mix[l], w_r, tm=tm_p, tn=WA, rope_args=(w_rot, cos_p, sin_p), n_rope_tiles=N_ROPE_TILES,
            t_outs=((0, 0, WA, WA, tq_c, BF16), (2, 0, WA, WA, tq_c, BF16), (N_ROPE_TILES, 0, LANES, 8, tq_c, F32)))
        oa = _attn_a_prompt(pb, B, L, tq=tq_a, q_blk=QA_BLK, k_blk=KA_BLK, v_blk=VA_BLK)
        ob, s_new = _hgrn(pf.reshape(B, L, NPJ), lb_all[l], hgrn_gnorm[l], None, C=C_h, c=min(16, C_h), valid=C_h)
        oc = _dsa_prompt(qcT, qiT, wiT, rb, vT, B, L, tq=tq_c, tk=tk_c, n_sel=n_sel_p)
        x1 = _merge(oa, ob.reshape(Tp, WB), oc, pf, xp, wa_b, wb_b, wc_b, wo_b, tm=min(512, tm_p), gate_blk=GATE_BLK)
        u = _rms_proj(x1, norm_ffn[l], wup_b, tm=tm_p, tn=tn_up, emit_bf16=False)
        xp = _ffn_down(u, x1, conv_w[l], conv_b[l], wd_b, tm=min(512, tm_p), tkf=tkf, seq_len=L,
                       g_final=norm_final if last else None)
        kv_off = (KA_BLK * WA)
        outs_p[0].append(pf[:, kv_off:kv_off + 2 * WA].reshape(B, L, 2, N_HEADS, DH))
        outs_p[1].append(jnp.stack([rf[:, WA:2 * WA].reshape(B, L, N_HEADS, DH),
                                    pf[:, VC_BLK * WA:(VC_BLK + 1) * WA].reshape(B, L, N_HEADS, DH)], axis=2))
        outs_p[2].append(rf[:, 3 * WA:3 * WA + DH].reshape(B, L, DH))
        outs_p[3].append(s_new)
        outs_p[4].append(u.reshape(B, L, 2 * F)[:, L - 2:])

        pf, pb = _rms_proj(xs, norm_mix[l], w_main, tm=Ts, tn=tn_main)
        rf, rb = _rms_proj(xs, norm_mix[l], w_r, tm=Ts, tn=WA, rope_args=(w_rot, cos_s, sin_s),
                           n_rope_tiles=N_ROPE_TILES)
        pb3 = pb.reshape(Bs, Ls, NPJ)
        rb3 = rb.reshape(Bs, Ls, NRJ)
        blk = lambda a, k: a[:, :, k * WA:(k + 1) * WA]
        oa8 = _attn_a_sample(pad_rows(blk(pb3, QA_BLK), 8), pad_rows(blk(pb3, KA_BLK), PAGE),
                             pad_rows(blk(pb3, VA_BLK), PAGE), cache_a, l, page_table, G=G)
        ph = jnp.pad(pf[:, :4 * WB].reshape(Bs, Ls, 4 * WB), ((0, 0), (0, 8 - Ls), (0, 0)))
        ob8, s_new = _hgrn(ph, lb_all[l], hgrn_gnorm[l], state_hgrn[l], C=8, c=8, valid=Ls)
        qi = blk(rb3, 2).reshape(Bs, Ls, N_HEADS, DH).transpose(0, 2, 1, 3)
        qi32 = jnp.pad(qi, ((0, 0), (0, 0), (0, 8 - Ls), (0, 0))).reshape(Bs, 4 * 8, DH)
        wi = rf[:, wi_off:wi_off + N_HEADS].reshape(Bs, Ls, N_HEADS).transpose(0, 2, 1) * (N_HEADS ** -0.5)
        w32 = jnp.broadcast_to(jnp.pad(wi, ((0, 0), (0, 0), (0, 8 - Ls))).reshape(Bs, 32, 1), (Bs, 32, LANES))
        kin = pad_rows(rb3[:, :, 3 * WA:3 * WA + DH], PAGE)
        qc8 = pad_rows(blk(rb3, 0), 8)
        head = (jnp.arange(WA) // DH)[None, None, None, :] == jnp.arange(N_HEADS)[None, :, None, None]
        qc32 = (jnp.where(head, qc8[:, None], 0) * jnp.asarray(DH ** -0.5, BF16)).reshape(Bs, 32, WA).astype(BF16)
        oc8 = _dsa_sample(qi32, w32, kin, qc32, pad_rows(blk(rb3, 1), PAGE), pad_rows(blk(pb3, VC_BLK), PAGE),
                          cache_i, cache_c, l, page_table, G=G, n_new=Ls, n_sel=n_sel_s)
        x1 = _merge(oa8[:, :Ls].reshape(Ts, WA), ob8[:, :Ls].reshape(Ts, WB), oc8[:, :Ls].reshape(Ts, WA), pf, xs,
                    wa_b, wb_b, wc_b, wo_b, tm=Ts, gate_blk=GATE_BLK)
        u = _rms_proj(x1, norm_ffn[l], wup_b, tm=Ts, tn=tn_up, emit_bf16=False)
        prev = state_ffn_conv[l]
        u3 = u.reshape(Bs, Ls, 2 * F)
        p1 = jnp.broadcast_to(prev[:, 1:2], (Bs, Ls, 2 * F)).reshape(Ts, 2 * F)
        p2 = jnp.concatenate([prev, jnp.zeros((Bs, Ls - 2, 2 * F), F32)], axis=1).reshape(Ts, 2 * F)
        xs = _ffn_down(u, x1, conv_w[l], conv_b[l], wd_b, tm=Ts, tkf=tkf, seq_len=Ls, prev=(p1, p2, tpos_s),
                       g_final=norm_final if last else None)
        kv_off = (KA_BLK * WA)
        outs_s[0].append(pf[:, kv_off:kv_off + 2 * WA].reshape(Bs, Ls, 2, N_HEADS, DH))
        outs_s[1].append(jnp.stack([rf[:, WA:2 * WA].reshape(Bs, Ls, N_HEADS, DH),
                                    pf[:, VC_BLK * WA:(VC_BLK + 1) * WA].reshape(Bs, Ls, N_HEADS, DH)], axis=2))
        outs_s[2].append(rf[:, 3 * WA:3 * WA + DH].reshape(Bs, Ls, DH))
        outs_s[3].append(s_new)
        outs_s[4].append(jnp.concatenate([prev, u3], axis=1)[:, Ls:])

    st = lambda xs_: jnp.stack(xs_, axis=0)
    return (xp.reshape(B, L, D), xs.reshape(Bs, Ls, D),
            st(outs_p[0]), st(outs_s[0]), st(outs_p[1]), st(outs_s[1]), st(outs_p[2]), st(outs_s[2]),
            st(outs_p[3]), st(outs_s[3]), st(outs_p[4]), st(outs_s[4]))
```

```python
import functools

import jax
import jax.numpy as jnp
from jax import lax
from jax.experimental import pallas as pl
from jax.experimental.pallas import tpu as pltpu

F32 = jnp.float32
BF16 = jnp.bfloat16
I32 = jnp.int32

EPS = 1e-6
NEG_BIG = -1e30
LOG_F_MIN = -30.0
ROPE_THETA = 10000.0
TOPK_MAX = 256
PAGE = 128
N_HEADS = 4
DH = 64
DK = 128
LANES = 128
VMEM_LIMIT = 56 * 1024 * 1024
INT_MIN = -(2 ** 31)
EXP_UNDERFLOW = -110.0


def _cparams(sem):
    return pltpu.CompilerParams(dimension_semantics=sem, vmem_limit_bytes=VMEM_LIMIT)


def _dot(a, b):
    return jnp.dot(a, b, preferred_element_type=F32)


def _dot_nt(a, b):
    return lax.dot_general(a, b, (((1,), (1,)), ((), ())), preferred_element_type=F32)


def _dot_tn(a, b):
    return lax.dot_general(a, b, (((0,), (0,)), ((), ())), preferred_element_type=F32)


def _silu(x):
    return x * jax.nn.sigmoid(x)


def _stack_heads(q, rows):
    head = lax.broadcasted_iota(I32, (rows, N_HEADS * DH), 1) // DH
    return jnp.concatenate([jnp.where(head == h, q, jnp.zeros_like(q)) for h in range(N_HEADS)], axis=0)


def _unstack_heads(acc, rows):
    head = lax.broadcasted_iota(I32, (rows, N_HEADS * DH), 1) // DH
    out = jnp.zeros((rows, N_HEADS * DH), acc.dtype)
    for h in range(N_HEADS):
        out = jnp.where(head == h, acc[h * rows:(h + 1) * rows], out)
    return out


def _proj_body(*refs, rope, emit_bf16, n_rope_tiles, t_outs):
    x_ref, g_ref, w_ref = refs[:3]
    rest = refs[3:]
    if rope:
        wrot_ref, cos_ref, sin_ref = rest[:3]
        rest = rest[3:]
    of_ref = rest[0]
    ob_ref = rest[1] if emit_bf16 else None
    t_refs = rest[(2 if emit_bf16 else 1):-1]
    h_ref = rest[-1]

    @pl.when(pl.program_id(1) == 0)
    def _():
        x = x_ref[...]
        ms = jnp.mean(x * x, axis=-1, keepdims=True)
        h_ref[...] = (x * lax.rsqrt(ms + EPS) * g_ref[...]).astype(BF16)

    h = h_ref[...]
    acc = _dot(h, w_ref[...])
    if rope:
        roped = acc * cos_ref[...] + _dot(h, wrot_ref[...]) * sin_ref[...]
        acc = jnp.where(pl.program_id(1) < n_rope_tiles, roped, acc)
    of_ref[...] = acc
    if emit_bf16:
        ob_ref[...] = acc.astype(BF16)
    for t_ref, (j_tile, c0, width, keep, sub, dtype) in zip(t_refs, t_outs):
        @pl.when(pl.program_id(1) == j_tile)
        def _(t_ref=t_ref, c0=c0, width=width, keep=keep, sub=sub, dtype=dtype):
            for r in range(acc.shape[0] // sub):
                t_ref[r] = acc[r * sub:(r + 1) * sub, c0:c0 + width].T[:keep].astype(dtype)


def _rms_proj(x, g, w, *, tm, tn, rope_args=None, emit_bf16=True, n_rope_tiles=0, t_outs=()):
    T, D = x.shape
    N = w.shape[1]
    rope = rope_args is not None
    in_specs = [pl.BlockSpec((tm, D), lambda i, j: (i, 0)),
                pl.BlockSpec((1, D), lambda i, j: (0, 0)),
                pl.BlockSpec((D, tn), lambda i, j: (0, j))]
    args = [x, g.reshape(1, D), w]
    if rope:
        w_rot, cos, sin = rope_args
        nblk = cos.shape[0] // tm
        in_specs += [pl.BlockSpec((D, tn), lambda i, j: (0, j)),
                     pl.BlockSpec((tm, tn), lambda i, j: (i % nblk, 0)),
                     pl.BlockSpec((tm, tn), lambda i, j: (i % nblk, 0))]
        args += [w_rot, cos, sin]
    out_shape = [jax.ShapeDtypeStruct((T, N), F32)]
    out_specs = [pl.BlockSpec((tm, tn), lambda i, j: (i, j))]
    if emit_bf16:
        out_shape.append(jax.ShapeDtypeStruct((T, N), BF16))
        out_specs.append(pl.BlockSpec((tm, tn), lambda i, j: (i, j)))
    for (_, _, _, keep, sub, dtype) in t_outs:
        out_shape.append(jax.ShapeDtypeStruct((T // sub, keep, sub), dtype))
        out_specs.append(pl.BlockSpec((tm // sub, keep, sub), lambda i, j: (i, 0, 0)))
    outs = pl.pallas_call(
        functools.partial(_proj_body, rope=rope, emit_bf16=emit_bf16, n_rope_tiles=n_rope_tiles, t_outs=tuple(t_outs)),
        grid=(T // tm, N // tn),
        in_specs=in_specs, out_specs=out_specs, out_shape=out_shape,
        scratch_shapes=[pltpu.VMEM((tm, D), BF16)],
        compiler_params=_cparams(("parallel", "arbitrary")),
        name="rms_proj_rope" if rope else "rms_proj",
    )(*args)
    return outs if (emit_bf16 or t_outs) else outs[0]


def _suffix_matrix():
    j = lax.broadcasted_iota(I32, (2 * LANES, LANES), 0) & (LANES - 1)
    s = lax.broadcasted_iota(I32, (2 * LANES, LANES), 1)
    return jnp.where(j > s, 1.0, 0.0).astype(BF16)


def _log_one_minus_beta(z):
    return -(jnp.maximum(z, 0.0) + jnp.log(1.0 + jnp.exp(-jnp.abs(z))))


def _sb_update(z, carry, suffix, vis):
    M, tk = z.shape
    ls = _log_one_minus_beta(z)
    if vis is not None:
        ls = jnp.where(vis, ls, 0.0)
    hi = ls.astype(BF16)
    lo = (ls - hi.astype(F32)).astype(BF16)
    lz = z + ls
    n = tk // LANES
    outs = [None] * n
    for g in reversed(range(n)):
        sl = slice(g * LANES, (g + 1) * LANES)
        between = _dot(jnp.concatenate([hi[:, sl], lo[:, sl]], axis=1), suffix)
        e = lz[:, sl] + between + carry
        if vis is not None:
            e = jnp.where(vis[:, sl], e, NEG_BIG)
        outs[g] = jnp.exp(e)
        carry = carry + jnp.sum(ls[:, sl], axis=1, keepdims=True)
    a = outs[0] if n == 1 else jnp.concatenate(outs, axis=1)
    return a.astype(BF16), carry


def _attn_a_prompt_body(q_ref, k_ref, v_ref, o_ref, *, tq):
    i = pl.program_id(1)
    M = N_HEADS * tq
    qs = _stack_heads(q_ref[...], tq) * jnp.asarray(DH ** -0.5, BF16)
    suffix = _suffix_matrix()

    def block(k0, carry, acc, vis):
        kb = k_ref[pl.ds(k0, tq), :]
        vb = v_ref[pl.ds(k0, tq), :]
        a, carry = _sb_update(_dot_nt(qs, kb), carry, suffix, vis)
        return carry, acc + _dot(a, vb)

    row_t = lax.broadcasted_iota(I32, (M, tq), 0) & (tq - 1)
    col_s = lax.broadcasted_iota(I32, (M, tq), 1)
    carry, acc = block(pl.multiple_of(i * tq, tq), jnp.zeros((M, 1), F32),
                       jnp.zeros((M, N_HEADS * DH), F32), col_s < row_t)

    def alive(carry):
        return (jnp.max(carry) > EXP_UNDERFLOW).astype(I32)

    def cond(c):
        return (c[0] < i) & (c[1] > 0)

    def body(c):
        step = c[0]
        k0 = pl.multiple_of((i - 1 - step) * tq, tq)
        carry, acc = block(k0, c[2], c[3], None)
        return step + 1, alive(carry), carry, acc

    _, _, carry, acc = lax.while_loop(cond, body, (jnp.int32(0), alive(carry), carry, acc))
    o_ref[...] = _unstack_heads(acc, tq).astype(o_ref.dtype)


def _attn_a_prompt(pb, B, L, *, tq, q_blk, k_blk, v_blk):
    W = N_HEADS * DH
    nq = L // tq
    return pl.pallas_call(
        functools.partial(_attn_a_prompt_body, tq=tq),
        grid=(B, nq),
        in_specs=[pl.BlockSpec((tq, W), lambda b, i: (b * nq + i, q_blk)),
                  pl.BlockSpec((L, W), lambda b, i: (b, k_blk)),
                  pl.BlockSpec((L, W), lambda b, i: (b, v_blk))],
        out_specs=pl.BlockSpec((tq, W), lambda b, i: (b * nq + i, 0)),
        out_shape=jax.ShapeDtypeStruct((B * L, W), BF16),
        compiler_params=_cparams(("parallel", "arbitrary")),
        name="attn_a_prompt",
    )(pb, pb, pb)


def _attn_a_sample_body(*refs, G, resume):
    n_pref = 2 if resume else 1
    q_ref, a_ref, b_ref = refs[n_pref:n_pref + 3]
    pages = refs[n_pref + 3:n_pref + 3 + G]
    rest = refs[n_pref + 3 + G:]
    if resume:
        o_ref, carry_ref, acc_ref = rest
    else:
        o_ref, carry_out_ref, acc_out_ref, carry_ref, acc_ref = rest
    p = pl.program_id(1)
    W = N_HEADS * DH
    R = 8
    M = N_HEADS * R
    qs = _stack_heads(q_ref[0], R) * jnp.asarray(DH ** -0.5, BF16)
    suffix = _suffix_matrix()

    @pl.when(p == 0)
    def _():
        if resume:
            carry_ref[...] = a_ref[0]
            acc_ref[...] = b_ref[0]
        else:
            row_t = lax.broadcasted_iota(I32, (M, PAGE), 0) & (R - 1)
            col_s = lax.broadcasted_iota(I32, (M, PAGE), 1)
            a, carry = _sb_update(_dot_nt(qs, a_ref[0]), jnp.zeros((M, 1), F32), suffix, col_s < row_t)
            carry_ref[...] = jnp.broadcast_to(carry, carry_ref.shape)
            acc_ref[...] = _dot(a, b_ref[0])

    @pl.when(jnp.max(carry_ref[...]) > EXP_UNDERFLOW)
    def _():
        kcat = jnp.concatenate([pages[g][:W, :] for g in reversed(range(G))], axis=1).astype(BF16)
        vcat = jnp.concatenate([pages[g][W:, :] for g in reversed(range(G))], axis=1).astype(BF16)
        a, carry = _sb_update(_dot(qs, kcat), carry_ref[:, 0:1], suffix, None)
        carry_ref[...] = jnp.broadcast_to(carry, carry_ref.shape)
        acc_ref[...] += _dot_nt(a, vcat)

    @pl.when(p == pl.num_programs(1) - 1)
    def _():
        o_ref[0] = _unstack_heads(acc_ref[...], R).astype(o_ref.dtype)
        if not resume:
            carry_out_ref[0] = carry_ref[...]
            acc_out_ref[0] = acc_ref[...]


def _attn_a_sample(q8, kn, vn, cache, layer, page_table, *, G):
    Bs = q8.shape[0]
    NP = page_table.shape[1]
    W = N_HEADS * DH
    M = N_HEADS * 8
    nstep = NP // G
    scratch = [pltpu.VMEM((M, LANES), F32), pltpu.VMEM((M, W), F32)]
    row3 = lambda *a: (a[0], 0, 0)

    def head_page(g):
        return pl.BlockSpec((None, None, 2 * W, PAGE), lambda b, p, pt: (layer, pt[b, NP - 1 - g], 0, 0))

    o, carry, acc = pl.pallas_call(
        functools.partial(_attn_a_sample_body, G=G, resume=False),
        grid_spec=pltpu.PrefetchScalarGridSpec(
            num_scalar_prefetch=1, grid=(Bs, 1),
            in_specs=[pl.BlockSpec((1, 8, W), row3), pl.BlockSpec((1, PAGE, W), row3),
                      pl.BlockSpec((1, PAGE, W), row3)] + [head_page(g) for g in range(G)],
            out_specs=[pl.BlockSpec((1, 8, W), row3), pl.BlockSpec((1, M, LANES), row3), pl.BlockSpec((1, M, W), row3)],
            scratch_shapes=scratch),
        out_shape=[jax.ShapeDtypeStruct((Bs, 8, W), BF16), jax.ShapeDtypeStruct((Bs, M, LANES), F32),
                   jax.ShapeDtypeStruct((Bs, M, W), F32)],
        compiler_params=_cparams(("parallel", "arbitrary")),
        name="attn_a_sample",
    )(page_table, q8, kn, vn, *([cache] * G))
    if nstep == 1:
        return o
    alive = (jnp.max(carry, axis=(1, 2)) > EXP_UNDERFLOW).astype(I32)

    def tail_page(g):
        return pl.BlockSpec((None, None, 2 * W, PAGE),
                            lambda b, p, pt, al: (layer, jnp.where(al[b] > 0, pt[b, NP - 1 - ((p + 1) * G + g)], 0), 0, 0))

    return pl.pallas_call(
        functools.partial(_attn_a_sample_body, G=G, resume=True),
        grid_spec=pltpu.PrefetchScalarGridSpec(
            num_scalar_prefetch=2, grid=(Bs, nstep - 1),
            in_specs=[pl.BlockSpec((1, 8, W), row3), pl.BlockSpec((1, M, LANES), row3),
                      pl.BlockSpec((1, M, W), row3)] + [tail_page(g) for g in range(G)],
            out_specs=pl.BlockSpec((1, 8, W), row3),
            scratch_shapes=scratch),
        out_shape=jax.ShapeDtypeStruct((Bs, 8, W), BF16),
        compiler_params=_cparams(("parallel", "arbitrary")),
        name="attn_a_sample_tail",
    )(page_table, alive, q8, carry, acc, *([cache] * G))


def _cumsum_rows(x):
    C = x.shape[0]
    row = lax.broadcasted_iota(I32, x.shape, 0)
    sh = 1
    while sh < C:
        x = x + jnp.where(row >= sh, pltpu.roll(x, sh, axis=0), 0.0)
        sh *= 2
    return x


def _hgrn_body(*refs, C, c, valid, has_s0):
    q_ref, f_ref, i_ref, g_ref, lb_ref, gn_ref = refs[:6]
    rest = refs[6:]
    if has_s0:
        s0_ref = rest[0]
        rest = rest[1:]
    o_ref, sout_ref, st_ref = rest
    ci = pl.program_id(1)

    @pl.when(ci == 0)
    def _():
        for h in range(N_HEADS):
            st_ref[h] = s0_ref[0, h].T if has_s0 else jnp.zeros((DK, DK), F32)

    row = lax.broadcasted_iota(I32, (C, DK), 0)
    rowc = lax.broadcasted_iota(I32, (c, 1), 0)
    for h in range(N_HEADS):
        hs = slice(h * DK, (h + 1) * DK)
        kk = (1.0 - lb_ref[:, hs]) * jax.nn.sigmoid(-f_ref[0, :, hs])
        lg = jnp.maximum(jnp.log1p(-kk), LOG_F_MIN)
        if valid < C:
            kk = jnp.where(row < valid, kk, 0.0)
            lg = jnp.where(row < valid, lg, 0.0)
        qq = _silu(q_ref[0, :, hs])
        vv = i_ref[0, :, hs]
        cum = _cumsum_rows(lg)
        st = st_ref[h]
        o = _dot_nt((qq * jnp.exp(cum)).astype(BF16), st.astype(BF16))
        parts = []
        for blk in range(C // c):
            r0 = blk * c
            q_b = qq[r0:r0 + c]
            cum_b = cum[r0:r0 + c]
            o_b = jnp.zeros((c, DK), F32)
            if blk > 0:
                base = cum[r0 - 1:r0]
                qt = q_b * jnp.exp(cum_b - base)
                kt = kk[:r0] * jnp.exp(base - cum[:r0])
                sc = _dot_nt(qt.astype(BF16), kt.astype(BF16))
                o_b = o_b + _dot(sc.astype(BF16), vv[:r0].astype(BF16))
            for s in range(c):
                r = r0 + s
                d = jnp.minimum(cum_b - cum[r:r + 1], 0.0)
                w = jnp.sum(q_b * kk[r:r + 1] * jnp.exp(d), axis=1, keepdims=True)
                o_b = o_b + jnp.where(rowc >= s, w, 0.0) * vv[r:r + 1]
            parts.append(o_b)
        o = o + (parts[0] if len(parts) == 1 else jnp.concatenate(parts, axis=0))
        last = cum[C - 1:C]
        kd = kk * jnp.exp(last - cum)
        st_ref[h] = st * jnp.exp(last) + _dot_tn(vv.astype(BF16), kd.astype(BF16))
        ms = jnp.mean(o * o, axis=1, keepdims=True)
        y = o * lax.rsqrt(ms + EPS) * gn_ref[...] * _silu(g_ref[0, :, hs])
        o_ref[0, :, hs] = y.astype(o_ref.dtype)

    @pl.when(ci == pl.num_programs(1) - 1)
    def _():
        for h in range(N_HEADS):
            sout_ref[0, h] = st_ref[h].T


def _hgrn(p3, lb, gn, s0, *, C, c, valid, col0=0):
    B, L, _ = p3.shape
    W = N_HEADS * DK
    has_s0 = s0 is not None
    in_specs = [pl.BlockSpec((1, C, W), functools.partial(lambda b, ci, k: (b, ci, col0 + k), k=k)) for k in range(4)]
    in_specs += [pl.BlockSpec((1, W), lambda b, ci: (0, 0)), pl.BlockSpec((1, DK), lambda b, ci: (0, 0))]
    args = [p3, p3, p3, p3, lb.reshape(1, W), gn.reshape(1, DK)]
    if has_s0:
        in_specs.append(pl.BlockSpec((1, N_HEADS, DK, DK), lambda b, ci: (b, 0, 0, 0)))
        args.append(s0)
    return pl.pallas_call(
        functools.partial(_hgrn_body, C=C, c=c, valid=valid, has_s0=has_s0),
        grid=(B, L // C),
        in_specs=in_specs,
        out_specs=[pl.BlockSpec((1, C, W), lambda b, ci: (b, ci, 0)),
                   pl.BlockSpec((1, N_HEADS, DK, DK), lambda b, ci: (b, 0, 0, 0))],
        out_shape=[jax.ShapeDtypeStruct((B, L, W), BF16), jax.ShapeDtypeStruct((B, N_HEADS, DK, DK), F32)],
        scratch_shapes=[pltpu.VMEM((N_HEADS, DK, DK), F32)],
        compiler_params=_cparams(("parallel", "arbitrary")),
        name="hgrn2",
    )(*args)


def _sort_key(score):
    b = pltpu.bitcast(score, I32)
    return jnp.where(b < 0, b ^ jnp.int32(0x7FFFFFFF), b)


def _prefix_matrix(n):
    j = lax.broadcasted_iota(I32, (n, n + LANES), 0)
    s = lax.broadcasted_iota(I32, (n, n + LANES), 1)
    return jnp.where((s >= n) | (j <= s), 1.0, 0.0).astype(BF16)


def _kth_largest_key(count_ge, rows, n_sel):
    def body(t, T):
        cand = T + jnp.left_shift(jnp.int32(1), 31 - t)
        return jnp.where(count_ge(cand) >= n_sel, cand, T)
    return lax.fori_loop(0, 32, body, jnp.full((rows, 1), INT_MIN, I32))


def _dsa_prompt_body(qcT_ref, qiT_ref, wiT_ref, kc_ref, ki_ref, vT_ref, lt_ref, o_ref, keys_ref, kh_ref, kl_ref,
                     *, tq, tk, n_sel):
    i = pl.program_id(1)
    nkb = ((i + 1) * tq + tk - 1) // tk
    M = N_HEADS * tq
    W = N_HEADS * DH
    I16 = jnp.int16
    kpos0 = lax.broadcasted_iota(I32, (tk, tq), 0)
    qpos = i * tq + lax.broadcasted_iota(I32, (tk, tq), 1)
    row_head = lax.broadcasted_iota(I32, (W, tq), 0) // DH

    def stack_lanes(qT):
        return jnp.concatenate([jnp.where(row_head == h, qT, jnp.zeros_like(qT)) for h in range(N_HEADS)], axis=1)

    qisT = stack_lanes(qiT_ref[0])
    wi = wiT_ref[0]
    w_row = jnp.concatenate([wi[j:j + 1, :] for j in range(N_HEADS)], axis=1) * (N_HEADS ** -0.5) * (DH ** -0.5)

    def score_body(j, _):
        k0 = pl.multiple_of(j * tk, tk)
        d = jnp.maximum(_dot(ki_ref[pl.ds(k0, tk), :], qisT), 0.0) * w_row
        score = d[:, 0:tq]
        for jh in range(1, N_HEADS):
            score = score + d[:, jh * tq:(jh + 1) * tq]
        score = jnp.where(kpos0 + k0 <= qpos, score + 0.0, NEG_BIG)
        key = _sort_key(score)
        keys_ref[j] = key
        kh_ref[j] = (key >> 16).astype(I16)
        kl_ref[j] = ((key & 0xFFFF) - 32768).astype(I16)
        return 0

    lax.fori_loop(0, nkb, score_body, 0)

    def count16(ref, cand):
        c16 = cand.astype(I16)

        def body(j, acc):
            m = jnp.where(ref[j] >= c16, jnp.ones((), BF16), jnp.zeros((), BF16))
            parts = [m[16 * r:16 * (r + 1)] for r in range(tk // 16)]
            while len(parts) > 1:
                parts = [parts[a] + parts[a + 1] for a in range(0, len(parts) - 1, 2)] + parts[len(parts) & ~1:]
            return acc + parts[0].astype(F32)

        acc = lax.fori_loop(0, nkb, body, jnp.zeros((16, tq), F32))
        return jnp.sum(acc, axis=0, keepdims=True)

    def kth16(ref, target):
        def body(t, T):
            cand = T + jnp.left_shift(jnp.int32(1), 15 - t)
            return jnp.where(count16(ref, cand) >= target, cand, T)
        return lax.fori_loop(0, 16, body, jnp.full((1, tq), -32768, I32))

    def count_above(ref, T):
        return jnp.where(T >= 32767, 0.0, count16(ref, jnp.minimum(T + 1, 32767)))

    TH = kth16(kh_ref, float(n_sel))
    need = n_sel - count_above(kh_ref, TH)
    th16 = TH.astype(I16)

    def low_body(j, _):
        kl_ref[j] = jnp.where(kh_ref[j] == th16, kl_ref[j], jnp.full((), -32768, I16))
        return 0

    lax.fori_loop(0, nkb, low_body, 0)
    TL = kth16(kl_ref, need)
    room = need - count_above(kl_ref, TL)
    T = TH * 65536 + (TL + 32768)

    qcsT = stack_lanes(qcT_ref[0]) * jnp.asarray(DH ** -0.5, BF16)
    lt = lt_ref[...]

    def att_body(j, c):
        m, l, accT, eq_before = c
        k0 = pl.multiple_of(j * tk, tk)
        key = keys_ref[j]
        eq = key == T
        rank = _dot(lt, jnp.where(eq, 1.0, 0.0).astype(BF16)) + eq_before
        sel = (key > T) | (eq & (rank <= room))
        bias = jnp.where(sel & (kpos0 + k0 <= qpos), 0.0, NEG_BIG)
        s = _dot(kc_ref[pl.ds(k0, tk), :], qcsT) + jnp.concatenate([bias] * N_HEADS, axis=1)
        m_new = jnp.maximum(m, jnp.max(s, axis=0, keepdims=True))
        alpha = jnp.exp(m - m_new)
        p = jnp.exp(s - m_new)
        l = alpha * l + jnp.sum(p, axis=0, keepdims=True)
        accT = alpha * accT + _dot(vT_ref[j], p.astype(BF16))
        return m_new, l, accT, rank[tk - 1:tk, :]

    init = (jnp.full((1, M), NEG_BIG, F32), jnp.zeros((1, M), F32), jnp.zeros((W, M), F32), jnp.zeros((1, tq), F32))
    m, l, accT, _ = lax.fori_loop(0, nkb, att_body, init)
    outT = accT / l
    oT = jnp.zeros((W, tq), F32)
    for h in range(N_HEADS):
        oT = jnp.where(row_head == h, outT[:, h * tq:(h + 1) * tq], oT)
    o_ref[...] = oT.T.astype(o_ref.dtype)


def _dsa_prompt(qcT, qiT, wiT, rb, vT, B, L, *, tq, tk, n_sel):
    W = N_HEADS * DH
    nq = L // tq
    assert tk // 16 <= 256
    lt = jnp.where(jnp.arange(tk)[None, :] <= jnp.arange(tk)[:, None], 1.0, 0.0).astype(BF16)
    return pl.pallas_call(
        functools.partial(_dsa_prompt_body, tq=tq, tk=tk, n_sel=n_sel),
        grid=(B, nq),
        in_specs=[pl.BlockSpec((1, W, tq), lambda b, i: (b * nq + i, 0, 0)),
                  pl.BlockSpec((1, W, tq), lambda b, i: (b * nq + i, 0, 0)),
                  pl.BlockSpec((1, 8, tq), lambda b, i: (b * nq + i, 0, 0)),
                  pl.BlockSpec((L, W), lambda b, i: (b, 1)),
                  pl.BlockSpec((L, W), lambda b, i: (b, 3)),
                  pl.BlockSpec((L // tk, W, tk), lambda b, i: (b, 0, 0)),
                  pl.BlockSpec((tk, tk), lambda b, i: (0, 0))],
        out_specs=pl.BlockSpec((tq, W), lambda b, i: (b * nq + i, 0)),
        out_shape=jax.ShapeDtypeStruct((B * L, W), BF16),
        scratch_shapes=[pltpu.VMEM((L // tk, tk, tq), I32), pltpu.VMEM((L // tk, tk, tq), jnp.int16),
                        pltpu.VMEM((L // tk, tk, tq), jnp.int16)],
        compiler_params=_cparams(("parallel", "arbitrary")),
        name="dsa_prompt",
    )(qcT, qiT, wiT, rb, rb, vT, lt)


def _dsa_s_score_body(pt_ref, qi_ref, w_ref, kn_ref, *rest, G, n_new):
    pages = rest[:G]
    o_ref = rest[G]
    p = pl.program_id(1)
    R = 8
    qi = qi_ref[0]
    w = w_ref[0]

    def score(dots):
        d = jnp.maximum(dots * (DH ** -0.5), 0.0) * w
        return (d[0:R] + d[R:2 * R]) + (d[2 * R:3 * R] + d[3 * R:4 * R]) + 0.0

    @pl.when(p < pl.num_programs(1) - 1)
    def _():
        for g in range(G):
            o_ref[0, :, g * PAGE:(g + 1) * PAGE] = score(_dot(qi, pages[g][...].astype(BF16)))

    @pl.when(p == pl.num_programs(1) - 1)
    def _():
        t = lax.broadcasted_iota(I32, (R, PAGE), 0)
        s = lax.broadcasted_iota(I32, (R, PAGE), 1)
        o_ref[0, :, 0:PAGE] = jnp.where((s <= t) & (s < n_new), score(_dot_nt(qi, kn_ref[0])), NEG_BIG)
        if G > 1:
            o_ref[0, :, PAGE:] = jnp.full((R, (G - 1) * PAGE), NEG_BIG, F32)


def _dsa_s_select_body(s_ref, o_ref, *, n_sel):
    R = 8
    key = _sort_key(s_ref[0])
    W = key.shape[1]

    def count_ge(cand):
        m = jnp.where(key >= cand, 1.0, 0.0)
        parts = [m[:, g * LANES:(g + 1) * LANES] for g in range(W // LANES)]
        while len(parts) > 1:
            parts = [parts[a] + parts[a + 1] for a in range(0, len(parts) - 1, 2)] + parts[len(parts) & ~1:]
        return jnp.sum(parts[0], axis=1, keepdims=True)

    T = _kth_largest_key(count_ge, R, n_sel)
    room = n_sel - count_ge(T + 1)
    eq = key == T
    gt = key > T
    nt = W // LANES
    eqf = jnp.where(eq, 1.0, 0.0)
    stacked = jnp.concatenate([eqf[:, g * LANES:(g + 1) * LANES] for g in range(nt)], axis=0).astype(BF16)
    pc = _dot(stacked, _prefix_matrix(LANES))
    before = jnp.zeros((R, LANES), F32)
    tiles = []
    for g in range(nt):
        sl = slice(g * LANES, (g + 1) * LANES)
        rank = pc[g * R:(g + 1) * R, :LANES] + before
        tiles.append(jnp.where(gt[:, sl], 1.0, jnp.where(eq[:, sl] & (rank <= room), 1.0, 0.0)))
        before = before + pc[g * R:(g + 1) * R, LANES:]
    o_ref[0] = jnp.concatenate(tiles, axis=1)


def _dsa_s_attend_body(pt_ref, q_ref, kn_ref, vn_ref, mask_ref, maskn_ref, *rest, G, n_new):
    pages = rest[:G]
    o_ref, m_ref, l_ref, acc_ref = rest[G:]
    p = pl.program_id(1)
    W = N_HEADS * DH
    R = 8
    qs = q_ref[0]

    def update(s, v, m, l, acc, v_feature_major=False):
        m_new = jnp.maximum(m, jnp.max(s, axis=1, keepdims=True))
        alpha = jnp.exp(m - m_new)
        pr = jnp.exp(s - m_new)
        pv = _dot_nt(pr.astype(BF16), v) if v_feature_major else _dot(pr.astype(BF16), v)
        return m_new, alpha * l + jnp.sum(pr, axis=1, keepdims=True), alpha * acc + pv

    @pl.when(p == 0)
    def _():
        t = lax.broadcasted_iota(I32, (R, PAGE), 0)
        sidx = lax.broadcasted_iota(I32, (R, PAGE), 1)
        sel = jnp.where((sidx <= t) & (sidx < n_new), maskn_ref[0, :, 0:PAGE], 0.0)
        s = jnp.where(jnp.concatenate([sel] * N_HEADS, axis=0) > 0.5, _dot_nt(qs, kn_ref[0]), NEG_BIG)
        m, l, acc = update(s, vn_ref[0], jnp.full((N_HEADS * R, 1), NEG_BIG, F32),
                           jnp.zeros((N_HEADS * R, 1), F32), jnp.zeros((N_HEADS * R, W), F32))
        m_ref[...] = jnp.broadcast_to(m, m_ref.shape)
        l_ref[...] = jnp.broadcast_to(l, l_ref.shape)
        acc_ref[...] = acc

    kcat = jnp.concatenate([pages[g][:W, :] for g in range(G)], axis=1).astype(BF16)
    vcat = jnp.concatenate([pages[g][W:, :] for g in range(G)], axis=1).astype(BF16)
    sel = jnp.concatenate([mask_ref[0]] * N_HEADS, axis=0)
    s = jnp.where(sel > 0.5, _dot(qs, kcat), NEG_BIG)
    m, l, acc = update(s, vcat, m_ref[:, 0:1], l_ref[:, 0:1], acc_ref[...], v_feature_major=True)
    m_ref[...] = jnp.broadcast_to(m, m_ref.shape)
    l_ref[...] = jnp.broadcast_to(l, l_ref.shape)
    acc_ref[...] = acc

    @pl.when(p == pl.num_programs(1) - 1)
    def _():
        o_ref[0] = _unstack_heads(acc_ref[...] / l_ref[:, 0:1], R).astype(o_ref.dtype)


def _dsa_sample(qi32, w32, kin, qc32, kcn, vcn, cache_kidx, cache_kv, layer, page_table, *, G, n_new, n_sel):
    Bs = qi32.shape[0]
    NP = page_table.shape[1]
    W = N_HEADS * DH
    nstep = NP // G
    width = (nstep + 1) * G * PAGE

    def kidx_spec(g):
        return pl.BlockSpec((None, None, DH, PAGE),
                            lambda b, p, pt: (layer, pt[b, jnp.minimum(p * G + g, NP - 1)], 0, 0))

    scores = pl.pallas_call(
        functools.partial(_dsa_s_score_body, G=G, n_new=n_new),
        grid_spec=pltpu.PrefetchScalarGridSpec(
            num_scalar_prefetch=1, grid=(Bs, nstep + 1),
            in_specs=[pl.BlockSpec((1, 32, DH), lambda b, p, pt: (b, 0, 0)),
                      pl.BlockSpec((1, 32, LANES), lambda b, p, pt: (b, 0, 0)),
                      pl.BlockSpec((1, PAGE, DH), lambda b, p, pt: (b, 0, 0))] + [kidx_spec(g) for g in range(G)],
            out_specs=pl.BlockSpec((1, 8, G * PAGE), lambda b, p, pt: (b, 0, p))),
        out_shape=jax.ShapeDtypeStruct((Bs, 8, width), F32),
        compiler_params=_cparams(("parallel", "arbitrary")),
        name="dsa_sample_score",
    )(page_table, qi32, w32, kin, *([cache_kidx] * G))

    mask = pl.pallas_call(
        functools.partial(_dsa_s_select_body, n_sel=n_sel),
        grid=(Bs,),
        in_specs=[pl.BlockSpec((1, 8, width), lambda b: (b, 0, 0))],
        out_specs=pl.BlockSpec((1, 8, width), lambda b: (b, 0, 0)),
        out_shape=jax.ShapeDtypeStruct((Bs, 8, width), F32),
        compiler_params=_cparams(("parallel",)),
        name="dsa_sample_select",
    )(scores)

    def kv_spec(g):
        return pl.BlockSpec((None, None, 2 * W, PAGE), lambda b, p, pt: (layer, pt[b, p * G + g], 0, 0))

    return pl.pallas_call(
        functools.partial(_dsa_s_attend_body, G=G, n_new=n_new),
        grid_spec=pltpu.PrefetchScalarGridSpec(
            num_scalar_prefetch=1, grid=(Bs, nstep),
            in_specs=[pl.BlockSpec((1, 32, W), lambda b, p, pt: (b, 0, 0)),
                      pl.BlockSpec((1, PAGE, W), lambda b, p, pt: (b, 0, 0)),
                      pl.BlockSpec((1, PAGE, W), lambda b, p, pt: (b, 0, 0)),
                      pl.BlockSpec((1, 8, G * PAGE), lambda b, p, pt: (b, 0, p)),
                      pl.BlockSpec((1, 8, G * PAGE), lambda b, p, pt: (b, 0, nstep))] + [kv_spec(g) for g in range(G)],
            out_specs=pl.BlockSpec((1, 8, W), lambda b, p, pt: (b, 0, 0)),
            scratch_shapes=[pltpu.VMEM((32, LANES), F32), pltpu.VMEM((32, LANES), F32), pltpu.VMEM((32, W), F32)]),
        out_shape=jax.ShapeDtypeStruct((Bs, 8, W), BF16),
        compiler_params=_cparams(("parallel", "arbitrary")),
        name="dsa_sample_attend",
    )(page_table, qc32, kcn, vcn, mask, mask, *([cache_kv] * G))


def _merge_body(oa_ref, ob_ref, oc_ref, g0_ref, g1_ref, g2_ref, x_ref, wa_ref, wb_ref, wc_ref, wo_ref, o_ref):
    merged = (jax.nn.sigmoid(g0_ref[...]) * _dot(oa_ref[...], wa_ref[...])
              + jax.nn.sigmoid(g1_ref[...]) * _dot(ob_ref[...], wb_ref[...])
              + jax.nn.sigmoid(g2_ref[...]) * _dot(oc_ref[...], wc_ref[...]))
    o_ref[...] = x_ref[...] + _dot(merged.astype(BF16), wo_ref[...])


def _merge(oa, ob, oc, pf, x, wa, wb, wc, wo, *, tm, gate_blk):
    T, D = x.shape
    row = lambda i: (i, 0)
    const = lambda i: (0, 0)
    return pl.pallas_call(
        _merge_body,
        grid=(T // tm,),
        in_specs=[pl.BlockSpec((tm, oa.shape[1]), row), pl.BlockSpec((tm, ob.shape[1]), row),
                  pl.BlockSpec((tm, oc.shape[1]), row),
                  pl.BlockSpec((tm, D), lambda i: (i, gate_blk)), pl.BlockSpec((tm, D), lambda i: (i, gate_blk + 1)),
                  pl.BlockSpec((tm, D), lambda i: (i, gate_blk + 2)), pl.BlockSpec((tm, D), row),
                  pl.BlockSpec(wa.shape, const), pl.BlockSpec(wb.shape, const),
                  pl.BlockSpec(wc.shape, const), pl.BlockSpec(wo.shape, const)],
        out_specs=pl.BlockSpec((tm, D), row),
        out_shape=jax.ShapeDtypeStruct((T, D), F32),
        compiler_params=_cparams(("parallel",)),
        name="merge",
    )(oa, ob, oc, pf, pf, pf, x, wa, wb, wc, wo)


def _ffn_down_body(*refs, tm, seq_tiles, expanded, final_norm):
    ua_ref, ub_ref, ha_ref, hb_ref = refs[:4]
    rest = refs[4:]
    if expanded:
        ha2_ref, hb2_ref, t_ref = rest[:3]
        rest = rest[3:]
    cwa_ref, cwb_ref, cba_ref, cbb_ref, wd_ref, x_ref = rest[:6]
    rest = rest[6:]
    if final_norm:
        gf_ref = rest[0]
        rest = rest[1:]
    o_ref, acc_ref = rest
    i = pl.program_id(0)
    k = pl.program_id(1)

    @pl.when(k == 0)
    def _():
        acc_ref[...] = jnp.zeros_like(acc_ref)

    def conv(u_ref, h_ref, h2_ref, cw_ref, cb_ref):
        u = u_ref[...]
        row = lax.broadcasted_iota(I32, u.shape, 0)
        r1 = pltpu.roll(u, 1, axis=0)
        r2 = pltpu.roll(u, 2, axis=0)
        if expanded:
            t = t_ref[...]
            u1 = jnp.where(t >= 1, r1, h_ref[...])
            u2 = jnp.where(t >= 2, r2, h2_ref[...])
        else:
            h = h_ref[...]
            h = jnp.where(i % seq_tiles == 0, jnp.zeros_like(h), h)
            u1 = jnp.where(row == 0, h[7:8], r1)
            u2 = jnp.where(row == 0, h[6:7], jnp.where(row == 1, h[7:8], r2))
        cw = cw_ref[...]
        return cb_ref[...] + cw[0:1] * u2 + cw[1:2] * u1 + cw[2:3] * u

    a = conv(ua_ref, ha_ref, ha2_ref if expanded else None, cwa_ref, cba_ref)
    b = conv(ub_ref, hb_ref, hb2_ref if expanded else None, cwb_ref, cbb_ref)
    acc_ref[...] += _dot((_silu(a) * b).astype(BF16), wd_ref[...])

    @pl.when(k == pl.num_programs(1) - 1)
    def _():
        y = x_ref[...] + acc_ref[...]
        if final_norm:
            ms = jnp.mean(y * y, axis=-1, keepdims=True)
            y = y * lax.rsqrt(ms + EPS) * gf_ref[...]
        o_ref[...] = y


def _ffn_down(u, x, cw, cb, wd, *, tm, tkf, seq_len, prev=None, g_final=None):
    T, D = x.shape
    F = wd.shape[0]
    nk = F // tkf
    expanded = prev is not None
    final_norm = g_final is not None
    seq_tiles = max(seq_len // tm, 1)
    hb8 = tm // 8
    ua = pl.BlockSpec((tm, tkf), lambda i, k: (i, k))
    ub = pl.BlockSpec((tm, tkf), lambda i, k: (i, nk + k))
    in_specs = [ua, ub]
    args = [u, u]
    if expanded:
        p1, p2, tpos = prev
        in_specs += [ua, ub, ua, ub, pl.BlockSpec((tm, 1), lambda i, k: (i, 0))]
        args += [p1, p1, p2, p2, tpos]
    else:
        in_specs += [pl.BlockSpec((8, tkf), lambda i, k: (jnp.maximum(i * hb8 - 1, 0), k)),
                     pl.BlockSpec((8, tkf), lambda i, k: (jnp.maximum(i * hb8 - 1, 0), nk + k))]
        args += [u, u]
    in_specs += [pl.BlockSpec((3, tkf), lambda i, k: (0, k)), pl.BlockSpec((3, tkf), lambda i, k: (0, nk + k)),
                 pl.BlockSpec((1, tkf), lambda i, k: (0, k)), pl.BlockSpec((1, tkf), lambda i, k: (0, nk + k)),
                 pl.BlockSpec((tkf, D), lambda i, k: (k, 0)), pl.BlockSpec((tm, D), lambda i, k: (i, 0))]
    args += [cw, cw, cb.reshape(1, -1), cb.reshape(1, -1), wd, x]
    if final_norm:
        in_specs.append(pl.BlockSpec((1, D), lambda i, k: (0, 0)))
        args.append(g_final.reshape(1, D))
    return pl.pallas_call(
        functools.partial(_ffn_down_body, tm=tm, seq_tiles=seq_tiles, expanded=expanded, final_norm=final_norm),
        grid=(T // tm, nk),
        in_specs=in_specs,
        out_specs=pl.BlockSpec((tm, D), lambda i, k: (i, 0)),
        out_shape=jax.ShapeDtypeStruct((T, D), F32),
        scratch_shapes=[pltpu.VMEM((tm, D), F32)],
        compiler_params=_cparams(("parallel", "arbitrary")),
        name="ffn_down",
    )(*args)


def _rotate_half_cols(w):
    D, N = w.shape
    w4 = w.reshape(D, N // DH, 2, DH // 2)
    return jnp.concatenate([-w4[:, :, 1], w4[:, :, 0]], axis=-1).reshape(D, N)


def _rope_tables(pos, width):
    half = DH // 2
    inv_freq = ROPE_THETA ** (-jnp.arange(half, dtype=F32) / half)
    ang = pos.astype(F32)[:, None] * inv_freq[None, :]
    reps = width // half
    return jnp.tile(jnp.cos(ang), (1, reps)), jnp.tile(jnp.sin(ang), (1, reps))


def _pick(n, prefs):
    for t in prefs:
        if n % t == 0:
            return t
    return n


def kernel(x_prompt, x_sample, cache_kv_a, cache_kv_c, cache_kidx_c, state_hgrn, state_ffn_conv, page_table,
           norm_mix, w_in, hgrn_lb_logits, hgrn_gnorm, w_br_a, w_br_b, w_br_c, w_out, norm_ffn, w_up, conv_w,
           conv_b, w_down, norm_final):
    B, L, D = x_prompt.shape
    Bs, Ls, _ = x_sample.shape
    depth = w_in.shape[0]
    NP = page_table.shape[1]
    past = NP * PAGE
    F = w_down.shape[1]
    WA = N_HEADS * DH
    WB = N_HEADS * DK
    Tp, Ts = B * L, Bs * Ls
    n_pool = cache_kv_a.shape[1]

    lb_soft = jax.nn.softmax(hgrn_lb_logits.astype(F32), axis=0)
    lb_all = jnp.cumsum(lb_soft, axis=0) - lb_soft[0]

    sizes = (WA, WA, WA, WB, WB, WB, WB, WA, WA, WA, N_HEADS * DH, DH, N_HEADS, 3 * D)
    offs = [0]
    for s in sizes:
        offs.append(offs[-1] + s)
    (o_qa, o_ka, o_va, o_qh, o_fh, o_ih, o_gh, o_qc, o_kc, o_vc, o_qi, o_ki, o_wi, o_gt, o_end) = offs
    QA_BLK, KA_BLK, VA_BLK, VC_BLK = (4 * WB) // WA, (4 * WB) // WA + 1, (4 * WB) // WA + 2, (4 * WB) // WA + 3
    gate_off = 4 * WB + 4 * WA
    assert gate_off % D == 0
    GATE_BLK = gate_off // D
    NPJ = gate_off + 3 * D
    N_ROPE_TILES = 4
    NRJ = (N_ROPE_TILES + 1) * WA
    wi_off = N_ROPE_TILES * WA
    WI_BLK = wi_off // LANES

    cache_a = cache_kv_a.transpose(0, 1, 3, 4, 5, 2).reshape(depth, n_pool, 2 * WA, PAGE)
    cache_c = cache_kv_c.transpose(0, 1, 3, 4, 5, 2).reshape(depth, n_pool, 2 * WA, PAGE)
    cache_i = cache_kidx_c.transpose(0, 1, 3, 2)

    cos_p, sin_p = _rope_tables(jnp.arange(L, dtype=I32), WA)
    cos_s, sin_s = _rope_tables(past + (jnp.arange(Ts, dtype=I32) % Ls), WA)

    tm_p = _pick(Tp, (1024, 512, 256, 128))
    tm_p = min(tm_p, L)
    tn_main = _pick(NPJ, (512, 256, 128))
    tn_up = _pick(2 * F, (512, 256, 128))
    tkf = _pick(F, (1408, 256, 128))
    tq_a = _pick(L, (256, 128))
    tq_c = _pick(L, (256, 128))
    tk_c = _pick(L, (512, 256, 128))
    C_h = _pick(L, (64,))
    G = _pick(NP, (16, 8, 4, 2, 1))
    n_sel_p = min(TOPK_MAX, L // 4)
    n_sel_s = min(TOPK_MAX, (past + Ls) // 4)

    xp = x_prompt.reshape(Tp, D)
    xs = x_sample.reshape(Ts, D)
    tpos_s = (jnp.arange(Ts, dtype=I32) % Ls).reshape(Ts, 1)

    def pad_rows(a, rows):
        return jnp.pad(a, ((0, 0), (0, rows - a.shape[1]), (0, 0)))

    outs_p = ([], [], [], [], [])
    outs_s = ([], [], [], [], [])
    for l in range(depth):
        w = w_in[l]
        col = lambda o, n: w[:, o:o + n]
        w_main = jnp.concatenate(
            [col(o_qh, 4 * WB), col(o_qa, 3 * WA), col(o_vc, WA), col(o_gt, 3 * D)], axis=1).astype(BF16)
        w_r = jnp.concatenate([col(o_qc, WA), col(o_kc, WA), col(o_qi, WA)] + [col(o_ki, DH)] * N_HEADS, axis=1)
        w_tail = jnp.concatenate([col(o_wi, N_HEADS), jnp.zeros((D, WA - N_HEADS), w.dtype)], axis=1)
        w_rot = jnp.concatenate([_rotate_half_cols(w_r), jnp.zeros((D, WA), w.dtype)], axis=1).astype(BF16)
        w_r = jnp.concatenate([w_r, w_tail], axis=1).astype(BF16)
        wa_b, wb_b, wc_b, wo_b = (t[l].astype(BF16) for t in (w_br_a, w_br_b, w_br_c, w_out))
        wup_b = w_up[l].astype(BF16)
        wd_b = w_down[l].astype(BF16)
        last = l == depth - 1

        vc_col = VC_BLK * WA
        pf, pb, vT = _rms_proj(xp, norm_mix[l], w_main, tm=tm_p, tn=tn_main,
                               t_outs=((vc_col // tn_main, vc_col % tn_main, WA, WA, tk_c, BF16),))
        rf, rb, qcT, qiT, wiT = _rms_proj(
            xp, norm_mix[l], w_r, tm=tm_p, tn=WA, rope_args=(w_rot, cos_p, sin_p), n_rope_tiles=N_ROPE_TILES,
            t_outs=((0, 0, WA, WA, tq_c, BF16), (2, 0, WA, WA, tq_c, BF16), (N_ROPE_TILES, 0, LANES, 8, tq_c, F32)))
        oa = _attn_a_prompt(pb, B, L, tq=tq_a, q_blk=QA_BLK, k_blk=KA_BLK, v_blk=VA_BLK)
        ob, s_new = _hgrn(pf.reshape(B, L, NPJ), lb_all[l], hgrn_gnorm[l], None, C=C_h, c=min(16, C_h), valid=C_h)
        oc = _dsa_prompt(qcT, qiT, wiT, rb, vT, B, L, tq=tq_c, tk=tk_c, n_sel=n_sel_p)
        x1 = _merge(oa, ob.reshape(Tp, WB), oc, pf, xp, wa_b, wb_b, wc_b, wo_b, tm=min(512, tm_p), gate_blk=GATE_BLK)
        u = _rms_proj(x1, norm_ffn[l], wup_b, tm=tm_p, tn=tn_up, emit_bf16=False)
        xp = _ffn_down(u, x1, conv_w[l], conv_b[l], wd_b, tm=min(512, tm_p), tkf=tkf, seq_len=L,
                       g_final=norm_final if last else None)
        kv_off = (KA_BLK * WA)
        outs_p[0].append(pf[:, kv_off:kv_off + 2 * WA].reshape(B, L, 2, N_HEADS, DH))
        outs_p[1].append(jnp.stack([rf[:, WA:2 * WA].reshape(B, L, N_HEADS, DH),
                                    pf[:, VC_BLK * WA:(VC_BLK + 1) * WA].reshape(B, L, N_HEADS, DH)], axis=2))
        outs_p[2].append(rf[:, 3 * WA:3 * WA + DH].reshape(B, L, DH))
        outs_p[3].append(s_new)
        outs_p[4].append(u.reshape(B, L, 2 * F)[:, L - 2:])

        pf, pb = _rms_proj(xs, norm_mix[l], w_main, tm=Ts, tn=tn_main)
        rf, rb = _rms_proj(xs, norm_mix[l], w_r, tm=Ts, tn=WA, rope_args=(w_rot, cos_s, sin_s),
                           n_rope_tiles=N_ROPE_TILES)
        pb3 = pb.reshape(Bs, Ls, NPJ)
        rb3 = rb.reshape(Bs, Ls, NRJ)
        blk = lambda a, k: a[:, :, k * WA:(k + 1) * WA]
        oa8 = _attn_a_sample(pad_rows(blk(pb3, QA_BLK), 8), pad_rows(blk(pb3, KA_BLK), PAGE),
                             pad_rows(blk(pb3, VA_BLK), PAGE), cache_a, l, page_table, G=G)
        ph = jnp.pad(pf[:, :4 * WB].reshape(Bs, Ls, 4 * WB), ((0, 0), (0, 8 - Ls), (0, 0)))
        ob8, s_new = _hgrn(ph, lb_all[l], hgrn_gnorm[l], state_hgrn[l], C=8, c=8, valid=Ls)
        qi = blk(rb3, 2).reshape(Bs, Ls, N_HEADS, DH).transpose(0, 2, 1, 3)
        qi32 = jnp.pad(qi, ((0, 0), (0, 0), (0, 8 - Ls), (0, 0))).reshape(Bs, 4 * 8, DH)
        wi = rf[:, wi_off:wi_off + N_HEADS].reshape(Bs, Ls, N_HEADS).transpose(0, 2, 1) * (N_HEADS ** -0.5)
        w32 = jnp.broadcast_to(jnp.pad(wi, ((0, 0), (0, 0), (0, 8 - Ls))).reshape(Bs, 32, 1), (Bs, 32, LANES))
        kin = pad_rows(rb3[:, :, 3 * WA:3 * WA + DH], PAGE)
        qc8 = pad_rows(blk(rb3, 0), 8)
        head = (jnp.arange(WA) // DH)[None, None, None, :] == jnp.arange(N_HEADS)[None, :, None, None]
        qc32 = (jnp.where(head, qc8[:, None], 0) * jnp.asarray(DH ** -0.5, BF16)).reshape(Bs, 32, WA).astype(BF16)
        oc8 = _dsa_sample(qi32, w32, kin, qc32, pad_rows(blk(rb3, 1), PAGE), pad_rows(blk(pb3, VC_BLK), PAGE),
                          cache_i, cache_c, l, page_table, G=G, n_new=Ls, n_sel=n_sel_s)
        x1 = _merge(oa8[:, :Ls].reshape(Ts, WA), ob8[:, :Ls].reshape(Ts, WB), oc8[:, :Ls].reshape(Ts, WA), pf, xs,
                    wa_b, wb_b, wc_b, wo_b, tm=Ts, gate_blk=GATE_BLK)
        u = _rms_proj(x1, norm_ffn[l], wup_b, tm=Ts, tn=tn_up, emit_bf16=False)
        prev = state_ffn_conv[l]
        u3 = u.reshape(Bs, Ls, 2 * F)
        p1 = jnp.broadcast_to(prev[:, 1:2], (Bs, Ls, 2 * F)).reshape(Ts, 2 * F)
        p2 = jnp.concatenate([prev, jnp.zeros((Bs, Ls - 2, 2 * F), F32)], axis=1).reshape(Ts, 2 * F)
        xs = _ffn_down(u, x1, conv_w[l], conv_b[l], wd_b, tm=Ts, tkf=tkf, seq_len=Ls, prev=(p1, p2, tpos_s),
                       g_final=norm_final if last else None)
        kv_off = (KA_BLK * WA)
        outs_s[0].append(pf[:, kv_off:kv_off + 2 * WA].reshape(Bs, Ls, 2, N_HEADS, DH))
        outs_s[1].append(jnp.stack([rf[:, WA:2 * WA].reshape(Bs, Ls, N_HEADS, DH),
                                    pf[:, VC_BLK * WA:(VC_BLK + 1) * WA].reshape(Bs, Ls, N_HEADS, DH)], axis=2))
        outs_s[2].append(rf[:, 3 * WA:3 * WA + DH].reshape(Bs, Ls, DH))
        outs_s[3].append(s_new)
        outs_s[4].append(jnp.concatenate([prev, u3], axis=1)[:, Ls:])

    st = lambda xs_: jnp.stack(xs_, axis=0)
    return (xp.reshape(B, L, D), xs.reshape(Bs, Ls, D),
            st(outs_p[0]), st(outs_s[0]), st(outs_p[1]), st(outs_s[1]), st(outs_p[2]), st(outs_s[2]),
            st(outs_p[3]), st(outs_s[3]), st(outs_p[4]), st(outs_s[4]))
```

```python
import functools

import jax
import jax.numpy as jnp
from jax import lax
from jax.experimental import pallas as pl
from jax.experimental.pallas import tpu as pltpu

F32 = jnp.float32
BF16 = jnp.bfloat16
I32 = jnp.int32

EPS = 1e-6
NEG_BIG = -1e30
LOG_F_MIN = -30.0
ROPE_THETA = 10000.0
TOPK_MAX = 256
PAGE = 128
N_HEADS = 4
DH = 64
DK = 128
LANES = 128
VMEM_LIMIT = 56 * 1024 * 1024
INT_MIN = -(2 ** 31)
EXP_UNDERFLOW = -110.0


def _cparams(sem):
    return pltpu.CompilerParams(dimension_semantics=sem, vmem_limit_bytes=VMEM_LIMIT)


def _dot(a, b):
    return jnp.dot(a, b, preferred_element_type=F32)


def _dot_nt(a, b):
    return lax.dot_general(a, b, (((1,), (1,)), ((), ())), preferred_element_type=F32)


def _dot_tn(a, b):
    return lax.dot_general(a, b, (((0,), (0,)), ((), ())), preferred_element_type=F32)


def _silu(x):
    return x * jax.nn.sigmoid(x)


def _stack_heads(q, rows):
    head = lax.broadcasted_iota(I32, (rows, N_HEADS * DH), 1) // DH
    return jnp.concatenate([jnp.where(head == h, q, jnp.zeros_like(q)) for h in range(N_HEADS)], axis=0)


def _unstack_heads(acc, rows):
    head = lax.broadcasted_iota(I32, (rows, N_HEADS * DH), 1) // DH
    out = jnp.zeros((rows, N_HEADS * DH), acc.dtype)
    for h in range(N_HEADS):
        out = jnp.where(head == h, acc[h * rows:(h + 1) * rows], out)
    return out


def _proj_body(*refs, rope, emit_bf16, n_rope_tiles, t_outs):
    x_ref, g_ref, w_ref = refs[:3]
    rest = refs[3:]
    if rope:
        wrot_ref, cos_ref, sin_ref = rest[:3]
        rest = rest[3:]
    of_ref = rest[0]
    ob_ref = rest[1] if emit_bf16 else None
    t_refs = rest[(2 if emit_bf16 else 1):-1]
    h_ref = rest[-1]

    @pl.when(pl.program_id(1) == 0)
    def _():
        x = x_ref[...]
        ms = jnp.mean(x * x, axis=-1, keepdims=True)
        h_ref[...] = (x * lax.rsqrt(ms + EPS) * g_ref[...]).astype(BF16)

    h = h_ref[...]
    acc = _dot(h, w_ref[...])
    if rope:
        roped = acc * cos_ref[...] + _dot(h, wrot_ref[...]) * sin_ref[...]
        acc = jnp.where(pl.program_id(1) < n_rope_tiles, roped, acc)
    of_ref[...] = acc
    if emit_bf16:
        ob_ref[...] = acc.astype(BF16)
    for t_ref, (j_tile, c0, width, keep, sub, dtype) in zip(t_refs, t_outs):
        @pl.when(pl.program_id(1) == j_tile)
        def _(t_ref=t_ref, c0=c0, width=width, keep=keep, sub=sub, dtype=dtype):
            for r in range(acc.shape[0] // sub):
                t_ref[r] = acc[r * sub:(r + 1) * sub, c0:c0 + width].T[:keep].astype(dtype)


def _rms_proj(x, g, w, *, tm, tn, rope_args=None, emit_bf16=True, n_rope_tiles=0, t_outs=()):
    T, D = x.shape
    N = w.shape[1]
    rope = rope_args is not None
    in_specs = [pl.BlockSpec((tm, D), lambda i, j: (i, 0)),
                pl.BlockSpec((1, D), lambda i, j: (0, 0)),
                pl.BlockSpec((D, tn), lambda i, j: (0, j))]
    args = [x, g.reshape(1, D), w]
    if rope:
        w_rot, cos, sin = rope_args
        nblk = cos.shape[0] // tm
        in_specs += [pl.BlockSpec((D, tn), lambda i, j: (0, j)),
                     pl.BlockSpec((tm, tn), lambda i, j: (i % nblk, 0)),
                     pl.BlockSpec((tm, tn), lambda i, j: (i % nblk, 0))]
        args += [w_rot, cos, sin]
    out_shape = [jax.ShapeDtypeStruct((T, N), F32)]
    out_specs = [pl.BlockSpec((tm, tn), lambda i, j: (i, j))]
    if emit_bf16:
        out_shape.append(jax.ShapeDtypeStruct((T, N), BF16))
        out_specs.append(pl.BlockSpec((tm, tn), lambda i, j: (i, j)))
    for (_, _, _, keep, sub, dtype) in t_outs:
        out_shape.append(jax.ShapeDtypeStruct((T // sub, keep, sub), dtype))
        out_specs.append(pl.BlockSpec((tm // sub, keep, sub), lambda i, j: (i, 0, 0)))
    outs = pl.pallas_call(
        functools.partial(_proj_body, rope=rope, emit_bf16=emit_bf16, n_rope_tiles=n_rope_tiles, t_outs=tuple(t_outs)),
        grid=(T // tm, N // tn),
        in_specs=in_specs, out_specs=out_specs, out_shape=out_shape,
        scratch_shapes=[pltpu.VMEM((tm, D), BF16)],
        compiler_params=_cparams(("parallel", "arbitrary")),
        name="rms_proj_rope" if rope else "rms_proj",
    )(*args)
    return outs if (emit_bf16 or t_outs) else outs[0]


def _suffix_matrix():
    j = lax.broadcasted_iota(I32, (2 * LANES, LANES), 0) & (LANES - 1)
    s = lax.broadcasted_iota(I32, (2 * LANES, LANES), 1)
    return jnp.where(j > s, 1.0, 0.0).astype(BF16)


def _log_one_minus_beta(z):
    return -(jnp.maximum(z, 0.0) + jnp.log(1.0 + jnp.exp(-jnp.abs(z))))


def _sb_update(z, carry, suffix, vis):
    M, tk = z.shape
    ls = _log_one_minus_beta(z)
    if vis is not None:
        ls = jnp.where(vis, ls, 0.0)
    hi = ls.astype(BF16)
    lo = (ls - hi.astype(F32)).astype(BF16)
    lz = z + ls
    n = tk // LANES
    outs = [None] * n
    for g in reversed(range(n)):
        sl = slice(g * LANES, (g + 1) * LANES)
        between = _dot(jnp.concatenate([hi[:, sl], lo[:, sl]], axis=1), suffix)
        e = lz[:, sl] + between + carry
        if vis is not None:
            e = jnp.where(vis[:, sl], e, NEG_BIG)
        outs[g] = jnp.exp(e)
        carry = carry + jnp.sum(ls[:, sl], axis=1, keepdims=True)
    a = outs[0] if n == 1 else jnp.concatenate(outs, axis=1)
    return a.astype(BF16), carry


def _attn_a_prompt_body(q_ref, k_ref, v_ref, o_ref, *, tq):
    i = pl.program_id(1)
    M = N_HEADS * tq
    qs = _stack_heads(q_ref[...], tq) * jnp.asarray(DH ** -0.5, BF16)
    suffix = _suffix_matrix()

    def block(k0, carry, acc, vis):
        kb = k_ref[pl.ds(k0, tq), :]
        vb = v_ref[pl.ds(k0, tq), :]
        a, carry = _sb_update(_dot_nt(qs, kb), carry, suffix, vis)
        return carry, acc + _dot(a, vb)

    row_t = lax.broadcasted_iota(I32, (M, tq), 0) & (tq - 1)
    col_s = lax.broadcasted_iota(I32, (M, tq), 1)
    carry, acc = block(pl.multiple_of(i * tq, tq), jnp.zeros((M, 1), F32),
                       jnp.zeros((M, N_HEADS * DH), F32), col_s < row_t)

    def alive(carry):
        return (jnp.max(carry) > EXP_UNDERFLOW).astype(I32)

    def cond(c):
        return (c[0] < i) & (c[1] > 0)

    def body(c):
        step = c[0]
        k0 = pl.multiple_of((i - 1 - step) * tq, tq)
        carry, acc = block(k0, c[2], c[3], None)
        return step + 1, alive(carry), carry, acc

    _, _, carry, acc = lax.while_loop(cond, body, (jnp.int32(0), alive(carry), carry, acc))
    o_ref[...] = _unstack_heads(acc, tq).astype(o_ref.dtype)


def _attn_a_prompt(pb, B, L, *, tq, q_blk, k_blk, v_blk):
    W = N_HEADS * DH
    nq = L // tq
    return pl.pallas_call(
        functools.partial(_attn_a_prompt_body, tq=tq),
        grid=(B, nq),
        in_specs=[pl.BlockSpec((tq, W), lambda b, i: (b * nq + i, q_blk)),
                  pl.BlockSpec((L, W), lambda b, i: (b, k_blk)),
                  pl.BlockSpec((L, W), lambda b, i: (b, v_blk))],
        out_specs=pl.BlockSpec((tq, W), lambda b, i: (b * nq + i, 0)),
        out_shape=jax.ShapeDtypeStruct((B * L, W), BF16),
        compiler_params=_cparams(("parallel", "arbitrary")),
        name="attn_a_prompt",
    )(pb, pb, pb)


def _attn_a_sample_body(*refs, G, resume):
    n_pref = 2 if resume else 1
    q_ref, a_ref, b_ref = refs[n_pref:n_pref + 3]
    pages = refs[n_pref + 3:n_pref + 3 + G]
    rest = refs[n_pref + 3 + G:]
    if resume:
        o_ref, carry_ref, acc_ref = rest
    else:
        o_ref, carry_out_ref, acc_out_ref, carry_ref, acc_ref = rest
    p = pl.program_id(1)
    W = N_HEADS * DH
    R = 8
    M = N_HEADS * R
    qs = _stack_heads(q_ref[0], R) * jnp.asarray(DH ** -0.5, BF16)
    suffix = _suffix_matrix()

    @pl.when(p == 0)
    def _():
        if resume:
            carry_ref[...] = a_ref[0]
            acc_ref[...] = b_ref[0]
        else:
            row_t = lax.broadcasted_iota(I32, (M, PAGE), 0) & (R - 1)
            col_s = lax.broadcasted_iota(I32, (M, PAGE), 1)
            a, carry = _sb_update(_dot_nt(qs, a_ref[0]), jnp.zeros((M, 1), F32), suffix, col_s < row_t)
            carry_ref[...] = jnp.broadcast_to(carry, carry_ref.shape)
            acc_ref[...] = _dot(a, b_ref[0])

    @pl.when(jnp.max(carry_ref[...]) > EXP_UNDERFLOW)
    def _():
        kcat = jnp.concatenate([pages[g][:W, :] for g in reversed(range(G))], axis=1).astype(BF16)
        vcat = jnp.concatenate([pages[g][W:, :] for g in reversed(range(G))], axis=1).astype(BF16)
        a, carry = _sb_update(_dot(qs, kcat), carry_ref[:, 0:1], suffix, None)
        carry_ref[...] = jnp.broadcast_to(carry, carry_ref.shape)
        acc_ref[...] += _dot_nt(a, vcat)

    @pl.when(p == pl.num_programs(1) - 1)
    def _():
        o_ref[0] = _unstack_heads(acc_ref[...], R).astype(o_ref.dtype)
        if not resume:
            carry_out_ref[0] = carry_ref[...]
            acc_out_ref[0] = acc_ref[...]


def _attn_a_sample(q8, kn, vn, cache, layer, page_table, *, G):
    Bs = q8.shape[0]
    NP = page_table.shape[1]
    W = N_HEADS * DH
    M = N_HEADS * 8
    nstep = NP // G
    scratch = [pltpu.VMEM((M, LANES), F32), pltpu.VMEM((M, W), F32)]
    row3 = lambda *a: (a[0], 0, 0)

    def head_page(g):
        return pl.BlockSpec((None, None, 2 * W, PAGE), lambda b, p, pt: (layer, pt[b, NP - 1 - g], 0, 0))

    o, carry, acc = pl.pallas_call(
        functools.partial(_attn_a_sample_body, G=G, resume=False),
        grid_spec=pltpu.PrefetchScalarGridSpec(
            num_scalar_prefetch=1, grid=(Bs, 1),
            in_specs=[pl.BlockSpec((1, 8, W), row3), pl.BlockSpec((1, PAGE, W), row3),
                      pl.BlockSpec((1, PAGE, W), row3)] + [head_page(g) for g in range(G)],
            out_specs=[pl.BlockSpec((1, 8, W), row3), pl.BlockSpec((1, M, LANES), row3), pl.BlockSpec((1, M, W), row3)],
            scratch_shapes=scratch),
        out_shape=[jax.ShapeDtypeStruct((Bs, 8, W), BF16), jax.ShapeDtypeStruct((Bs, M, LANES), F32),
                   jax.ShapeDtypeStruct((Bs, M, W), F32)],
        compiler_params=_cparams(("parallel", "arbitrary")),
        name="attn_a_sample",
    )(page_table, q8, kn, vn, *([cache] * G))
    if nstep == 1:
        return o
    alive = (jnp.max(carry, axis=(1, 2)) > EXP_UNDERFLOW).astype(I32)

    def tail_page(g):
        return pl.BlockSpec((None, None, 2 * W, PAGE),
                            lambda b, p, pt, al: (layer, jnp.where(al[b] > 0, pt[b, NP - 1 - ((p + 1) * G + g)], 0), 0, 0))

    tail = pl.pallas_call(
        functools.partial(_attn_a_sample_body, G=G, resume=True),
        grid_spec=pltpu.PrefetchScalarGridSpec(
            num_scalar_prefetch=2, grid=(Bs, nstep - 1),
            in_specs=[pl.BlockSpec((1, 8, W), row3), pl.BlockSpec((1, M, LANES), row3),
                      pl.BlockSpec((1, M, W), row3)] + [tail_page(g) for g in range(G)],
            out_specs=pl.BlockSpec((1, 8, W), row3),
            scratch_shapes=scratch),
        out_shape=jax.ShapeDtypeStruct((Bs, 8, W), BF16),
        compiler_params=_cparams(("parallel", "arbitrary")),
        name="attn_a_sample_tail",
    )
    return lax.cond(jnp.any(alive > 0), lambda: tail(page_table, alive, q8, carry, acc, *([cache] * G)), lambda: o)


def _cumsum_rows(x):
    C = x.shape[0]
    row = lax.broadcasted_iota(I32, x.shape, 0)
    sh = 1
    while sh < C:
        x = x + jnp.where(row >= sh, pltpu.roll(x, sh, axis=0), 0.0)
        sh *= 2
    return x


def _hgrn_body(*refs, C, c, valid, has_s0):
    q_ref, f_ref, i_ref, g_ref, lb_ref, gn_ref = refs[:6]
    rest = refs[6:]
    if has_s0:
        s0_ref = rest[0]
        rest = rest[1:]
    o_ref, sout_ref, st_ref = rest
    ci = pl.program_id(1)

    @pl.when(ci == 0)
    def _():
        for h in range(N_HEADS):
            st_ref[h] = s0_ref[0, h].T if has_s0 else jnp.zeros((DK, DK), F32)

    row = lax.broadcasted_iota(I32, (C, DK), 0)
    rowc = lax.broadcasted_iota(I32, (c, 1), 0)
    for h in range(N_HEADS):
        hs = slice(h * DK, (h + 1) * DK)
        kk = (1.0 - lb_ref[:, hs]) * jax.nn.sigmoid(-f_ref[0, :, hs])
        lg = jnp.maximum(jnp.log1p(-kk), LOG_F_MIN)
        if valid < C:
            kk = jnp.where(row < valid, kk, 0.0)
            lg = jnp.where(row < valid, lg, 0.0)
        qq = _silu(q_ref[0, :, hs])
        vv = i_ref[0, :, hs]
        cum = _cumsum_rows(lg)
        st = st_ref[h]
        o = _dot_nt((qq * jnp.exp(cum)).astype(BF16), st.astype(BF16))
        parts = []
        for blk in range(C // c):
            r0 = blk * c
            q_b = qq[r0:r0 + c]
            cum_b = cum[r0:r0 + c]
            o_b = jnp.zeros((c, DK), F32)
            if blk > 0:
                base = cum[r0 - 1:r0]
                qt = q_b * jnp.exp(cum_b - base)
                kt = kk[:r0] * jnp.exp(base - cum[:r0])
                sc = _dot_nt(qt.astype(BF16), kt.astype(BF16))
                o_b = o_b + _dot(sc.astype(BF16), vv[:r0].astype(BF16))
            for s in range(c):
                r = r0 + s
                d = jnp.minimum(cum_b - cum[r:r + 1], 0.0)
                w = jnp.sum(q_b * kk[r:r + 1] * jnp.exp(d), axis=1, keepdims=True)
                o_b = o_b + jnp.where(rowc >= s, w, 0.0) * vv[r:r + 1]
            parts.append(o_b)
        o = o + (parts[0] if len(parts) == 1 else jnp.concatenate(parts, axis=0))
        last = cum[C - 1:C]
        kd = kk * jnp.exp(last - cum)
        st_ref[h] = st * jnp.exp(last) + _dot_tn(vv.astype(BF16), kd.astype(BF16))
        ms = jnp.mean(o * o, axis=1, keepdims=True)
        y = o * lax.rsqrt(ms + EPS) * gn_ref[...] * _silu(g_ref[0, :, hs])
        o_ref[0, :, hs] = y.astype(o_ref.dtype)

    @pl.when(ci == pl.num_programs(1) - 1)
    def _():
        for h in range(N_HEADS):
            sout_ref[0, h] = st_ref[h].T


def _hgrn(p3, lb, gn, s0, *, C, c, valid, col0=0):
    B, L, _ = p3.shape
    W = N_HEADS * DK
    has_s0 = s0 is not None
    in_specs = [pl.BlockSpec((1, C, W), functools.partial(lambda b, ci, k: (b, ci, col0 + k), k=k)) for k in range(4)]
    in_specs += [pl.BlockSpec((1, W), lambda b, ci: (0, 0)), pl.BlockSpec((1, DK), lambda b, ci: (0, 0))]
    args = [p3, p3, p3, p3, lb.reshape(1, W), gn.reshape(1, DK)]
    if has_s0:
        in_specs.append(pl.BlockSpec((1, N_HEADS, DK, DK), lambda b, ci: (b, 0, 0, 0)))
        args.append(s0)
    return pl.pallas_call(
        functools.partial(_hgrn_body, C=C, c=c, valid=valid, has_s0=has_s0),
        grid=(B, L // C),
        in_specs=in_specs,
        out_specs=[pl.BlockSpec((1, C, W), lambda b, ci: (b, ci, 0)),
                   pl.BlockSpec((1, N_HEADS, DK, DK), lambda b, ci: (b, 0, 0, 0))],
        out_shape=[jax.ShapeDtypeStruct((B, L, W), BF16), jax.ShapeDtypeStruct((B, N_HEADS, DK, DK), F32)],
        scratch_shapes=[pltpu.VMEM((N_HEADS, DK, DK), F32)],
        compiler_params=_cparams(("parallel", "arbitrary")),
        name="hgrn2",
    )(*args)


def _sort_key(score):
    b = pltpu.bitcast(score, I32)
    return jnp.where(b < 0, b ^ jnp.int32(0x7FFFFFFF), b)


def _prefix_matrix(n):
    j = lax.broadcasted_iota(I32, (n, n + LANES), 0)
    s = lax.broadcasted_iota(I32, (n, n + LANES), 1)
    return jnp.where((s >= n) | (j <= s), 1.0, 0.0).astype(BF16)


def _kth_largest_key(count_ge, rows, n_sel):
    def body(t, T):
        cand = T + jnp.left_shift(jnp.int32(1), 31 - t)
        return jnp.where(count_ge(cand) >= n_sel, cand, T)
    return lax.fori_loop(0, 32, body, jnp.full((rows, 1), INT_MIN, I32))


def _dsa_prompt_body(qcT_ref, qiT_ref, wiT_ref, kc_ref, ki_ref, vT_ref, lt_ref, o_ref, keys_ref, kh_ref, kl_ref,
                     *, tq, tk, n_sel):
    i = pl.program_id(1)
    nkb = ((i + 1) * tq + tk - 1) // tk
    M = N_HEADS * tq
    W = N_HEADS * DH
    I16 = jnp.int16
    kpos0 = lax.broadcasted_iota(I32, (tk, tq), 0)
    qpos = i * tq + lax.broadcasted_iota(I32, (tk, tq), 1)
    row_head = lax.broadcasted_iota(I32, (W, tq), 0) // DH

    def stack_lanes(qT):
        return jnp.concatenate([jnp.where(row_head == h, qT, jnp.zeros_like(qT)) for h in range(N_HEADS)], axis=1)

    qisT = stack_lanes(qiT_ref[0])
    wi = wiT_ref[0]
    w_row = jnp.concatenate([wi[j:j + 1, :] for j in range(N_HEADS)], axis=1) * (N_HEADS ** -0.5) * (DH ** -0.5)

    def score_body(j, _):
        k0 = pl.multiple_of(j * tk, tk)
        d = jnp.maximum(_dot(ki_ref[pl.ds(k0, tk), :], qisT), 0.0) * w_row
        score = d[:, 0:tq]
        for jh in range(1, N_HEADS):
            score = score + d[:, jh * tq:(jh + 1) * tq]
        score = jnp.where(kpos0 + k0 <= qpos, score + 0.0, NEG_BIG)
        key = _sort_key(score)
        keys_ref[j] = key
        kh_ref[j] = (key >> 16).astype(I16)
        kl_ref[j] = ((key & 0xFFFF) - 32768).astype(I16)
        return 0

    lax.fori_loop(0, nkb, score_body, 0)

    def count16(ref, cand):
        c16 = cand.astype(I16)

        def body(j, acc):
            m = jnp.where(ref[j] >= c16, jnp.ones((), BF16), jnp.zeros((), BF16))
            parts = [m[16 * r:16 * (r + 1)] for r in range(tk // 16)]
            while len(parts) > 1:
                parts = [parts[a] + parts[a + 1] for a in range(0, len(parts) - 1, 2)] + parts[len(parts) & ~1:]
            return acc + parts[0].astype(F32)

        acc = lax.fori_loop(0, nkb, body, jnp.zeros((16, tq), F32))
        return jnp.sum(acc, axis=0, keepdims=True)

    def kth16(ref, target):
        def body(t, T):
            cand = T + jnp.left_shift(jnp.int32(1), 15 - t)
            return jnp.where(count16(ref, cand) >= target, cand, T)
        return lax.fori_loop(0, 16, body, jnp.full((1, tq), -32768, I32))

    def count_above(ref, T):
        return jnp.where(T >= 32767, 0.0, count16(ref, jnp.minimum(T + 1, 32767)))

    TH = kth16(kh_ref, float(n_sel))
    need = n_sel - count_above(kh_ref, TH)
    th16 = TH.astype(I16)

    def low_body(j, _):
        kl_ref[j] = jnp.where(kh_ref[j] == th16, kl_ref[j], jnp.full((), -32768, I16))
        return 0

    lax.fori_loop(0, nkb, low_body, 0)
    TL = kth16(kl_ref, need)
    room = need - count_above(kl_ref, TL)
    T = TH * 65536 + (TL + 32768)

    qcsT = stack_lanes(qcT_ref[0]) * jnp.asarray(DH ** -0.5, BF16)
    lt = lt_ref[...]

    def att_body(j, c):
        m, l, accT, eq_before = c
        k0 = pl.multiple_of(j * tk, tk)
        key = keys_ref[j]
        eq = key == T
        rank = _dot(lt, jnp.where(eq, 1.0, 0.0).astype(BF16)) + eq_before
        sel = (key > T) | (eq & (rank <= room))
        bias = jnp.where(sel & (kpos0 + k0 <= qpos), 0.0, NEG_BIG)
        s = _dot(kc_ref[pl.ds(k0, tk), :], qcsT) + jnp.concatenate([bias] * N_HEADS, axis=1)
        m_new = jnp.maximum(m, jnp.max(s, axis=0, keepdims=True))
        alpha = jnp.exp(m - m_new)
        p = jnp.exp(s - m_new)
        l = alpha * l + jnp.sum(p, axis=0, keepdims=True)
        accT = alpha * accT + _dot(vT_ref[j], p.astype(BF16))
        return m_new, l, accT, rank[tk - 1:tk, :]

    init = (jnp.full((1, M), NEG_BIG, F32), jnp.zeros((1, M), F32), jnp.zeros((W, M), F32), jnp.zeros((1, tq), F32))
    m, l, accT, _ = lax.fori_loop(0, nkb, att_body, init)
    outT = accT / l
    oT = jnp.zeros((W, tq), F32)
    for h in range(N_HEADS):
        oT = jnp.where(row_head == h, outT[:, h * tq:(h + 1) * tq], oT)
    o_ref[...] = oT.T.astype(o_ref.dtype)


def _dsa_prompt(qcT, qiT, wiT, rb, vT, B, L, *, tq, tk, n_sel):
    W = N_HEADS * DH
    nq = L // tq
    assert tk // 16 <= 256
    lt = jnp.where(jnp.arange(tk)[None, :] <= jnp.arange(tk)[:, None], 1.0, 0.0).astype(BF16)
    return pl.pallas_call(
        functools.partial(_dsa_prompt_body, tq=tq, tk=tk, n_sel=n_sel),
        grid=(B, nq),
        in_specs=[pl.BlockSpec((1, W, tq), lambda b, i: (b * nq + i, 0, 0)),
                  pl.BlockSpec((1, W, tq), lambda b, i: (b * nq + i, 0, 0)),
                  pl.BlockSpec((1, 8, tq), lambda b, i: (b * nq + i, 0, 0)),
                  pl.BlockSpec((L, W), lambda b, i: (b, 1)),
                  pl.BlockSpec((L, W), lambda b, i: (b, 3)),
                  pl.BlockSpec((L // tk, W, tk), lambda b, i: (b, 0, 0)),
                  pl.BlockSpec((tk, tk), lambda b, i: (0, 0))],
        out_specs=pl.BlockSpec((tq, W), lambda b, i: (b * nq + i, 0)),
        out_shape=jax.ShapeDtypeStruct((B * L, W), BF16),
        scratch_shapes=[pltpu.VMEM((L // tk, tk, tq), I32), pltpu.VMEM((L // tk, tk, tq), jnp.int16),
                        pltpu.VMEM((L // tk, tk, tq), jnp.int16)],
        compiler_params=_cparams(("parallel", "arbitrary")),
        name="dsa_prompt",
    )(qcT, qiT, wiT, rb, rb, vT, lt)


def _dsa_s_score_body(pt_ref, qi_ref, w_ref, kn_ref, *rest, G, n_new):
    pages = rest[:G]
    o_ref = rest[G]
    p = pl.program_id(1)
    R = 8
    qi = qi_ref[0]
    w = w_ref[0]

    def score(dots):
        d = jnp.maximum(dots * (DH ** -0.5), 0.0) * w
        return (d[0:R] + d[R:2 * R]) + (d[2 * R:3 * R] + d[3 * R:4 * R]) + 0.0

    @pl.when(p < pl.num_programs(1) - 1)
    def _():
        for g in range(G):
            o_ref[0, :, g * PAGE:(g + 1) * PAGE] = score(_dot(qi, pages[g][...].astype(BF16)))

    @pl.when(p == pl.num_programs(1) - 1)
    def _():
        t = lax.broadcasted_iota(I32, (R, PAGE), 0)
        s = lax.broadcasted_iota(I32, (R, PAGE), 1)
        o_ref[0, :, 0:PAGE] = jnp.where((s <= t) & (s < n_new), score(_dot_nt(qi, kn_ref[0])), NEG_BIG)
        if G > 1:
            o_ref[0, :, PAGE:] = jnp.full((R, (G - 1) * PAGE), NEG_BIG, F32)


def _dsa_s_select_body(s_ref, o_ref, *, n_sel):
    R = 8
    key = _sort_key(s_ref[0])
    W = key.shape[1]

    def count_ge(cand):
        m = jnp.where(key >= cand, 1.0, 0.0)
        parts = [m[:, g * LANES:(g + 1) * LANES] for g in range(W // LANES)]
        while len(parts) > 1:
            parts = [parts[a] + parts[a + 1] for a in range(0, len(parts) - 1, 2)] + parts[len(parts) & ~1:]
        return jnp.sum(parts[0], axis=1, keepdims=True)

    T = _kth_largest_key(count_ge, R, n_sel)
    room = n_sel - count_ge(T + 1)
    eq = key == T
    gt = key > T
    nt = W // LANES
    eqf = jnp.where(eq, 1.0, 0.0)
    stacked = jnp.concatenate([eqf[:, g * LANES:(g + 1) * LANES] for g in range(nt)], axis=0).astype(BF16)
    pc = _dot(stacked, _prefix_matrix(LANES))
    before = jnp.zeros((R, LANES), F32)
    tiles = []
    for g in range(nt):
        sl = slice(g * LANES, (g + 1) * LANES)
        rank = pc[g * R:(g + 1) * R, :LANES] + before
        tiles.append(jnp.where(gt[:, sl], 1.0, jnp.where(eq[:, sl] & (rank <= room), 1.0, 0.0)))
        before = before + pc[g * R:(g + 1) * R, LANES:]
    o_ref[0] = jnp.concatenate(tiles, axis=1)


def _dsa_s_attend_body(pt_ref, q_ref, kn_ref, vn_ref, mask_ref, maskn_ref, *rest, G, n_new):
    pages = rest[:G]
    o_ref, m_ref, l_ref, acc_ref = rest[G:]
    p = pl.program_id(1)
    W = N_HEADS * DH
    R = 8
    qs = q_ref[0]

    def update(s, v, m, l, acc, v_feature_major=False):
        m_new = jnp.maximum(m, jnp.max(s, axis=1, keepdims=True))
        alpha = jnp.exp(m - m_new)
        pr = jnp.exp(s - m_new)
        pv = _dot_nt(pr.astype(BF16), v) if v_feature_major else _dot(pr.astype(BF16), v)
        return m_new, alpha * l + jnp.sum(pr, axis=1, keepdims=True), alpha * acc + pv

    @pl.when(p == 0)
    def _():
        t = lax.broadcasted_iota(I32, (R, PAGE), 0)
        sidx = lax.broadcasted_iota(I32, (R, PAGE), 1)
        sel = jnp.where((sidx <= t) & (sidx < n_new), maskn_ref[0, :, 0:PAGE], 0.0)
        s = jnp.where(jnp.concatenate([sel] * N_HEADS, axis=0) > 0.5, _dot_nt(qs, kn_ref[0]), NEG_BIG)
        m, l, acc = update(s, vn_ref[0], jnp.full((N_HEADS * R, 1), NEG_BIG, F32),
                           jnp.zeros((N_HEADS * R, 1), F32), jnp.zeros((N_HEADS * R, W), F32))
        m_ref[...] = jnp.broadcast_to(m, m_ref.shape)
        l_ref[...] = jnp.broadcast_to(l, l_ref.shape)
        acc_ref[...] = acc

    kcat = jnp.concatenate([pages[g][:W, :] for g in range(G)], axis=1).astype(BF16)
    vcat = jnp.concatenate([pages[g][W:, :] for g in range(G)], axis=1).astype(BF16)
    sel = jnp.concatenate([mask_ref[0]] * N_HEADS, axis=0)
    s = jnp.where(sel > 0.5, _dot(qs, kcat), NEG_BIG)
    m, l, acc = update(s, vcat, m_ref[:, 0:1], l_ref[:, 0:1], acc_ref[...], v_feature_major=True)
    m_ref[...] = jnp.broadcast_to(m, m_ref.shape)
    l_ref[...] = jnp.broadcast_to(l, l_ref.shape)
    acc_ref[...] = acc

    @pl.when(p == pl.num_programs(1) - 1)
    def _():
        o_ref[0] = _unstack_heads(acc_ref[...] / l_ref[:, 0:1], R).astype(o_ref.dtype)


def _dsa_sample(qi32, w32, kin, qc32, kcn, vcn, cache_kidx, cache_kv, layer, page_table, *, G, n_new, n_sel):
    Bs = qi32.shape[0]
    NP = page_table.shape[1]
    W = N_HEADS * DH
    nstep = NP // G
    width = (nstep + 1) * G * PAGE

    def kidx_spec(g):
        return pl.BlockSpec((None, None, DH, PAGE),
                            lambda b, p, pt: (layer, pt[b, jnp.minimum(p * G + g, NP - 1)], 0, 0))

    scores = pl.pallas_call(
        functools.partial(_dsa_s_score_body, G=G, n_new=n_new),
        grid_spec=pltpu.PrefetchScalarGridSpec(
            num_scalar_prefetch=1, grid=(Bs, nstep + 1),
            in_specs=[pl.BlockSpec((1, 32, DH), lambda b, p, pt: (b, 0, 0)),
                      pl.BlockSpec((1, 32, LANES), lambda b, p, pt: (b, 0, 0)),
                      pl.BlockSpec((1, PAGE, DH), lambda b, p, pt: (b, 0, 0))] + [kidx_spec(g) for g in range(G)],
            out_specs=pl.BlockSpec((1, 8, G * PAGE), lambda b, p, pt: (b, 0, p))),
        out_shape=jax.ShapeDtypeStruct((Bs, 8, width), F32),
        compiler_params=_cparams(("parallel", "arbitrary")),
        name="dsa_sample_score",
    )(page_table, qi32, w32, kin, *([cache_kidx] * G))

    mask = pl.pallas_call(
        functools.partial(_dsa_s_select_body, n_sel=n_sel),
        grid=(Bs,),
        in_specs=[pl.BlockSpec((1, 8, width), lambda b: (b, 0, 0))],
        out_specs=pl.BlockSpec((1, 8, width), lambda b: (b, 0, 0)),
        out_shape=jax.ShapeDtypeStruct((Bs, 8, width), F32),
        compiler_params=_cparams(("parallel",)),
        name="dsa_sample_select",
    )(scores)

    def kv_spec(g):
        return pl.BlockSpec((None, None, 2 * W, PAGE), lambda b, p, pt: (layer, pt[b, p * G + g], 0, 0))

    return pl.pallas_call(
        functools.partial(_dsa_s_attend_body, G=G, n_new=n_new),
        grid_spec=pltpu.PrefetchScalarGridSpec(
            num_scalar_prefetch=1, grid=(Bs, nstep),
            in_specs=[pl.BlockSpec((1, 32, W), lambda b, p, pt: (b, 0, 0)),
                      pl.BlockSpec((1, PAGE, W), lambda b, p, pt: (b, 0, 0)),
                      pl.BlockSpec((1, PAGE, W), lambda b, p, pt: (b, 0, 0)),
                      pl.BlockSpec((1, 8, G * PAGE), lambda b, p, pt: (b, 0, p)),
                      pl.BlockSpec((1, 8, G * PAGE), lambda b, p, pt: (b, 0, nstep))] + [kv_spec(g) for g in range(G)],
            out_specs=pl.BlockSpec((1, 8, W), lambda b, p, pt: (b, 0, 0)),
            scratch_shapes=[pltpu.VMEM((32, LANES), F32), pltpu.VMEM((32, LANES), F32), pltpu.VMEM((32, W), F32)]),
        out_shape=jax.ShapeDtypeStruct((Bs, 8, W), BF16),
        compiler_params=_cparams(("parallel", "arbitrary")),
        name="dsa_sample_attend",
    )(page_table, qc32, kcn, vcn, mask, mask, *([cache_kv] * G))


def _merge_body(oa_ref, ob_ref, oc_ref, g0_ref, g1_ref, g2_ref, x_ref, wa_ref, wb_ref, wc_ref, wo_ref, o_ref):
    merged = (jax.nn.sigmoid(g0_ref[...]) * _dot(oa_ref[...], wa_ref[...])
              + jax.nn.sigmoid(g1_ref[...]) * _dot(ob_ref[...], wb_ref[...])
              + jax.nn.sigmoid(g2_ref[...]) * _dot(oc_ref[...], wc_ref[...]))
    o_ref[...] = x_ref[...] + _dot(merged.astype(BF16), wo_ref[...])


def _merge(oa, ob, oc, pf, x, wa, wb, wc, wo, *, tm, gate_blk):
    T, D = x.shape
    row = lambda i: (i, 0)
    const = lambda i: (0, 0)
    return pl.pallas_call(
        _merge_body,
        grid=(T // tm,),
        in_specs=[pl.BlockSpec((tm, oa.shape[1]), row), pl.BlockSpec((tm, ob.shape[1]), row),
                  pl.BlockSpec((tm, oc.shape[1]), row),
                  pl.BlockSpec((tm, D), lambda i: (i, gate_blk)), pl.BlockSpec((tm, D), lambda i: (i, gate_blk + 1)),
                  pl.BlockSpec((tm, D), lambda i: (i, gate_blk + 2)), pl.BlockSpec((tm, D), row),
                  pl.BlockSpec(wa.shape, const), pl.BlockSpec(wb.shape, const),
                  pl.BlockSpec(wc.shape, const), pl.BlockSpec(wo.shape, const)],
        out_specs=pl.BlockSpec((tm, D), row),
        out_shape=jax.ShapeDtypeStruct((T, D), F32),
        compiler_params=_cparams(("parallel",)),
        name="merge",
    )(oa, ob, oc, pf, pf, pf, x, wa, wb, wc, wo)


def _ffn_down_body(*refs, tm, seq_tiles, expanded, final_norm):
    ua_ref, ub_ref, ha_ref, hb_ref = refs[:4]
    rest = refs[4:]
    if expanded:
        ha2_ref, hb2_ref, t_ref = rest[:3]
        rest = rest[3:]
    cwa_ref, cwb_ref, cba_ref, cbb_ref, wd_ref, x_ref = rest[:6]
    rest = rest[6:]
    if final_norm:
        gf_ref = rest[0]
        rest = rest[1:]
    o_ref, acc_ref = rest
    i = pl.program_id(0)
    k = pl.program_id(1)

    @pl.when(k == 0)
    def _():
        acc_ref[...] = jnp.zeros_like(acc_ref)

    def conv(u_ref, h_ref, h2_ref, cw_ref, cb_ref):
        u = u_ref[...]
        row = lax.broadcasted_iota(I32, u.shape, 0)
        r1 = pltpu.roll(u, 1, axis=0)
        r2 = pltpu.roll(u, 2, axis=0)
        if expanded:
            t = t_ref[...]
            u1 = jnp.where(t >= 1, r1, h_ref[...])
            u2 = jnp.where(t >= 2, r2, h2_ref[...])
        else:
            h = h_ref[...]
            h = jnp.where(i % seq_tiles == 0, jnp.zeros_like(h), h)
            u1 = jnp.where(row == 0, h[7:8], r1)
            u2 = jnp.where(row == 0, h[6:7], jnp.where(row == 1, h[7:8], r2))
        cw = cw_ref[...]
        return cb_ref[...] + cw[0:1] * u2 + cw[1:2] * u1 + cw[2:3] * u

    a = conv(ua_ref, ha_ref, ha2_ref if expanded else None, cwa_ref, cba_ref)
    b = conv(ub_ref, hb_ref, hb2_ref if expanded else None, cwb_ref, cbb_ref)
    acc_ref[...] += _dot((_silu(a) * b).astype(BF16), wd_ref[...])

    @pl.when(k == pl.num_programs(1) - 1)
    def _():
        y = x_ref[...] + acc_ref[...]
        if final_norm:
            ms = jnp.mean(y * y, axis=-1, keepdims=True)
            y = y * lax.rsqrt(ms + EPS) * gf_ref[...]
        o_ref[...] = y


def _ffn_down(u, x, cw, cb, wd, *, tm, tkf, seq_len, prev=None, g_final=None):
    T, D = x.shape
    F = wd.shape[0]
    nk = F // tkf
    expanded = prev is not None
    final_norm = g_final is not None
    seq_tiles = max(seq_len // tm, 1)
    hb8 = tm // 8
    ua = pl.BlockSpec((tm, tkf), lambda i, k: (i, k))
    ub = pl.BlockSpec((tm, tkf), lambda i, k: (i, nk + k))
    in_specs = [ua, ub]
    args = [u, u]
    if expanded:
        p1, p2, tpos = prev
        in_specs += [ua, ub, ua, ub, pl.BlockSpec((tm, 1), lambda i, k: (i, 0))]
        args += [p1, p1, p2, p2, tpos]
    else:
        in_specs += [pl.BlockSpec((8, tkf), lambda i, k: (jnp.maximum(i * hb8 - 1, 0), k)),
                     pl.BlockSpec((8, tkf), lambda i, k: (jnp.maximum(i * hb8 - 1, 0), nk + k))]
        args += [u, u]
    in_specs += [pl.BlockSpec((3, tkf), lambda i, k: (0, k)), pl.BlockSpec((3, tkf), lambda i, k: (0, nk + k)),
                 pl.BlockSpec((1, tkf), lambda i, k: (0, k)), pl.BlockSpec((1, tkf), lambda i, k: (0, nk + k)),
                 pl.BlockSpec((tkf, D), lambda i, k: (k, 0)), pl.BlockSpec((tm, D), lambda i, k: (i, 0))]
    args += [cw, cw, cb.reshape(1, -1), cb.reshape(1, -1), wd, x]
    if final_norm:
        in_specs.append(pl.BlockSpec((1, D), lambda i, k: (0, 0)))
        args.append(g_final.reshape(1, D))
    return pl.pallas_call(
        functools.partial(_ffn_down_body, tm=tm, seq_tiles=seq_tiles, expanded=expanded, final_norm=final_norm),
        grid=(T // tm, nk),
        in_specs=in_specs,
        out_specs=pl.BlockSpec((tm, D), lambda i, k: (i, 0)),
        out_shape=jax.ShapeDtypeStruct((T, D), F32),
        scratch_shapes=[pltpu.VMEM((tm, D), F32)],
        compiler_params=_cparams(("parallel", "arbitrary")),
        name="ffn_down",
    )(*args)


def _rotate_half_cols(w):
    D, N = w.shape
    w4 = w.reshape(D, N // DH, 2, DH // 2)
    return jnp.concatenate([-w4[:, :, 1], w4[:, :, 0]], axis=-1).reshape(D, N)


def _rope_tables(pos, width):
    half = DH // 2
    inv_freq = ROPE_THETA ** (-jnp.arange(half, dtype=F32) / half)
    ang = pos.astype(F32)[:, None] * inv_freq[None, :]
    reps = width // half
    return jnp.tile(jnp.cos(ang), (1, reps)), jnp.tile(jnp.sin(ang), (1, reps))


def _pick(n, prefs):
    for t in prefs:
        if n % t == 0:
            return t
    return n


def kernel(x_prompt, x_sample, cache_kv_a, cache_kv_c, cache_kidx_c, state_hgrn, state_ffn_conv, page_table,
           norm_mix, w_in, hgrn_lb_logits, hgrn_gnorm, w_br_a, w_br_b, w_br_c, w_out, norm_ffn, w_up, conv_w,
           conv_b, w_down, norm_final):
    B, L, D = x_prompt.shape
    Bs, Ls, _ = x_sample.shape
    depth = w_in.shape[0]
    NP = page_table.shape[1]
    past = NP * PAGE
    F = w_down.shape[1]
    WA = N_HEADS * DH
    WB = N_HEADS * DK
    Tp, Ts = B * L, Bs * Ls
    n_pool = cache_kv_a.shape[1]

    lb_soft = jax.nn.softmax(hgrn_lb_logits.astype(F32), axis=0)
    lb_all = jnp.cumsum(lb_soft, axis=0) - lb_soft[0]

    sizes = (WA, WA, WA, WB, WB, WB, WB, WA, WA, WA, N_HEADS * DH, DH, N_HEADS, 3 * D)
    offs = [0]
    for s in sizes:
        offs.append(offs[-1] + s)
    (o_qa, o_ka, o_va, o_qh, o_fh, o_ih, o_gh, o_qc, o_kc, o_vc, o_qi, o_ki, o_wi, o_gt, o_end) = offs
    QA_BLK, KA_BLK, VA_BLK, VC_BLK = (4 * WB) // WA, (4 * WB) // WA + 1, (4 * WB) // WA + 2, (4 * WB) // WA + 3
    gate_off = 4 * WB + 4 * WA
    assert gate_off % D == 0
    GATE_BLK = gate_off // D
    NPJ = gate_off + 3 * D
    N_ROPE_TILES = 4
    NRJ = (N_ROPE_TILES + 1) * WA
    wi_off = N_ROPE_TILES * WA
    WI_BLK = wi_off // LANES

    cache_a = cache_kv_a.transpose(0, 1, 3, 4, 5, 2).reshape(depth, n_pool, 2 * WA, PAGE)
    cache_c = cache_kv_c.transpose(0, 1, 3, 4, 5, 2).reshape(depth, n_pool, 2 * WA, PAGE)
    cache_i = cache_kidx_c.transpose(0, 1, 3, 2)

    cos_p, sin_p = _rope_tables(jnp.arange(L, dtype=I32), WA)
    cos_s, sin_s = _rope_tables(past + (jnp.arange(Ts, dtype=I32) % Ls), WA)

    tm_p = _pick(Tp, (1024, 512, 256, 128))
    tm_p = min(tm_p, L)
    tn_main = _pick(NPJ, (512, 256, 128))
    tn_up = _pick(2 * F, (512, 256, 128))
    tkf = _pick(F, (1408, 256, 128))
    tq_a = _pick(L, (256, 128))
    tq_c = _pick(L, (256, 128))
    tk_c = _pick(L, (512, 256, 128))
    C_h = _pick(L, (64,))
    G = _pick(NP, (16, 8, 4, 2, 1))
    n_sel_p = min(TOPK_MAX, L // 4)
    n_sel_s = min(TOPK_MAX, (past + Ls) // 4)

    xp = x_prompt.reshape(Tp, D)
    xs = x_sample.reshape(Ts, D)
    tpos_s = (jnp.arange(Ts, dtype=I32) % Ls).reshape(Ts, 1)

    def pad_rows(a, rows):
        return jnp.pad(a, ((0, 0), (0, rows - a.shape[1]), (0, 0)))

    outs_p = ([], [], [], [], [])
    outs_s = ([], [], [], [], [])
    for l in range(depth):
        w = w_in[l]
        col = lambda o, n: w[:, o:o + n]
        w_main = jnp.concatenate(
            [col(o_qh, 4 * WB), col(o_qa, 3 * WA), col(o_vc, WA), col(o_gt, 3 * D)], axis=1).astype(BF16)
        w_r = jnp.concatenate([col(o_qc, WA), col(o_kc, WA), col(o_qi, WA)] + [col(o_ki, DH)] * N_HEADS, axis=1)
        w_tail = jnp.concatenate([col(o_wi, N_HEADS), jnp.zeros((D, WA - N_HEADS), w.dtype)], axis=1)
        w_rot = jnp.concatenate([_rotate_half_cols(w_r), jnp.zeros((D, WA), w.dtype)], axis=1).astype(BF16)
        w_r = jnp.concatenate([w_r, w_tail], axis=1).astype(BF16)
        wa_b, wb_b, wc_b, wo_b = (t[l].astype(BF16) for t in (w_br_a, w_br_b, w_br_c, w_out))
        wup_b = w_up[l].astype(BF16)
        wd_b = w_down[l].astype(BF16)
        last = l == depth - 1

        vc_col = VC_BLK * WA
        pf, pb, vT = _rms_proj(xp, norm_mix[l], w_main, tm=tm_p, tn=tn_main,
                               t_outs=((vc_col // tn_main, vc_col % tn_main, WA, WA, tk_c, BF16),))
        rf, rb, qcT, qiT, wiT = _rms_proj(
            xp, norm_mix[l], w_r, tm=tm_p, tn=WA, rope_args=(w_rot, cos_p, sin_p), n_rope_tiles=N_ROPE_TILES,
            t_outs=((0, 0, WA, WA, tq_c, BF16), (2, 0, WA, WA, tq_c, BF16), (N_ROPE_TILES, 0, LANES, 8, tq_c, F32)))
        oa = _attn_a_prompt(pb, B, L, tq=tq_a, q_blk=QA_BLK, k_blk=KA_BLK, v_blk=VA_BLK)
        ob, s_new = _hgrn(pf.reshape(B, L, NPJ), lb_all[l], hgrn_gnorm[l], None, C=C_h, c=min(16, C_h), valid=C_h)
        oc = _dsa_prompt(qcT, qiT, wiT, rb, vT, B, L, tq=tq_c, tk=tk_c, n_sel=n_sel_p)
        x1 = _merge(oa, ob.reshape(Tp, WB), oc, pf, xp, wa_b, wb_b, wc_b, wo_b, tm=min(512, tm_p), gate_blk=GATE_BLK)
        u = _rms_proj(x1, norm_ffn[l], wup_b, tm=tm_p, tn=tn_up, emit_bf16=False)
        xp = _ffn_down(u, x1, conv_w[l], conv_b[l], wd_b, tm=min(512, tm_p), tkf=tkf, seq_len=L,
                       g_final=norm_final if last else None)
        kv_off = (KA_BLK * WA)
        outs_p[0].append(pf[:, kv_off:kv_off + 2 * WA].reshape(B, L, 2, N_HEADS, DH))
        outs_p[1].append(jnp.stack([rf[:, WA:2 * WA].reshape(B, L, N_HEADS, DH),
                                    pf[:, VC_BLK * WA:(VC_BLK + 1) * WA].reshape(B, L, N_HEADS, DH)], axis=2))
        outs_p[2].append(rf[:, 3 * WA:3 * WA + DH].reshape(B, L, DH))
        outs_p[3].append(s_new)
        outs_p[4].append(u.reshape(B, L, 2 * F)[:, L - 2:])

        pf, pb = _rms_proj(xs, norm_mix[l], w_main, tm=Ts, tn=tn_main)
        rf, rb = _rms_proj(xs, norm_mix[l], w_r, tm=Ts, tn=WA, rope_args=(w_rot, cos_s, sin_s),
                           n_rope_tiles=N_ROPE_TILES)
        pb3 = pb.reshape(Bs, Ls, NPJ)
        rb3 = rb.reshape(Bs, Ls, NRJ)
        blk = lambda a, k: a[:, :, k * WA:(k + 1) * WA]
        oa8 = _attn_a_sample(pad_rows(blk(pb3, QA_BLK), 8), pad_rows(blk(pb3, KA_BLK), PAGE),
                             pad_rows(blk(pb3, VA_BLK), PAGE), cache_a, l, page_table, G=G)
        ph = jnp.pad(pf[:, :4 * WB].reshape(Bs, Ls, 4 * WB), ((0, 0), (0, 8 - Ls), (0, 0)))
        ob8, s_new = _hgrn(ph, lb_all[l], hgrn_gnorm[l], state_hgrn[l], C=8, c=8, valid=Ls)
        qi = blk(rb3, 2).reshape(Bs, Ls, N_HEADS, DH).transpose(0, 2, 1, 3)
        qi32 = jnp.pad(qi, ((0, 0), (0, 0), (0, 8 - Ls), (0, 0))).reshape(Bs, 4 * 8, DH)
        wi = rf[:, wi_off:wi_off + N_HEADS].reshape(Bs, Ls, N_HEADS).transpose(0, 2, 1) * (N_HEADS ** -0.5)
        w32 = jnp.broadcast_to(jnp.pad(wi, ((0, 0), (0, 0), (0, 8 - Ls))).reshape(Bs, 32, 1), (Bs, 32, LANES))
        kin = pad_rows(rb3[:, :, 3 * WA:3 * WA + DH], PAGE)
        qc8 = pad_rows(blk(rb3, 0), 8)
        head = (jnp.arange(WA) // DH)[None, None, None, :] == jnp.arange(N_HEADS)[None, :, None, None]
        qc32 = (jnp.where(head, qc8[:, None], 0) * jnp.asarray(DH ** -0.5, BF16)).reshape(Bs, 32, WA).astype(BF16)
        oc8 = _dsa_sample(qi32, w32, kin, qc32, pad_rows(blk(rb3, 1), PAGE), pad_rows(blk(pb3, VC_BLK), PAGE),
                          cache_i, cache_c, l, page_table, G=G, n_new=Ls, n_sel=n_sel_s)
        x1 = _merge(oa8[:, :Ls].reshape(Ts, WA), ob8[:, :Ls].reshape(Ts, WB), oc8[:, :Ls].reshape(Ts, WA), pf, xs,
                    wa_b, wb_b, wc_b, wo_b, tm=Ts, gate_blk=GATE_BLK)
        u = _rms_proj(x1, norm_ffn[l], wup_b, tm=Ts, tn=tn_up, emit_bf16=False)
        prev = state_ffn_conv[l]
        u3 = u.reshape(Bs, Ls, 2 * F)
        p1 = jnp.broadcast_to(prev[:, 1:2], (Bs, Ls, 2 * F)).reshape(Ts, 2 * F)
        p2 = jnp.concatenate([prev, jnp.zeros((Bs, Ls - 2, 2 * F), F32)], axis=1).reshape(Ts, 2 * F)
        xs = _ffn_down(u, x1, conv_w[l], conv_b[l], wd_b, tm=Ts, tkf=tkf, seq_len=Ls, prev=(p1, p2, tpos_s),
                       g_final=norm_final if last else None)
        kv_off = (KA_BLK * WA)
        outs_s[0].append(pf[:, kv_off:kv_off + 2 * WA].reshape(Bs, Ls, 2, N_HEADS, DH))
        outs_s[1].append(jnp.stack([rf[:, WA:2 * WA].reshape(Bs, Ls, N_HEADS, DH),
                                    pf[:, VC_BLK * WA:(VC_BLK + 1) * WA].reshape(Bs, Ls, N_HEADS, DH)], axis=2))
        outs_s[2].append(rf[:, 3 * WA:3 * WA + DH].reshape(Bs, Ls, DH))
        outs_s[3].append(s_new)
        outs_s[4].append(jnp.concatenate([prev, u3], axis=1)[:, Ls:])

    st = lambda xs_: jnp.stack(xs_, axis=0)
    return (xp.reshape(B, L, D), xs.reshape(Bs, Ls, D),
            st(outs_p[0]), st(outs_s[0]), st(outs_p[1]), st(outs_s[1]), st(outs_p[2]), st(outs_s[2]),
            st(outs_p[3]), st(outs_s[3]), st(outs_p[4]), st(outs_s[4]))
```

```python
import functools

import jax
import jax.numpy as jnp
from jax import lax
from jax.experimental import pallas as pl
from jax.experimental.pallas import tpu as pltpu

F32 = jnp.float32
BF16 = jnp.bfloat16
I32 = jnp.int32

EPS = 1e-6
NEG_BIG = -1e30
LOG_F_MIN = -30.0
ROPE_THETA = 10000.0
TOPK_MAX = 256
PAGE = 128
N_HEADS = 4
DH = 64
DK = 128
LANES = 128
VMEM_LIMIT = 56 * 1024 * 1024
INT_MIN = -(2 ** 31)
EXP_UNDERFLOW = -110.0


def _cparams(sem):
    return pltpu.CompilerParams(dimension_semantics=sem, vmem_limit_bytes=VMEM_LIMIT)


def _dot(a, b):
    return jnp.dot(a, b, preferred_element_type=F32)


def _dot_nt(a, b):
    return lax.dot_general(a, b, (((1,), (1,)), ((), ())), preferred_element_type=F32)


def _dot_tn(a, b):
    return lax.dot_general(a, b, (((0,), (0,)), ((), ())), preferred_element_type=F32)


def _silu(x):
    return x * jax.nn.sigmoid(x)


def _stack_heads(q, rows):
    head = lax.broadcasted_iota(I32, (rows, N_HEADS * DH), 1) // DH
    return jnp.concatenate([jnp.where(head == h, q, jnp.zeros_like(q)) for h in range(N_HEADS)], axis=0)


def _unstack_heads(acc, rows):
    head = lax.broadcasted_iota(I32, (rows, N_HEADS * DH), 1) // DH
    out = jnp.zeros((rows, N_HEADS * DH), acc.dtype)
    for h in range(N_HEADS):
        out = jnp.where(head == h, acc[h * rows:(h + 1) * rows], out)
    return out


def _proj_body(*refs, rope, emit_bf16, n_rope_tiles, t_outs):
    x_ref, g_ref, w_ref = refs[:3]
    rest = refs[3:]
    if rope:
        wrot_ref, cos_ref, sin_ref = rest[:3]
        rest = rest[3:]
    of_ref = rest[0]
    ob_ref = rest[1] if emit_bf16 else None
    t_refs = rest[(2 if emit_bf16 else 1):-1]
    h_ref = rest[-1]

    @pl.when(pl.program_id(1) == 0)
    def _():
        x = x_ref[...]
        ms = jnp.mean(x * x, axis=-1, keepdims=True)
        h_ref[...] = (x * lax.rsqrt(ms + EPS) * g_ref[...]).astype(BF16)

    h = h_ref[...]
    acc = _dot(h, w_ref[...])
    if rope:
        roped = acc * cos_ref[...] + _dot(h, wrot_ref[...]) * sin_ref[...]
        acc = jnp.where(pl.program_id(1) < n_rope_tiles, roped, acc)
    of_ref[...] = acc
    if emit_bf16:
        ob_ref[...] = acc.astype(BF16)
    for t_ref, (j_tile, c0, width, keep, sub, dtype) in zip(t_refs, t_outs):
        @pl.when(pl.program_id(1) == j_tile)
        def _(t_ref=t_ref, c0=c0, width=width, keep=keep, sub=sub, dtype=dtype):
            for r in range(acc.shape[0] // sub):
                t_ref[r] = acc[r * sub:(r + 1) * sub, c0:c0 + width].T[:keep].astype(dtype)


def _rms_proj(x, g, w, *, tm, tn, rope_args=None, emit_bf16=True, n_rope_tiles=0, t_outs=()):
    T, D = x.shape
    N = w.shape[1]
    rope = rope_args is not None
    in_specs = [pl.BlockSpec((tm, D), lambda i, j: (i, 0)),
                pl.BlockSpec((1, D), lambda i, j: (0, 0)),
                pl.BlockSpec((D, tn), lambda i, j: (0, j))]
    args = [x, g.reshape(1, D), w]
    if rope:
        w_rot, cos, sin = rope_args
        nblk = cos.shape[0] // tm
        in_specs += [pl.BlockSpec((D, tn), lambda i, j: (0, j)),
                     pl.BlockSpec((tm, tn), lambda i, j: (i % nblk, 0)),
                     pl.BlockSpec((tm, tn), lambda i, j: (i % nblk, 0))]
        args += [w_rot, cos, sin]
    out_shape = [jax.ShapeDtypeStruct((T, N), F32)]
    out_specs = [pl.BlockSpec((tm, tn), lambda i, j: (i, j))]
    if emit_bf16:
        out_shape.append(jax.ShapeDtypeStruct((T, N), BF16))
        out_specs.append(pl.BlockSpec((tm, tn), lambda i, j: (i, j)))
    for (_, _, _, keep, sub, dtype) in t_outs:
        out_shape.append(jax.ShapeDtypeStruct((T // sub, keep, sub), dtype))
        out_specs.append(pl.BlockSpec((tm // sub, keep, sub), lambda i, j: (i, 0, 0)))
    outs = pl.pallas_call(
        functools.partial(_proj_body, rope=rope, emit_bf16=emit_bf16, n_rope_tiles=n_rope_tiles, t_outs=tuple(t_outs)),
        grid=(T // tm, N // tn),
        in_specs=in_specs, out_specs=out_specs, out_shape=out_shape,
        scratch_shapes=[pltpu.VMEM((tm, D), BF16)],
        compiler_params=_cparams(("parallel", "arbitrary")),
        name="rms_proj_rope" if rope else "rms_proj",
    )(*args)
    return outs if (emit_bf16 or t_outs) else outs[0]


def _suffix_matrix():
    j = lax.broadcasted_iota(I32, (2 * LANES, LANES), 0) & (LANES - 1)
    s = lax.broadcasted_iota(I32, (2 * LANES, LANES), 1)
    return jnp.where(j > s, 1.0, 0.0).astype(BF16)


def _log_one_minus_beta(z):
    return -(jnp.maximum(z, 0.0) + jnp.log(1.0 + jnp.exp(-jnp.abs(z))))


def _sb_update(z, carry, suffix, vis):
    M, tk = z.shape
    ls = _log_one_minus_beta(z)
    if vis is not None:
        ls = jnp.where(vis, ls, 0.0)
    hi = ls.astype(BF16)
    lo = (ls - hi.astype(F32)).astype(BF16)
    lz = z + ls
    n = tk // LANES
    outs = [None] * n
    for g in reversed(range(n)):
        sl = slice(g * LANES, (g + 1) * LANES)
        between = _dot(jnp.concatenate([hi[:, sl], lo[:, sl]], axis=1), suffix)
        e = lz[:, sl] + between + carry
        if vis is not None:
            e = jnp.where(vis[:, sl], e, NEG_BIG)
        outs[g] = jnp.exp(e)
        carry = carry + jnp.sum(ls[:, sl], axis=1, keepdims=True)
    a = outs[0] if n == 1 else jnp.concatenate(outs, axis=1)
    return a.astype(BF16), carry


def _attn_a_prompt_body(q_ref, k_ref, v_ref, o_ref, *, tq):
    i = pl.program_id(1)
    M = N_HEADS * tq
    qs = _stack_heads(q_ref[...], tq) * jnp.asarray(DH ** -0.5, BF16)
    suffix = _suffix_matrix()

    def block(k0, carry, acc, vis):
        kb = k_ref[pl.ds(k0, tq), :]
        vb = v_ref[pl.ds(k0, tq), :]
        a, carry = _sb_update(_dot_nt(qs, kb), carry, suffix, vis)
        return carry, acc + _dot(a, vb)

    row_t = lax.broadcasted_iota(I32, (M, tq), 0) & (tq - 1)
    col_s = lax.broadcasted_iota(I32, (M, tq), 1)
    carry, acc = block(pl.multiple_of(i * tq, tq), jnp.zeros((M, 1), F32),
                       jnp.zeros((M, N_HEADS * DH), F32), col_s < row_t)

    def alive(carry):
        return (jnp.max(carry) > EXP_UNDERFLOW).astype(I32)

    def cond(c):
        return (c[0] < i) & (c[1] > 0)

    def body(c):
        step = c[0]
        k0 = pl.multiple_of((i - 1 - step) * tq, tq)
        carry, acc = block(k0, c[2], c[3], None)
        return step + 1, alive(carry), carry, acc

    _, _, carry, acc = lax.while_loop(cond, body, (jnp.int32(0), alive(carry), carry, acc))
    o_ref[...] = _unstack_heads(acc, tq).astype(o_ref.dtype)


def _attn_a_prompt(pb, B, L, *, tq, q_blk, k_blk, v_blk):
    W = N_HEADS * DH
    nq = L // tq
    return pl.pallas_call(
        functools.partial(_attn_a_prompt_body, tq=tq),
        grid=(B, nq),
        in_specs=[pl.BlockSpec((tq, W), lambda b, i: (b * nq + i, q_blk)),
                  pl.BlockSpec((L, W), lambda b, i: (b, k_blk)),
                  pl.BlockSpec((L, W), lambda b, i: (b, v_blk))],
        out_specs=pl.BlockSpec((tq, W), lambda b, i: (b * nq + i, 0)),
        out_shape=jax.ShapeDtypeStruct((B * L, W), BF16),
        compiler_params=_cparams(("parallel", "arbitrary")),
        name="attn_a_prompt",
    )(pb, pb, pb)


def _attn_a_sample_body(*refs, G, resume):
    n_pref = 2 if resume else 1
    q_ref, a_ref, b_ref = refs[n_pref:n_pref + 3]
    pages = refs[n_pref + 3:n_pref + 3 + G]
    rest = refs[n_pref + 3 + G:]
    if resume:
        o_ref, carry_ref, acc_ref = rest
    else:
        o_ref, carry_out_ref, acc_out_ref, carry_ref, acc_ref = rest
    p = pl.program_id(1)
    W = N_HEADS * DH
    R = 8
    M = N_HEADS * R
    qs = _stack_heads(q_ref[0], R) * jnp.asarray(DH ** -0.5, BF16)
    suffix = _suffix_matrix()

    @pl.when(p == 0)
    def _():
        if resume:
            carry_ref[...] = a_ref[0]
            acc_ref[...] = b_ref[0]
        else:
            row_t = lax.broadcasted_iota(I32, (M, PAGE), 0) & (R - 1)
            col_s = lax.broadcasted_iota(I32, (M, PAGE), 1)
            a, carry = _sb_update(_dot_nt(qs, a_ref[0]), jnp.zeros((M, 1), F32), suffix, col_s < row_t)
            carry_ref[...] = jnp.broadcast_to(carry, carry_ref.shape)
            acc_ref[...] = _dot(a, b_ref[0])

    @pl.when(jnp.max(carry_ref[...]) > EXP_UNDERFLOW)
    def _():
        kcat = jnp.concatenate([pages[g][:W, :] for g in reversed(range(G))], axis=1).astype(BF16)
        vcat = jnp.concatenate([pages[g][W:, :] for g in reversed(range(G))], axis=1).astype(BF16)
        a, carry = _sb_update(_dot(qs, kcat), carry_ref[:, 0:1], suffix, None)
        carry_ref[...] = jnp.broadcast_to(carry, carry_ref.shape)
        acc_ref[...] += _dot_nt(a, vcat)

    @pl.when(p == pl.num_programs(1) - 1)
    def _():
        o_ref[0] = _unstack_heads(acc_ref[...], R).astype(o_ref.dtype)
        if not resume:
            carry_out_ref[0] = carry_ref[...]
            acc_out_ref[0] = acc_ref[...]


def _attn_a_sample(q8, kn, vn, cache, layer, page_table, *, G):
    Bs = q8.shape[0]
    NP = page_table.shape[1]
    W = N_HEADS * DH
    M = N_HEADS * 8
    nstep = NP // G
    scratch = [pltpu.VMEM((M, LANES), F32), pltpu.VMEM((M, W), F32)]
    row3 = lambda *a: (a[0], 0, 0)

    def head_page(g):
        return pl.BlockSpec((None, None, 2 * W, PAGE), lambda b, p, pt: (layer, pt[b, NP - 1 - g], 0, 0))

    o, carry, acc = pl.pallas_call(
        functools.partial(_attn_a_sample_body, G=G, resume=False),
        grid_spec=pltpu.PrefetchScalarGridSpec(
            num_scalar_prefetch=1, grid=(Bs, 1),
            in_specs=[pl.BlockSpec((1, 8, W), row3), pl.BlockSpec((1, PAGE, W), row3),
                      pl.BlockSpec((1, PAGE, W), row3)] + [head_page(g) for g in range(G)],
            out_specs=[pl.BlockSpec((1, 8, W), row3), pl.BlockSpec((1, M, LANES), row3), pl.BlockSpec((1, M, W), row3)],
            scratch_shapes=scratch),
        out_shape=[jax.ShapeDtypeStruct((Bs, 8, W), BF16), jax.ShapeDtypeStruct((Bs, M, LANES), F32),
                   jax.ShapeDtypeStruct((Bs, M, W), F32)],
        compiler_params=_cparams(("parallel", "arbitrary")),
        name="attn_a_sample",
    )(page_table, q8, kn, vn, *([cache] * G))
    if nstep == 1:
        return o
    alive = (jnp.max(carry, axis=(1, 2)) > EXP_UNDERFLOW).astype(I32)

    def tail_page(g):
        return pl.BlockSpec((None, None, 2 * W, PAGE),
                            lambda b, p, pt, al: (layer, jnp.where(al[b] > 0, pt[b, NP - 1 - ((p + 1) * G + g)], 0), 0, 0))

    tail = pl.pallas_call(
        functools.partial(_attn_a_sample_body, G=G, resume=True),
        grid_spec=pltpu.PrefetchScalarGridSpec(
            num_scalar_prefetch=2, grid=(Bs, nstep - 1),
            in_specs=[pl.BlockSpec((1, 8, W), row3), pl.BlockSpec((1, M, LANES), row3),
                      pl.BlockSpec((1, M, W), row3)] + [tail_page(g) for g in range(G)],
            out_specs=pl.BlockSpec((1, 8, W), row3),
            scratch_shapes=scratch),
        out_shape=jax.ShapeDtypeStruct((Bs, 8, W), BF16),
        compiler_params=_cparams(("parallel", "arbitrary")),
        name="attn_a_sample_tail",
    )
    return lax.cond(jnp.any(alive > 0), lambda: tail(page_table, alive, q8, carry, acc, *([cache] * G)), lambda: o)


def _cumsum_rows(x):
    C = x.shape[0]
    row = lax.broadcasted_iota(I32, x.shape, 0)
    sh = 1
    while sh < C:
        x = x + jnp.where(row >= sh, pltpu.roll(x, sh, axis=0), 0.0)
        sh *= 2
    return x


def _hgrn_body(*refs, C, c, valid, has_s0):
    q_ref, f_ref, i_ref, g_ref, lb_ref, gn_ref = refs[:6]
    rest = refs[6:]
    if has_s0:
        s0_ref = rest[0]
        rest = rest[1:]
    o_ref, sout_ref, st_ref = rest
    ci = pl.program_id(1)

    @pl.when(ci == 0)
    def _():
        for h in range(N_HEADS):
            st_ref[h] = s0_ref[0, h].T if has_s0 else jnp.zeros((DK, DK), F32)

    row = lax.broadcasted_iota(I32, (C, DK), 0)
    rowc = lax.broadcasted_iota(I32, (c, 1), 0)
    for h in range(N_HEADS):
        hs = slice(h * DK, (h + 1) * DK)
        kk = (1.0 - lb_ref[:, hs]) * jax.nn.sigmoid(-f_ref[0, :, hs])
        lg = jnp.maximum(jnp.log1p(-kk), LOG_F_MIN)
        if valid < C:
            kk = jnp.where(row < valid, kk, 0.0)
            lg = jnp.where(row < valid, lg, 0.0)
        qq = _silu(q_ref[0, :, hs])
        vv = i_ref[0, :, hs]
        cum = _cumsum_rows(lg)
        st = st_ref[h]
        o = _dot_nt((qq * jnp.exp(cum)).astype(BF16), st.astype(BF16))
        parts = []
        for blk in range(C // c):
            r0 = blk * c
            q_b = qq[r0:r0 + c]
            cum_b = cum[r0:r0 + c]
            o_b = jnp.zeros((c, DK), F32)
            if blk > 0:
                base = cum[r0 - 1:r0]
                qt = q_b * jnp.exp(cum_b - base)
                kt = kk[:r0] * jnp.exp(base - cum[:r0])
                sc = _dot_nt(qt.astype(BF16), kt.astype(BF16))
                o_b = o_b + _dot(sc.astype(BF16), vv[:r0].astype(BF16))
            for s in range(c):
                r = r0 + s
                d = jnp.minimum(cum_b - cum[r:r + 1], 0.0)
                w = jnp.sum(q_b * kk[r:r + 1] * jnp.exp(d), axis=1, keepdims=True)
                o_b = o_b + jnp.where(rowc >= s, w, 0.0) * vv[r:r + 1]
            parts.append(o_b)
        o = o + (parts[0] if len(parts) == 1 else jnp.concatenate(parts, axis=0))
        last = cum[C - 1:C]
        kd = kk * jnp.exp(last - cum)
        st_ref[h] = st * jnp.exp(last) + _dot_tn(vv.astype(BF16), kd.astype(BF16))
        ms = jnp.mean(o * o, axis=1, keepdims=True)
        y = o * lax.rsqrt(ms + EPS) * gn_ref[...] * _silu(g_ref[0, :, hs])
        o_ref[0, :, hs] = y.astype(o_ref.dtype)

    @pl.when(ci == pl.num_programs(1) - 1)
    def _():
        for h in range(N_HEADS):
            sout_ref[0, h] = st_ref[h].T


def _hgrn(p3, lb, gn, s0, *, C, c, valid, col0=0):
    B, L, _ = p3.shape
    W = N_HEADS * DK
    has_s0 = s0 is not None
    in_specs = [pl.BlockSpec((1, C, W), functools.partial(lambda b, ci, k: (b, ci, col0 + k), k=k)) for k in range(4)]
    in_specs += [pl.BlockSpec((1, W), lambda b, ci: (0, 0)), pl.BlockSpec((1, DK), lambda b, ci: (0, 0))]
    args = [p3, p3, p3, p3, lb.reshape(1, W), gn.reshape(1, DK)]
    if has_s0:
        in_specs.append(pl.BlockSpec((1, N_HEADS, DK, DK), lambda b, ci: (b, 0, 0, 0)))
        args.append(s0)
    return pl.pallas_call(
        functools.partial(_hgrn_body, C=C, c=c, valid=valid, has_s0=has_s0),
        grid=(B, L // C),
        in_specs=in_specs,
        out_specs=[pl.BlockSpec((1, C, W), lambda b, ci: (b, ci, 0)),
                   pl.BlockSpec((1, N_HEADS, DK, DK), lambda b, ci: (b, 0, 0, 0))],
        out_shape=[jax.ShapeDtypeStruct((B, L, W), BF16), jax.ShapeDtypeStruct((B, N_HEADS, DK, DK), F32)],
        scratch_shapes=[pltpu.VMEM((N_HEADS, DK, DK), F32)],
        compiler_params=_cparams(("parallel", "arbitrary")),
        name="hgrn2",
    )(*args)


def _sort_key(score):
    b = pltpu.bitcast(score, I32)
    return jnp.where(b < 0, b ^ jnp.int32(0x7FFFFFFF), b)


def _prefix_matrix(n):
    j = lax.broadcasted_iota(I32, (n, n + LANES), 0)
    s = lax.broadcasted_iota(I32, (n, n + LANES), 1)
    return jnp.where((s >= n) | (j <= s), 1.0, 0.0).astype(BF16)


def _kth_largest_key(count_ge, rows, n_sel):
    def body(t, T):
        cand = T + jnp.left_shift(jnp.int32(1), 31 - t)
        return jnp.where(count_ge(cand) >= n_sel, cand, T)
    return lax.fori_loop(0, 32, body, jnp.full((rows, 1), INT_MIN, I32))


def _dsa_prompt_body(qcT_ref, qiT_ref, wiT_ref, kc_ref, ki_ref, vT_ref, lt_ref, o_ref, keys_ref, kh_ref, kl_ref,
                     *, tq, tk, n_sel):
    i = pl.program_id(1)
    nkb = ((i + 1) * tq + tk - 1) // tk
    M = N_HEADS * tq
    W = N_HEADS * DH
    I16 = jnp.int16
    kpos0 = lax.broadcasted_iota(I32, (tk, tq), 0)
    qpos = i * tq + lax.broadcasted_iota(I32, (tk, tq), 1)
    row_head = lax.broadcasted_iota(I32, (W, tq), 0) // DH

    def stack_lanes(qT):
        return jnp.concatenate([jnp.where(row_head == h, qT, jnp.zeros_like(qT)) for h in range(N_HEADS)], axis=1)

    qisT = stack_lanes(qiT_ref[0])
    wi = wiT_ref[0]
    w_row = jnp.concatenate([wi[j:j + 1, :] for j in range(N_HEADS)], axis=1) * (N_HEADS ** -0.5) * (DH ** -0.5)

    def score_body(j, _):
        k0 = pl.multiple_of(j * tk, tk)
        d = jnp.maximum(_dot(ki_ref[pl.ds(k0, tk), :], qisT), 0.0) * w_row
        score = d[:, 0:tq]
        for jh in range(1, N_HEADS):
            score = score + d[:, jh * tq:(jh + 1) * tq]
        score = jnp.where(kpos0 + k0 <= qpos, score + 0.0, NEG_BIG)
        key = _sort_key(score)
        keys_ref[j] = key
        kh_ref[j] = (key >> 16).astype(I16)
        kl_ref[j] = ((key & 0xFFFF) - 32768).astype(I16)
        return 0

    lax.fori_loop(0, nkb, score_body, 0)

    def count16(ref, cand):
        c16 = cand.astype(I16)

        def body(j, acc):
            m = jnp.where(ref[j] >= c16, jnp.ones((), BF16), jnp.zeros((), BF16))
            parts = [m[16 * r:16 * (r + 1)] for r in range(tk // 16)]
            while len(parts) > 1:
                parts = [parts[a] + parts[a + 1] for a in range(0, len(parts) - 1, 2)] + parts[len(parts) & ~1:]
            return acc + parts[0].astype(F32)

        acc = lax.fori_loop(0, nkb, body, jnp.zeros((16, tq), F32))
        return jnp.sum(acc, axis=0, keepdims=True)

    def kth16(ref, target):
        def body(t, T):
            cand = T + jnp.left_shift(jnp.int32(1), 15 - t)
            return jnp.where(count16(ref, cand) >= target, cand, T)
        return lax.fori_loop(0, 16, body, jnp.full((1, tq), -32768, I32))

    def count_above(ref, T):
        return jnp.where(T >= 32767, 0.0, count16(ref, jnp.minimum(T + 1, 32767)))

    TH = kth16(kh_ref, float(n_sel))
    need = n_sel - count_above(kh_ref, TH)
    th16 = TH.astype(I16)

    def low_body(j, _):
        kl_ref[j] = jnp.where(kh_ref[j] == th16, kl_ref[j], jnp.full((), -32768, I16))
        return 0

    lax.fori_loop(0, nkb, low_body, 0)
    TL = kth16(kl_ref, need)
    room = need - count_above(kl_ref, TL)
    T = TH * 65536 + (TL + 32768)

    qcsT = stack_lanes(qcT_ref[0]) * jnp.asarray(DH ** -0.5, BF16)
    lt = lt_ref[...]

    def att_body(j, c):
        m, l, accT, eq_before = c
        k0 = pl.multiple_of(j * tk, tk)
        key = keys_ref[j]
        eq = key == T
        rank = _dot(lt, jnp.where(eq, 1.0, 0.0).astype(BF16)) + eq_before
        sel = (key > T) | (eq & (rank <= room))
        bias = jnp.where(sel & (kpos0 + k0 <= qpos), 0.0, NEG_BIG)
        s = _dot(kc_ref[pl.ds(k0, tk), :], qcsT) + jnp.concatenate([bias] * N_HEADS, axis=1)
        m_new = jnp.maximum(m, jnp.max(s, axis=0, keepdims=True))
        alpha = jnp.exp(m - m_new)
        p = jnp.exp(s - m_new)
        l = alpha * l + jnp.sum(p, axis=0, keepdims=True)
        accT = alpha * accT + _dot(vT_ref[j], p.astype(BF16))
        return m_new, l, accT, rank[tk - 1:tk, :]

    init = (jnp.full((1, M), NEG_BIG, F32), jnp.zeros((1, M), F32), jnp.zeros((W, M), F32), jnp.zeros((1, tq), F32))
    m, l, accT, _ = lax.fori_loop(0, nkb, att_body, init)
    outT = accT / l
    oT = jnp.zeros((W, tq), F32)
    for h in range(N_HEADS):
        oT = jnp.where(row_head == h, outT[:, h * tq:(h + 1) * tq], oT)
    o_ref[...] = oT.T.astype(o_ref.dtype)


def _dsa_prompt(qcT, qiT, wiT, rb, vT, B, L, *, tq, tk, n_sel):
    W = N_HEADS * DH
    nq = L // tq
    assert tk // 16 <= 256
    lt = jnp.where(jnp.arange(tk)[None, :] <= jnp.arange(tk)[:, None], 1.0, 0.0).astype(BF16)
    return pl.pallas_call(
        functools.partial(_dsa_prompt_body, tq=tq, tk=tk, n_sel=n_sel),
        grid=(B, nq),
        in_specs=[pl.BlockSpec((1, W, tq), lambda b, i: (b * nq + i, 0, 0)),
                  pl.BlockSpec((1, W, tq), lambda b, i: (b * nq + i, 0, 0)),
                  pl.BlockSpec((1, 8, tq), lambda b, i: (b * nq + i, 0, 0)),
                  pl.BlockSpec((L, W), lambda b, i: (b, 1)),
                  pl.BlockSpec((L, W), lambda b, i: (b, 3)),
                  pl.BlockSpec((L // tk, W, tk), lambda b, i: (b, 0, 0)),
                  pl.BlockSpec((tk, tk), lambda b, i: (0, 0))],
        out_specs=pl.BlockSpec((tq, W), lambda b, i: (b * nq + i, 0)),
        out_shape=jax.ShapeDtypeStruct((B * L, W), BF16),
        scratch_shapes=[pltpu.VMEM((L // tk, tk, tq), I32), pltpu.VMEM((L // tk, tk, tq), jnp.int16),
                        pltpu.VMEM((L // tk, tk, tq), jnp.int16)],
        compiler_params=_cparams(("parallel", "arbitrary")),
        name="dsa_prompt",
    )(qcT, qiT, wiT, rb, rb, vT, lt)


def _dsa_s_score_body(pt_ref, qi_ref, w_ref, kn_ref, *rest, G, n_new):
    pages = rest[:G]
    o_ref = rest[G]
    p = pl.program_id(1)
    R = 8
    qi = qi_ref[0]
    w = w_ref[0]

    def score(dots):
        d = jnp.maximum(dots * (DH ** -0.5), 0.0) * w
        return (d[0:R] + d[R:2 * R]) + (d[2 * R:3 * R] + d[3 * R:4 * R]) + 0.0

    @pl.when(p < pl.num_programs(1) - 1)
    def _():
        for g in range(G):
            o_ref[0, :, g * PAGE:(g + 1) * PAGE] = score(_dot(qi, pages[g][...].astype(BF16)))

    @pl.when(p == pl.num_programs(1) - 1)
    def _():
        t = lax.broadcasted_iota(I32, (R, PAGE), 0)
        s = lax.broadcasted_iota(I32, (R, PAGE), 1)
        o_ref[0, :, 0:PAGE] = jnp.where((s <= t) & (s < n_new), score(_dot_nt(qi, kn_ref[0])), NEG_BIG)
        if G > 1:
            o_ref[0, :, PAGE:] = jnp.full((R, (G - 1) * PAGE), NEG_BIG, F32)


def _dsa_s_select_body(s_ref, o_ref, *, n_sel):
    R = 8
    key = _sort_key(s_ref[0])
    W = key.shape[1]

    def count_ge(cand):
        m = jnp.where(key >= cand, 1.0, 0.0)
        parts = [m[:, g * LANES:(g + 1) * LANES] for g in range(W // LANES)]
        while len(parts) > 1:
            parts = [parts[a] + parts[a + 1] for a in range(0, len(parts) - 1, 2)] + parts[len(parts) & ~1:]
        return jnp.sum(parts[0], axis=1, keepdims=True)

    T = _kth_largest_key(count_ge, R, n_sel)
    room = n_sel - count_ge(T + 1)
    eq = key == T
    gt = key > T
    nt = W // LANES
    eqf = jnp.where(eq, 1.0, 0.0)
    stacked = jnp.concatenate([eqf[:, g * LANES:(g + 1) * LANES] for g in range(nt)], axis=0).astype(BF16)
    pc = _dot(stacked, _prefix_matrix(LANES))
    before = jnp.zeros((R, LANES), F32)
    tiles = []
    for g in range(nt):
        sl = slice(g * LANES, (g + 1) * LANES)
        rank = pc[g * R:(g + 1) * R, :LANES] + before
        tiles.append(jnp.where(gt[:, sl], 1.0, jnp.where(eq[:, sl] & (rank <= room), 1.0, 0.0)))
        before = before + pc[g * R:(g + 1) * R, LANES:]
    o_ref[0] = jnp.concatenate(tiles, axis=1)


def _dsa_s_attend_body(pt_ref, q_ref, kn_ref, vn_ref, mask_ref, maskn_ref, *rest, G, n_new):
    pages = rest[:G]
    o_ref, m_ref, l_ref, acc_ref = rest[G:]
    p = pl.program_id(1)
    W = N_HEADS * DH
    R = 8
    qs = q_ref[0]

    def update(s, v, m, l, acc, v_feature_major=False):
        m_new = jnp.maximum(m, jnp.max(s, axis=1, keepdims=True))
        alpha = jnp.exp(m - m_new)
        pr = jnp.exp(s - m_new)
        pv = _dot_nt(pr.astype(BF16), v) if v_feature_major else _dot(pr.astype(BF16), v)
        return m_new, alpha * l + jnp.sum(pr, axis=1, keepdims=True), alpha * acc + pv

    @pl.when(p == 0)
    def _():
        t = lax.broadcasted_iota(I32, (R, PAGE), 0)
        sidx = lax.broadcasted_iota(I32, (R, PAGE), 1)
        sel = jnp.where((sidx <= t) & (sidx < n_new), maskn_ref[0, :, 0:PAGE], 0.0)
        s = jnp.where(jnp.concatenate([sel] * N_HEADS, axis=0) > 0.5, _dot_nt(qs, kn_ref[0]), NEG_BIG)
        m, l, acc = update(s, vn_ref[0], jnp.full((N_HEADS * R, 1), NEG_BIG, F32),
                           jnp.zeros((N_HEADS * R, 1), F32), jnp.zeros((N_HEADS * R, W), F32))
        m_ref[...] = jnp.broadcast_to(m, m_ref.shape)
        l_ref[...] = jnp.broadcast_to(l, l_ref.shape)
        acc_ref[...] = acc

    kcat = jnp.concatenate([pages[g][:W, :] for g in range(G)], axis=1).astype(BF16)
    vcat = jnp.concatenate([pages[g][W:, :] for g in range(G)], axis=1).astype(BF16)
    sel = jnp.concatenate([mask_ref[0]] * N_HEADS, axis=0)
    s = jnp.where(sel > 0.5, _dot(qs, kcat), NEG_BIG)
    m, l, acc = update(s, vcat, m_ref[:, 0:1], l_ref[:, 0:1], acc_ref[...], v_feature_major=True)
    m_ref[...] = jnp.broadcast_to(m, m_ref.shape)
    l_ref[...] = jnp.broadcast_to(l, l_ref.shape)
    acc_ref[...] = acc

    @pl.when(p == pl.num_programs(1) - 1)
    def _():
        o_ref[0] = _unstack_heads(acc_ref[...] / l_ref[:, 0:1], R).astype(o_ref.dtype)


def _dsa_sample(qi32, w32, kin, qc32, kcn, vcn, cache_kidx, cache_kv, layer, page_table, *, G, n_new, n_sel):
    Bs = qi32.shape[0]
    NP = page_table.shape[1]
    W = N_HEADS * DH
    nstep = NP // G
    width = (nstep + 1) * G * PAGE

    def kidx_spec(g):
        return pl.BlockSpec((None, None, DH, PAGE),
                            lambda b, p, pt: (layer, pt[b, jnp.minimum(p * G + g, NP - 1)], 0, 0))

    scores = pl.pallas_call(
        functools.partial(_dsa_s_score_body, G=G, n_new=n_new),
        grid_spec=pltpu.PrefetchScalarGridSpec(
            num_scalar_prefetch=1, grid=(Bs, nstep + 1),
            in_specs=[pl.BlockSpec((1, 32, DH), lambda b, p, pt: (b, 0, 0)),
                      pl.BlockSpec((1, 32, LANES), lambda b, p, pt: (b, 0, 0)),
                      pl.BlockSpec((1, PAGE, DH), lambda b, p, pt: (b, 0, 0))] + [kidx_spec(g) for g in range(G)],
            out_specs=pl.BlockSpec((1, 8, G * PAGE), lambda b, p, pt: (b, 0, p))),
        out_shape=jax.ShapeDtypeStruct((Bs, 8, width), F32),
        compiler_params=_cparams(("parallel", "arbitrary")),
        name="dsa_sample_score",
    )(page_table, qi32, w32, kin, *([cache_kidx] * G))

    mask = pl.pallas_call(
        functools.partial(_dsa_s_select_body, n_sel=n_sel),
        grid=(Bs,),
        in_specs=[pl.BlockSpec((1, 8, width), lambda b: (b, 0, 0))],
        out_specs=pl.BlockSpec((1, 8, width), lambda b: (b, 0, 0)),
        out_shape=jax.ShapeDtypeStruct((Bs, 8, width), F32),
        compiler_params=_cparams(("parallel",)),
        name="dsa_sample_select",
    )(scores)

    def kv_spec(g):
        return pl.BlockSpec((None, None, 2 * W, PAGE), lambda b, p, pt: (layer, pt[b, p * G + g], 0, 0))

    return pl.pallas_call(
        functools.partial(_dsa_s_attend_body, G=G, n_new=n_new),
        grid_spec=pltpu.PrefetchScalarGridSpec(
            num_scalar_prefetch=1, grid=(Bs, nstep),
            in_specs=[pl.BlockSpec((1, 32, W), lambda b, p, pt: (b, 0, 0)),
                      pl.BlockSpec((1, PAGE, W), lambda b, p, pt: (b, 0, 0)),
                      pl.BlockSpec((1, PAGE, W), lambda b, p, pt: (b, 0, 0)),
                      pl.BlockSpec((1, 8, G * PAGE), lambda b, p, pt: (b, 0, p)),
                      pl.BlockSpec((1, 8, G * PAGE), lambda b, p, pt: (b, 0, nstep))] + [kv_spec(g) for g in range(G)],
            out_specs=pl.BlockSpec((1, 8, W), lambda b, p, pt: (b, 0, 0)),
            scratch_shapes=[pltpu.VMEM((32, LANES), F32), pltpu.VMEM((32, LANES), F32), pltpu.VMEM((32, W), F32)]),
        out_shape=jax.ShapeDtypeStruct((Bs, 8, W), BF16),
        compiler_params=_cparams(("parallel", "arbitrary")),
        name="dsa_sample_attend",
    )(page_table, qc32, kcn, vcn, mask, mask, *([cache_kv] * G))


def _merge_body(oa_ref, ob_ref, oc_ref, g0_ref, g1_ref, g2_ref, x_ref, wa_ref, wb_ref, wc_ref, wo_ref, o_ref):
    merged = (jax.nn.sigmoid(g0_ref[...]) * _dot(oa_ref[...], wa_ref[...])
              + jax.nn.sigmoid(g1_ref[...]) * _dot(ob_ref[...], wb_ref[...])
              + jax.nn.sigmoid(g2_ref[...]) * _dot(oc_ref[...], wc_ref[...]))
    o_ref[...] = x_ref[...] + _dot(merged.astype(BF16), wo_ref[...])


def _merge(oa, ob, oc, pf, x, wa, wb, wc, wo, *, tm, gate_blk):
    T, D = x.shape
    row = lambda i: (i, 0)
    const = lambda i: (0, 0)
    return pl.pallas_call(
        _merge_body,
        grid=(T // tm,),
        in_specs=[pl.BlockSpec((tm, oa.shape[1]), row), pl.BlockSpec((tm, ob.shape[1]), row),
                  pl.BlockSpec((tm, oc.shape[1]), row),
                  pl.BlockSpec((tm, D), lambda i: (i, gate_blk)), pl.BlockSpec((tm, D), lambda i: (i, gate_blk + 1)),
                  pl.BlockSpec((tm, D), lambda i: (i, gate_blk + 2)), pl.BlockSpec((tm, D), row),
                  pl.BlockSpec(wa.shape, const), pl.BlockSpec(wb.shape, const),
                  pl.BlockSpec(wc.shape, const), pl.BlockSpec(wo.shape, const)],
        out_specs=pl.BlockSpec((tm, D), row),
        out_shape=jax.ShapeDtypeStruct((T, D), F32),
        compiler_params=_cparams(("parallel",)),
        name="merge",
    )(oa, ob, oc, pf, pf, pf, x, wa, wb, wc, wo)


def _ffn_down_body(*refs, tm, seq_tiles, expanded, final_norm):
    ua_ref, ub_ref, ha_ref, hb_ref = refs[:4]
    rest = refs[4:]
    if expanded:
        ha2_ref, hb2_ref, t_ref = rest[:3]
        rest = rest[3:]
    cwa_ref, cwb_ref, cba_ref, cbb_ref, wd_ref, x_ref = rest[:6]
    rest = rest[6:]
    if final_norm:
        gf_ref = rest[0]
        rest = rest[1:]
    o_ref, acc_ref = rest
    i = pl.program_id(0)
    k = pl.program_id(1)

    @pl.when(k == 0)
    def _():
        acc_ref[...] = jnp.zeros_like(acc_ref)

    def conv(u_ref, h_ref, h2_ref, cw_ref, cb_ref):
        u = u_ref[...]
        row = lax.broadcasted_iota(I32, u.shape, 0)
        r1 = pltpu.roll(u, 1, axis=0)
        r2 = pltpu.roll(u, 2, axis=0)
        if expanded:
            t = t_ref[...]
            u1 = jnp.where(t >= 1, r1, h_ref[...])
            u2 = jnp.where(t >= 2, r2, h2_ref[...])
        else:
            h = h_ref[...]
            h = jnp.where(i % seq_tiles == 0, jnp.zeros_like(h), h)
            u1 = jnp.where(row == 0, h[7:8], r1)
            u2 = jnp.where(row == 0, h[6:7], jnp.where(row == 1, h[7:8], r2))
        cw = cw_ref[...]
        return cb_ref[...] + cw[0:1] * u2 + cw[1:2] * u1 + cw[2:3] * u

    a = conv(ua_ref, ha_ref, ha2_ref if expanded else None, cwa_ref, cba_ref)
    b = conv(ub_ref, hb_ref, hb2_ref if expanded else None, cwb_ref, cbb_ref)
    acc_ref[...] += _dot((_silu(a) * b).astype(BF16), wd_ref[...])

    @pl.when(k == pl.num_programs(1) - 1)
    def _():
        y = x_ref[...] + acc_ref[...]
        if final_norm:
            ms = jnp.mean(y * y, axis=-1, keepdims=True)
            y = y * lax.rsqrt(ms + EPS) * gf_ref[...]
        o_ref[...] = y


def _ffn_down(u, x, cw, cb, wd, *, tm, tkf, seq_len, prev=None, g_final=None):
    T, D = x.shape
    F = wd.shape[0]
    nk = F // tkf
    expanded = prev is not None
    final_norm = g_final is not None
    seq_tiles = max(seq_len // tm, 1)
    hb8 = tm // 8
    ua = pl.BlockSpec((tm, tkf), lambda i, k: (i, k))
    ub = pl.BlockSpec((tm, tkf), lambda i, k: (i, nk + k))
    in_specs = [ua, ub]
    args = [u, u]
    if expanded:
        p1, p2, tpos = prev
        in_specs += [ua, ub, ua, ub, pl.BlockSpec((tm, 1), lambda i, k: (i, 0))]
        args += [p1, p1, p2, p2, tpos]
    else:
        in_specs += [pl.BlockSpec((8, tkf), lambda i, k: (jnp.maximum(i * hb8 - 1, 0), k)),
                     pl.BlockSpec((8, tkf), lambda i, k: (jnp.maximum(i * hb8 - 1, 0), nk + k))]
        args += [u, u]
    in_specs += [pl.BlockSpec((3, tkf), lambda i, k: (0, k)), pl.BlockSpec((3, tkf), lambda i, k: (0, nk + k)),
                 pl.BlockSpec((1, tkf), lambda i, k: (0, k)), pl.BlockSpec((1, tkf), lambda i, k: (0, nk + k)),
                 pl.BlockSpec((tkf, D), lambda i, k: (k, 0)), pl.BlockSpec((tm, D), lambda i, k: (i, 0))]
    args += [cw, cw, cb.reshape(1, -1), cb.reshape(1, -1), wd, x]
    if final_norm:
        in_specs.append(pl.BlockSpec((1, D), lambda i, k: (0, 0)))
        args.append(g_final.reshape(1, D))
    return pl.pallas_call(
        functools.partial(_ffn_down_body, tm=tm, seq_tiles=seq_tiles, expanded=expanded, final_norm=final_norm),
        grid=(T // tm, nk),
        in_specs=in_specs,
        out_specs=pl.BlockSpec((tm, D), lambda i, k: (i, 0)),
        out_shape=jax.ShapeDtypeStruct((T, D), F32),
        scratch_shapes=[pltpu.VMEM((tm, D), F32)],
        compiler_params=_cparams(("parallel", "arbitrary")),
        name="ffn_down",
    )(*args)


def _rotate_half_cols(w):
    D, N = w.shape
    w4 = w.reshape(D, N // DH, 2, DH // 2)
    return jnp.concatenate([-w4[:, :, 1], w4[:, :, 0]], axis=-1).reshape(D, N)


def _rope_tables(pos, width):
    half = DH // 2
    inv_freq = ROPE_THETA ** (-jnp.arange(half, dtype=F32) / half)
    ang = pos.astype(F32)[:, None] * inv_freq[None, :]
    reps = width // half
    return jnp.tile(jnp.cos(ang), (1, reps)), jnp.tile(jnp.sin(ang), (1, reps))


def _pick(n, prefs):
    for t in prefs:
        if n % t == 0:
            return t
    return n


def kernel(x_prompt, x_sample, cache_kv_a, cache_kv_c, cache_kidx_c, state_hgrn, state_ffn_conv, page_table,
           norm_mix, w_in, hgrn_lb_logits, hgrn_gnorm, w_br_a, w_br_b, w_br_c, w_out, norm_ffn, w_up, conv_w,
           conv_b, w_down, norm_final):
    B, L, D = x_prompt.shape
    Bs, Ls, _ = x_sample.shape
    depth = w_in.shape[0]
    NP = page_table.shape[1]
    past = NP * PAGE
    F = w_down.shape[1]
    WA = N_HEADS * DH
    WB = N_HEADS * DK
    Tp, Ts = B * L, Bs * Ls
    n_pool = cache_kv_a.shape[1]

    lb_soft = jax.nn.softmax(hgrn_lb_logits.astype(F32), axis=0)
    lb_all = jnp.cumsum(lb_soft, axis=0) - lb_soft[0]

    sizes = (WA, WA, WA, WB, WB, WB, WB, WA, WA, WA, N_HEADS * DH, DH, N_HEADS, 3 * D)
    offs = [0]
    for s in sizes:
        offs.append(offs[-1] + s)
    (o_qa, o_ka, o_va, o_qh, o_fh, o_ih, o_gh, o_qc, o_kc, o_vc, o_qi, o_ki, o_wi, o_gt, o_end) = offs
    QA_BLK, KA_BLK, VA_BLK, VC_BLK = (4 * WB) // WA, (4 * WB) // WA + 1, (4 * WB) // WA + 2, (4 * WB) // WA + 3
    gate_off = 4 * WB + 4 * WA
    assert gate_off % D == 0
    GATE_BLK = gate_off // D
    NPJ = gate_off + 3 * D
    N_ROPE_TILES = 4
    NRJ = (N_ROPE_TILES + 1) * WA
    wi_off = N_ROPE_TILES * WA
    WI_BLK = wi_off // LANES

    cache_a = cache_kv_a.transpose(0, 1, 3, 4, 5, 2).reshape(depth, n_pool, 2 * WA, PAGE)
    cache_c = cache_kv_c.transpose(0, 1, 3, 4, 5, 2).reshape(depth, n_pool, 2 * WA, PAGE)
    cache_i = cache_kidx_c.transpose(0, 1, 3, 2)

    cos_p, sin_p = _rope_tables(jnp.arange(L, dtype=I32), WA)
    cos_s, sin_s = _rope_tables(past + (jnp.arange(Ts, dtype=I32) % Ls), WA)

    tm_p = _pick(Tp, (1024, 512, 256, 128))
    tm_p = min(tm_p, L)
    tn_main = _pick(NPJ, (512, 256, 128))
    tn_up = _pick(2 * F, (512, 256, 128))
    tkf = _pick(F, (1408, 256, 128))
    tq_a = _pick(L, (256, 128))
    tq_c = _pick(L, (256, 128))
    tk_c = _pick(L, (512, 256, 128))
    C_h = _pick(L, (64,))
    G = _pick(NP, (32, 16, 8, 4, 2, 1))
    n_sel_p = min(TOPK_MAX, L // 4)
    n_sel_s = min(TOPK_MAX, (past + Ls) // 4)

    xp = x_prompt.reshape(Tp, D)
    xs = x_sample.reshape(Ts, D)
    tpos_s = (jnp.arange(Ts, dtype=I32) % Ls).reshape(Ts, 1)

    def pad_rows(a, rows):
        return jnp.pad(a, ((0, 0), (0, rows - a.shape[1]), (0, 0)))

    outs_p = ([], [], [], [], [])
    outs_s = ([], [], [], [], [])
    for l in range(depth):
        w = w_in[l]
        col = lambda o, n: w[:, o:o + n]
        w_main = jnp.concatenate(
            [col(o_qh, 4 * WB), col(o_qa, 3 * WA), col(o_vc, WA), col(o_gt, 3 * D)], axis=1).astype(BF16)
        w_r = jnp.concatenate([col(o_qc, WA), col(o_kc, WA), col(o_qi, WA)] + [col(o_ki, DH)] * N_HEADS, axis=1)
        w_tail = jnp.concatenate([col(o_wi, N_HEADS), jnp.zeros((D, WA - N_HEADS), w.dtype)], axis=1)
        w_rot = jnp.concatenate([_rotate_half_cols(w_r), jnp.zeros((D, WA), w.dtype)], axis=1).astype(BF16)
        w_r = jnp.concatenate([w_r, w_tail], axis=1).astype(BF16)
        wa_b, wb_b, wc_b, wo_b = (t[l].astype(BF16) for t in (w_br_a, w_br_b, w_br_c, w_out))
        wup_b = w_up[l].astype(BF16)
        wd_b = w_down[l].astype(BF16)
        last = l == depth - 1

        vc_col = VC_BLK * WA
        pf, pb, vT = _rms_proj(xp, norm_mix[l], w_main, tm=tm_p, tn=tn_main,
                               t_outs=((vc_col // tn_main, vc_col % tn_main, WA, WA, tk_c, BF16),))
        rf, rb, qcT, qiT, wiT = _rms_proj(
            xp, norm_mix[l], w_r, tm=tm_p, tn=WA, rope_args=(w_rot, cos_p, sin_p), n_rope_tiles=N_ROPE_TILES,
            t_outs=((0, 0, WA, WA, tq_c, BF16), (2, 0, WA, WA, tq_c, BF16), (N_ROPE_TILES, 0, LANES, 8, tq_c, F32)))
        oa = _attn_a_prompt(pb, B, L, tq=tq_a, q_blk=QA_BLK, k_blk=KA_BLK, v_blk=VA_BLK)
        ob, s_new = _hgrn(pf.reshape(B, L, NPJ), lb_all[l], hgrn_gnorm[l], None, C=C_h, c=min(16, C_h), valid=C_h)
        oc = _dsa_prompt(qcT, qiT, wiT, rb, vT, B, L, tq=tq_c, tk=tk_c, n_sel=n_sel_p)
        x1 = _merge(oa, ob.reshape(Tp, WB), oc, pf, xp, wa_b, wb_b, wc_b, wo_b, tm=min(512, tm_p), gate_blk=GATE_BLK)
        u = _rms_proj(x1, norm_ffn[l], wup_b, tm=tm_p, tn=tn_up, emit_bf16=False)
        xp = _ffn_down(u, x1, conv_w[l], conv_b[l], wd_b, tm=min(512, tm_p), tkf=tkf, seq_len=L,
                       g_final=norm_final if last else None)
        kv_off = (KA_BLK * WA)
        outs_p[0].append(pf[:, kv_off:kv_off + 2 * WA].reshape(B, L, 2, N_HEADS, DH))
        outs_p[1].append(jnp.stack([rf[:, WA:2 * WA].reshape(B, L, N_HEADS, DH),
                                    pf[:, VC_BLK * WA:(VC_BLK + 1) * WA].reshape(B, L, N_HEADS, DH)], axis=2))
        outs_p[2].append(rf[:, 3 * WA:3 * WA + DH].reshape(B, L, DH))
        outs_p[3].append(s_new)
        outs_p[4].append(u.reshape(B, L, 2 * F)[:, L - 2:])

        pf, pb = _rms_proj(xs, norm_mix[l], w_main, tm=Ts, tn=tn_main)
        rf, rb = _rms_proj(xs, norm_mix[l], w_r, tm=Ts, tn=WA, rope_args=(w_rot, cos_s, sin_s),
                           n_rope_tiles=N_ROPE_TILES)
        pb3 = pb.reshape(Bs, Ls, NPJ)
        rb3 = rb.reshape(Bs, Ls, NRJ)
        blk = lambda a, k: a[:, :, k * WA:(k + 1) * WA]
        oa8 = _attn_a_sample(pad_rows(blk(pb3, QA_BLK), 8), pad_rows(blk(pb3, KA_BLK), PAGE),
                             pad_rows(blk(pb3, VA_BLK), PAGE), cache_a, l, page_table, G=G)
        ph = jnp.pad(pf[:, :4 * WB].reshape(Bs, Ls, 4 * WB), ((0, 0), (0, 8 - Ls), (0, 0)))
        ob8, s_new = _hgrn(ph, lb_all[l], hgrn_gnorm[l], state_hgrn[l], C=8, c=8, valid=Ls)
        qi = blk(rb3, 2).reshape(Bs, Ls, N_HEADS, DH).transpose(0, 2, 1, 3)
        qi32 = jnp.pad(qi, ((0, 0), (0, 0), (0, 8 - Ls), (0, 0))).reshape(Bs, 4 * 8, DH)
        wi = rf[:, wi_off:wi_off + N_HEADS].reshape(Bs, Ls, N_HEADS).transpose(0, 2, 1) * (N_HEADS ** -0.5)
        w32 = jnp.broadcast_to(jnp.pad(wi, ((0, 0), (0, 0), (0, 8 - Ls))).reshape(Bs, 32, 1), (Bs, 32, LANES))
        kin = pad_rows(rb3[:, :, 3 * WA:3 * WA + DH], PAGE)
        qc8 = pad_rows(blk(rb3, 0), 8)
        head = (jnp.arange(WA) // DH)[None, None, None, :] == jnp.arange(N_HEADS)[None, :, None, None]
        qc32 = (jnp.where(head, qc8[:, None], 0) * jnp.asarray(DH ** -0.5, BF16)).reshape(Bs, 32, WA).astype(BF16)
        oc8 = _dsa_sample(qi32, w32, kin, qc32, pad_rows(blk(rb3, 1), PAGE), pad_rows(blk(pb3, VC_BLK), PAGE),
                          cache_i, cache_c, l, page_table, G=G, n_new=Ls, n_sel=n_sel_s)
        x1 = _merge(oa8[:, :Ls].reshape(Ts, WA), ob8[:, :Ls].reshape(Ts, WB), oc8[:, :Ls].reshape(Ts, WA), pf, xs,
                    wa_b, wb_b, wc_b, wo_b, tm=Ts, gate_blk=GATE_BLK)
        u = _rms_proj(x1, norm_ffn[l], wup_b, tm=Ts, tn=tn_up, emit_bf16=False)
        prev = state_ffn_conv[l]
        u3 = u.reshape(Bs, Ls, 2 * F)
        p1 = jnp.broadcast_to(prev[:, 1:2], (Bs, Ls, 2 * F)).reshape(Ts, 2 * F)
        p2 = jnp.concatenate([prev, jnp.zeros((Bs, Ls - 2, 2 * F), F32)], axis=1).reshape(Ts, 2 * F)
        xs = _ffn_down(u, x1, conv_w[l], conv_b[l], wd_b, tm=Ts, tkf=tkf, seq_len=Ls, prev=(p1, p2, tpos_s),
                       g_final=norm_final if last else None)
        kv_off = (KA_BLK * WA)
        outs_s[0].append(pf[:, kv_off:kv_off + 2 * WA].reshape(Bs, Ls, 2, N_HEADS, DH))
        outs_s[1].append(jnp.stack([rf[:, WA:2 * WA].reshape(Bs, Ls, N_HEADS, DH),
                                    pf[:, VC_BLK * WA:(VC_BLK + 1) * WA].reshape(Bs, Ls, N_HEADS, DH)], axis=2))
        outs_s[2].append(rf[:, 3 * WA:3 * WA + DH].reshape(Bs, Ls, DH))
        outs_s[3].append(s_new)
        outs_s[4].append(jnp.concatenate([prev, u3], axis=1)[:, Ls:])

    st = lambda xs_: jnp.stack(xs_, axis=0)
    return (xp.reshape(B, L, D), xs.reshape(Bs, Ls, D),
            st(outs_p[0]), st(outs_s[0]), st(outs_p[1]), st(outs_s[1]), st(outs_p[2]), st(outs_s[2]),
            st(outs_p[3]), st(outs_s[3]), st(outs_p[4]), st(outs_s[4]))
```
